```python
import math
import jax, jax.numpy as jnp
from jax import lax
import numpy as np

D_MODEL = 2048
BATCH = 8
SEQ = 2048
DEPTH = 1
DEC_BATCH = 128
DEC_SEQ = 4
PAST_LEN = 8192
PAGE_SIZE = 128

ATTN_WIDTH = D_MODEL // 2
SSM_WIDTH = D_MODEL - ATTN_WIDTH
HEAD_DIM = 64
N_HEADS = ATTN_WIDTH // HEAD_DIM
N_KV_HEADS = 4
Q_PER_KV = N_HEADS // N_KV_HEADS
KV_WIDTH = N_KV_HEADS * HEAD_DIM
WINDOW = 128
BLOCK = 128
SSM_GROUP = 16
N_SSM_GROUPS = SSM_WIDTH // SSM_GROUP
STATE_DIM = 64
D_FF = 4 * D_MODEL
PROJ_WIDTH = ATTN_WIDTH + 2 * KV_WIDTH + SSM_WIDTH
EPS = 1e-6
NEG_INF = -1e30
DT_MIN = 1e-3
DT_MAX = 1e-1

kernel_name = 'hymba_swa_sink_s5_step'


def rmsnorm(x, g):
    xf = x.astype(jnp.float32)
    r = lax.rsqrt(jnp.mean(xf * xf, axis=-1, keepdims=True) + EPS)
    return (xf * r * g.astype(jnp.float32)).astype(x.dtype)


def alibi_slopes():
    h = jnp.arange(1, N_HEADS + 1, dtype=jnp.float32)
    return jnp.exp2(-8.0 * h / N_HEADS).reshape(N_KV_HEADS, Q_PER_KV)


def split_projection(z):
    n, t = z.shape[:2]
    o1 = ATTN_WIDTH
    o2 = o1 + KV_WIDTH
    o3 = o2 + KV_WIDTH
    q = z[..., :o1].reshape(n, t, N_KV_HEADS, Q_PER_KV, HEAD_DIM)
    k = z[..., o1:o2].reshape(n, t, N_KV_HEADS, HEAD_DIM)
    v = z[..., o2:o3].reshape(n, t, N_KV_HEADS, HEAD_DIM)
    u = z[..., o3:]
    return q, k, v, u


def band_probs(scores, delta, valid, slopes, sinks):
    s = scores - slopes[:, :, None, None] * delta.astype(jnp.float32)
    s = jnp.where(valid, s, NEG_INF)
    sink = sinks[:, :, None]
    m = jnp.maximum(jnp.max(s, axis=-1), sink)
    p = jnp.exp(s - m[..., None])
    denom = jnp.sum(p, axis=-1) + jnp.exp(sink - m)
    return p / denom[..., None]


def prompt_window_attention(q, k, v, slopes, sinks):
    b, s = q.shape[:2]
    nb = s // BLOCK
    qb = q.reshape(b, nb, BLOCK, N_KV_HEADS, Q_PER_KV, HEAD_DIM)
    pad = jnp.zeros((b, BLOCK, N_KV_HEADS, HEAD_DIM), k.dtype)
    kp = jnp.concatenate([pad, k], axis=1).reshape(b, nb + 1, BLOCK, N_KV_HEADS, HEAD_DIM)
    vp = jnp.concatenate([pad, v], axis=1).reshape(b, nb + 1, BLOCK, N_KV_HEADS, HEAD_DIM)
    kb = jnp.concatenate([kp[:, :-1], kp[:, 1:]], axis=2)
    vb = jnp.concatenate([vp[:, :-1], vp[:, 1:]], axis=2)
    scores = jnp.einsum('bnqkrd,bnskd->bnkrqs', qb, kb,
                        preferred_element_type=jnp.float32) * (HEAD_DIM ** -0.5)
    i = jnp.arange(BLOCK)[:, None]
    j = jnp.arange(2 * BLOCK)[None, :]
    delta = i + BLOCK - j
    key_pos = jnp.arange(nb)[:, None, None] * BLOCK - BLOCK + j[None]
    valid = (delta >= 0) & (delta <= WINDOW) & (key_pos >= 0)
    p = band_probs(scores, delta, valid[:, None, None], slopes, sinks)
    out = jnp.einsum('bnkrqs,bnskd->bnqkrd', p.astype(v.dtype), vb)
    return out.reshape(b, s, ATTN_WIDTH)


def sample_window_attention(q, k_new, v_new, cache_k, cache_v, slopes, sinks):
    n, t = q.shape[:2]
    k_all = jnp.concatenate([cache_k.astype(k_new.dtype), k_new], axis=1)
    v_all = jnp.concatenate([cache_v.astype(v_new.dtype), v_new], axis=1)
    scores = jnp.einsum('nqkrd,nskd->nkrqs', q, k_all,
                        preferred_element_type=jnp.float32) * (HEAD_DIM ** -0.5)
    i = jnp.arange(t)[:, None]
    j = jnp.arange(WINDOW + t)[None, :]
    delta = i + WINDOW - j
    valid = (delta >= 0) & (delta <= WINDOW)
    p = band_probs(scores, delta, valid, slopes, sinks)
    out = jnp.einsum('nkrqs,nskd->nqkrd', p.astype(v_all.dtype), v_all)
    return out.reshape(n, t, ATTN_WIDTH), k_all[:, -WINDOW:], v_all[:, -WINDOW:]


def ssm_discretize(A_re, A_im, log_dt, B_re, B_im):
    A = lax.complex(A_re.astype(jnp.float32), A_im.astype(jnp.float32))
    dt = jnp.exp(log_dt.astype(jnp.float32))[:, None]
    A_bar = jnp.exp(A * dt)
    B = lax.complex(B_re.astype(jnp.float32), B_im.astype(jnp.float32))
    B_bar = ((A_bar - 1.0) / A)[..., None] * B
    return A_bar, B_bar


def ssm_combine(e1, e2):
    a1, b1 = e1
    a2, b2 = e2
    return a1 * a2, a2 * b1 + b2


def ssm_mixer(u, h0_re, h0_im, A_bar, B_bar, C_re, C_im, D_skip):
    n, t = u.shape[:2]
    uf = u.astype(jnp.float32).reshape(n, t, N_SSM_GROUPS, SSM_GROUP)
    bu = jnp.einsum('ntgh,gph->ntgp', uf.astype(jnp.complex64), B_bar)
    h0 = lax.complex(h0_re.astype(jnp.float32), h0_im.astype(jnp.float32))
    bu = bu.at[:, 0].add(A_bar * h0)
    a = jnp.broadcast_to(A_bar, bu.shape)
    _, h = lax.associative_scan(ssm_combine, (a, bu), axis=1)
    C = lax.complex(C_re.astype(jnp.float32), C_im.astype(jnp.float32))
    y = jnp.real(jnp.einsum('ntgp,ghp->ntgh', h, C)) \
        + D_skip.astype(jnp.float32).reshape(N_SSM_GROUPS, SSM_GROUP) * uf
    h_last = h[:, -1]
    return y.reshape(n, t, SSM_WIDTH), jnp.real(h_last), jnp.imag(h_last)


def hybrid_layer(x, cache_k, cache_v, h0_re, h0_im, p):
    t = x.shape[1]
    z = rmsnorm(x, p['attn_norm_g']) @ p['w_in']
    q, k, v, u = split_projection(z)
    q = rmsnorm(q, p['q_norm_g'])
    k = rmsnorm(k, p['k_norm_g'])
    slopes = alibi_slopes()
    sinks = p['attn_sinks'].astype(jnp.float32).reshape(N_KV_HEADS, Q_PER_KV)
    if cache_k is None:
        attn = prompt_window_attention(q, k, v, slopes, sinks)
        new_k = k[:, t - WINDOW:]
        new_v = v[:, t - WINDOW:]
    else:
        attn, new_k, new_v = sample_window_attention(q, k, v, cache_k, cache_v, slopes, sinks)
    A_bar, B_bar = ssm_discretize(p['ssm_A_re'], p['ssm_A_im'], p['ssm_log_dt'],
                                  p['ssm_B_re'], p['ssm_B_im'])
    y_ssm, h_re, h_im = ssm_mixer(u, h0_re, h0_im, A_bar, B_bar,
                                  p['ssm_C_re'], p['ssm_C_im'], p['ssm_D'])
    g = jax.nn.gelu(y_ssm)
    ssm_out = (g * jax.nn.sigmoid(g @ p['w_glu'].astype(jnp.float32)
                                  + p['b_glu'].astype(jnp.float32))).astype(x.dtype)
    mix = jnp.concatenate([rmsnorm(attn, p['attn_out_g']),
                           rmsnorm(ssm_out, p['ssm_out_g'])], axis=-1)
    h = x + mix @ p['w_out']
    y = h + jnp.square(jax.nn.relu(rmsnorm(h, p['mlp_norm_g']) @ p['w_mlp_up'])) @ p['w_mlp_down']
    return y, new_k, new_v, h_re.astype(h0_re.dtype), h_im.astype(h0_im.dtype)


def setup_inputs(seed: int = 0) -> dict:
    key = jax.random.key(seed)
    ks = jax.random.split(key, 32)
    f32 = jnp.float32
    nrm = lambda k, shape, s: jax.random.normal(k, shape, f32) * s
    gain = lambda k, shape: 1.0 + 0.05 * jax.random.normal(k, shape, f32)
    L = DEPTH
    G, P, H = N_SSM_GROUPS, STATE_DIM, SSM_GROUP
    a_im_base = math.pi * jnp.arange(P, dtype=f32)
    return {
        'x_prompt': nrm(ks[0], (BATCH, SEQ, D_MODEL), 1.0),
        'x_sample': nrm(ks[1], (DEC_BATCH, DEC_SEQ, D_MODEL), 1.0),
        'cache_k': nrm(ks[2], (L, DEC_BATCH, WINDOW, N_KV_HEADS, HEAD_DIM), 1.0),
        'cache_v': nrm(ks[3], (L, DEC_BATCH, WINDOW, N_KV_HEADS, HEAD_DIM), 1.0),
        'state_ssm_re': nrm(ks[4], (L, DEC_BATCH, G, P), 0.3),
        'state_ssm_im': nrm(ks[5], (L, DEC_BATCH, G, P), 0.3),
        'attn_norm_g': gain(ks[6], (L, D_MODEL)),
        'w_in': nrm(ks[7], (L, D_MODEL, PROJ_WIDTH), D_MODEL ** -0.5),
        'q_norm_g': gain(ks[8], (L, HEAD_DIM)),
        'k_norm_g': gain(ks[9], (L, HEAD_DIM)),
        'attn_sinks': nrm(ks[10], (L, N_HEADS), 1.0),
        'ssm_A_re': -0.5 + 0.01 * jax.random.normal(ks[11], (L, G, P), f32),
        'ssm_A_im': a_im_base + 0.01 * jax.random.normal(ks[12], (L, G, P), f32),
        'ssm_log_dt': jax.random.uniform(ks[13], (L, G), f32, math.log(DT_MIN), math.log(DT_MAX)),
        'ssm_B_re': nrm(ks[14], (L, G, P, H), (2 * H) ** -0.5),
        'ssm_B_im': nrm(ks[15], (L, G, P, H), (2 * H) ** -0.5),
        'ssm_C_re': nrm(ks[16], (L, G, H, P), (2 * P) ** -0.5),
        'ssm_C_im': nrm(ks[17], (L, G, H, P), (2 * P) ** -0.5),
        'ssm_D': nrm(ks[18], (L, SSM_WIDTH), 1.0),
        'w_glu': nrm(ks[19], (L, SSM_WIDTH, SSM_WIDTH), SSM_WIDTH ** -0.5),
        'b_glu': nrm(ks[20], (L, SSM_WIDTH), 0.01),
        'attn_out_g': gain(ks[21], (L, ATTN_WIDTH)),
        'ssm_out_g': gain(ks[22], (L, SSM_WIDTH)),
        'w_out': nrm(ks[23], (L, D_MODEL, D_MODEL), D_MODEL ** -0.5),
        'mlp_norm_g': gain(ks[24], (L, D_MODEL)),
        'w_mlp_up': nrm(ks[25], (L, D_MODEL, D_FF), D_MODEL ** -0.5),
        'w_mlp_down': nrm(ks[26], (L, D_FF, D_MODEL), D_FF ** -0.5),
    }


def reference(x_prompt, x_sample, cache_k, cache_v, state_ssm_re, state_ssm_im,
              attn_norm_g, w_in, q_norm_g, k_norm_g, attn_sinks,
              ssm_A_re, ssm_A_im, ssm_log_dt, ssm_B_re, ssm_B_im, ssm_C_re, ssm_C_im, ssm_D,
              w_glu, b_glu, attn_out_g, ssm_out_g, w_out, mlp_norm_g, w_mlp_up, w_mlp_down):
    xp = x_prompt
    xs = x_sample
    zeros_state = jnp.zeros((x_prompt.shape[0], N_SSM_GROUPS, STATE_DIM), jnp.float32)
    nkp, nvp, nrp, nip = [], [], [], []
    nks, nvs, nrs, nis = [], [], [], []
    for l in range(DEPTH):
        p = dict(attn_norm_g=attn_norm_g[l], w_in=w_in[l], q_norm_g=q_norm_g[l],
                 k_norm_g=k_norm_g[l], attn_sinks=attn_sinks[l],
                 ssm_A_re=ssm_A_re[l], ssm_A_im=ssm_A_im[l], ssm_log_dt=ssm_log_dt[l],
                 ssm_B_re=ssm_B_re[l], ssm_B_im=ssm_B_im[l],
                 ssm_C_re=ssm_C_re[l], ssm_C_im=ssm_C_im[l], ssm_D=ssm_D[l],
                 w_glu=w_glu[l], b_glu=b_glu[l], attn_out_g=attn_out_g[l],
                 ssm_out_g=ssm_out_g[l], w_out=w_out[l], mlp_norm_g=mlp_norm_g[l],
                 w_mlp_up=w_mlp_up[l], w_mlp_down=w_mlp_down[l])
        xp, kp, vp, hrp, hip = hybrid_layer(xp, None, None, zeros_state, zeros_state, p)
        xs, ks_, vs_, hrs, his = hybrid_layer(xs, cache_k[l], cache_v[l],
                                              state_ssm_re[l], state_ssm_im[l], p)
        nkp.append(kp); nvp.append(vp); nrp.append(hrp); nip.append(hip)
        nks.append(ks_); nvs.append(vs_); nrs.append(hrs); nis.append(his)
    return (xp, xs,
            jnp.stack(nkp), jnp.stack(nvp), jnp.stack(nrp), jnp.stack(nip),
            jnp.stack(nks), jnp.stack(nvs), jnp.stack(nrs), jnp.stack(nis))
```

```python
import functools
import math

import jax
import jax.numpy as jnp
from jax import lax
from jax.experimental import pallas as pl
from jax.experimental.pallas import tpu as pltpu

D_MODEL = 2048
ATTN_WIDTH = 1024
SSM_WIDTH = 1024
HEAD_DIM = 64
N_HEADS = 16
N_KV_HEADS = 4
Q_PER_KV = 4
KV_WIDTH = 256
WINDOW = 128
SSM_GROUP = 16
N_SSM_GROUPS = 64
STATE_DIM = 64
N_STATE = N_SSM_GROUPS * STATE_DIM
D_FF = 8192
PROJ_WIDTH = ATTN_WIDTH + 2 * KV_WIDTH + SSM_WIDTH
EPS = 1e-6
NEG_INF = -1e30

LANES = 128
SUBLANES = 8
MXU_DIM = 256
VMEM_LIMIT = 56 * 1024 * 1024

GROUPS_PER_BLOCK = MXU_DIM // SSM_GROUP
N_SSM_BLOCKS = N_SSM_GROUPS // GROUPS_PER_BLOCK
BLOCK_STATES = GROUPS_PER_BLOCK * STATE_DIM
SCAN_LANES = 512

F32 = jnp.float32
BF16 = jnp.bfloat16


def _cparams(sem):
    return pltpu.CompilerParams(dimension_semantics=sem, vmem_limit_bytes=VMEM_LIMIT)


def _const_spec(shape):
    nd = len(shape)
    return pl.BlockSpec(shape, lambda *_: (0,) * nd, pipeline_mode=pl.Buffered(1))


def _discretize_kernel(are_ref, aim_ref, ldt_ref, bre_ref, bim_ref,
                       abr_ref, abi_ref, bbr_ref, bbi_ref):
    a_re = are_ref[...]
    a_im = aim_ref[...]
    dt = jnp.exp(ldt_ref[...])
    mag = jnp.exp(a_re * dt)
    ab_re = mag * jnp.cos(a_im * dt)
    ab_im = mag * jnp.sin(a_im * dt)
    abr_ref[...] = ab_re
    abi_ref[...] = ab_im
    x = ab_re - 1.0
    y = ab_im
    den = a_re * a_re + a_im * a_im
    c_re = (x * a_re + y * a_im) / den
    c_im = (y * a_re - x * a_im) / den
    b_re = bre_ref[...]
    b_im = bim_ref[...]
    bbr_ref[...] = c_re * b_re - c_im * b_im
    bbi_ref[...] = c_re * b_im + c_im * b_re


def _ssm_discretize(a_re, a_im, log_dt, b_re, b_im):
    row = jax.ShapeDtypeStruct((1, N_STATE), F32)
    mat = jax.ShapeDtypeStruct((SSM_GROUP, N_STATE), F32)
    return pl.pallas_call(
        _discretize_kernel,
        out_shape=(row, row, mat, mat),
        name="ssm_discretize",
    )(a_re, a_im, log_dt, b_re, b_im)


def _head_rmsnorm(zc, gain, ones_blk):
    sq = zc * zc
    hi = sq.astype(BF16)
    lo = (sq - hi.astype(F32)).astype(BF16)
    ss = (jnp.dot(hi, ones_blk, preferred_element_type=F32)
          + jnp.dot(lo, ones_blk, preferred_element_type=F32))
    return zc * lax.rsqrt(ss * (1.0 / HEAD_DIM) + EPS) * gain


def _in_proj_kernel(x_ref, gn_ref, w_ref, gq_ref, gk_ref, q_ref, kv_ref, u_ref):
    x = x_ref[...]
    r = lax.rsqrt(jnp.mean(x * x, axis=-1, keepdims=True) + EPS)
    xn = (x * r * gn_ref[...]).astype(BF16)
    z = jnp.dot(xn, w_ref[...], preferred_element_type=F32)
    ri = lax.broadcasted_iota(jnp.int32, (MXU_DIM, MXU_DIM), 0) // HEAD_DIM
    ci = lax.broadcasted_iota(jnp.int32, (MXU_DIM, MXU_DIM), 1) // HEAD_DIM
    ones_blk = (ri == ci).astype(BF16)
    gq = gq_ref[...]
    for c in range(ATTN_WIDTH // MXU_DIM):
        sl = slice(c * MXU_DIM, (c + 1) * MXU_DIM)
        q_ref[:, sl] = _head_rmsnorm(z[:, sl], gq, ones_blk).astype(BF16)
    k0 = ATTN_WIDTH
    v0 = k0 + KV_WIDTH
    u0 = v0 + KV_WIDTH
    kv_ref[:, :KV_WIDTH] = _head_rmsnorm(z[:, k0:v0], gk_ref[...], ones_blk)
    kv_ref[:, KV_WIDTH:] = z[:, v0:u0]
    u_ref[...] = z[:, u0:]


def _in_proj(x2d, gn, w_in_b, gq, gk, tm):
    rows = x2d.shape[0]
    return pl.pallas_call(
        _in_proj_kernel,
        grid=(rows // tm,),
        in_specs=[
            pl.BlockSpec((tm, D_MODEL), lambda i: (i, 0)),
            _const_spec((1, D_MODEL)),
            _const_spec((D_MODEL, PROJ_WIDTH)),
            _const_spec((1, MXU_DIM)),
            _const_spec((1, MXU_DIM)),
        ],
        out_specs=[
            pl.BlockSpec((tm, ATTN_WIDTH), lambda i: (i, 0)),
            pl.BlockSpec((tm, 2 * KV_WIDTH), lambda i: (i, 0)),
            pl.BlockSpec((tm, SSM_WIDTH), lambda i: (i, 0)),
        ],
        out_shape=(
            jax.ShapeDtypeStruct((rows, ATTN_WIDTH), BF16),
            jax.ShapeDtypeStruct((rows, 2 * KV_WIDTH), F32),
            jax.ShapeDtypeStruct((rows, SSM_WIDTH), F32),
        ),
        compiler_params=_cparams(("parallel",)),
        name="in_proj",
    )(x2d, gn, w_in_b, gq, gk)


def _alibi_slope(head):
    return 2.0 ** (-8.0 * (head + 1) / N_HEADS)


def _pair_attention(q_blocks, kc, vc, delta, valid, sink_ref, pair):
    tq = q_blocks[0].shape[0]
    lane = lax.broadcasted_iota(jnp.int32, (tq, LANES), 1)
    low = lane < HEAD_DIM
    zero = jnp.zeros((tq, LANES), BF16)
    stacked = [jnp.where(low, qb, zero) for qb in q_blocks] + [jnp.where(low, zero, qb) for qb in q_blocks]
    qs = jnp.concatenate(stacked, axis=0)
    scores = lax.dot_general(qs, kc, (((1,), (1,)), ((), ())), preferred_element_type=F32)
    probs = []
    inv_den = []
    for hh in range(2 * Q_PER_KV):
        head = (2 * pair + hh // Q_PER_KV) * Q_PER_KV + hh % Q_PER_KV
        s = scores[hh * tq:(hh + 1) * tq] * (HEAD_DIM ** -0.5) - _alibi_slope(head) * delta
        s = jnp.where(valid, s, NEG_INF)
        sink = sink_ref[head]
        m = jnp.maximum(jnp.max(s, axis=-1, keepdims=True), sink)
        p = jnp.exp(s - m)
        den = jnp.sum(p, axis=-1, keepdims=True) + jnp.exp(sink - m)
        probs.append(p.astype(BF16))
        inv_den.append(1.0 / den)
    pv = jnp.dot(jnp.concatenate(probs, axis=0), vc, preferred_element_type=F32)
    outs = []
    for r in range(Q_PER_KV):
        o_low = pv[r * tq:(r + 1) * tq] * inv_den[r]
        o_high = pv[(Q_PER_KV + r) * tq:(Q_PER_KV + r + 1) * tq] * inv_den[Q_PER_KV + r]
        outs.append(jnp.where(low, o_low, o_high))
    return outs


def _prompt_attn_kernel(sink_ref, q_ref, kvp_ref, kvc_ref, o_ref):
    n = pl.program_id(1)
    i = lax.broadcasted_iota(jnp.int32, (WINDOW, 2 * WINDOW), 0)
    j = lax.broadcasted_iota(jnp.int32, (WINDOW, 2 * WINDOW), 1)
    d = i + WINDOW - j
    valid = (d >= 0) & (d <= WINDOW) & ((j >= WINDOW) | (n > 0))
    delta = d.astype(F32)
    for pair in range(N_KV_HEADS // 2):
        ks = slice(pair * LANES, (pair + 1) * LANES)
        vs = slice(KV_WIDTH + pair * LANES, KV_WIDTH + (pair + 1) * LANES)
        kc = jnp.concatenate([kvp_ref[:, ks], kvc_ref[:, ks]], axis=0).astype(BF16)
        vc = jnp.concatenate([kvp_ref[:, vs], kvc_ref[:, vs]], axis=0).astype(BF16)
        qb = [q_ref[:, (pair * Q_PER_KV + r) * LANES:(pair * Q_PER_KV + r + 1) * LANES] for r in range(Q_PER_KV)]
        outs = _pair_attention(qb, kc, vc, delta, valid, sink_ref, pair)
        for r in range(Q_PER_KV):
            o_ref[:, (pair * Q_PER_KV + r) * LANES:(pair * Q_PER_KV + r + 1) * LANES] = outs[r]


def _prompt_attention(sinks, q, kv, batch, seq):
    nb = seq // WINDOW
    return pl.pallas_call(
        _prompt_attn_kernel,
        grid=(batch, nb),
        in_specs=[
            pl.BlockSpec(memory_space=pltpu.SMEM),
            pl.BlockSpec((WINDOW, ATTN_WIDTH), lambda b, n: (b * nb + n, 0)),
            pl.BlockSpec((WINDOW, 2 * KV_WIDTH), lambda b, n: (b * nb + jnp.maximum(n - 1, 0), 0)),
            pl.BlockSpec((WINDOW, 2 * KV_WIDTH), lambda b, n: (b * nb + n, 0)),
        ],
        out_specs=pl.BlockSpec((WINDOW, ATTN_WIDTH), lambda b, n: (b * nb + n, 0)),
        out_shape=jax.ShapeDtypeStruct((batch * seq, ATTN_WIDTH), F32),
        compiler_params=_cparams(("parallel", "parallel")),
        name="prompt_attention",
    )(sinks, q, kv, kv)


SAMPLE_ATTN_BATCH = 8


def _sample_attn_kernel(sink_ref, q_ref, ck_ref, cv_ref, kvn_ref, o_ref, *, t_new):
    rows = Q_PER_KV * t_new
    tk = WINDOW + SUBLANES
    i = lax.broadcasted_iota(jnp.int32, (rows, tk), 0) % t_new
    j = lax.broadcasted_iota(jnp.int32, (rows, tk), 1)
    d = i + WINDOW - j
    valid = (d >= 0) & (d <= WINDOW)
    delta = d.astype(F32)
    for e in range(SAMPLE_ATTN_BATCH):
        for pair in range(N_KV_HEADS // 2):
            ks = slice(pair * LANES, (pair + 1) * LANES)
            vs = slice(KV_WIDTH + pair * LANES, KV_WIDTH + (pair + 1) * LANES)
            kc = jnp.concatenate([ck_ref[e, :, ks], kvn_ref[e, :, ks]], axis=0).astype(BF16)
            vc = jnp.concatenate([cv_ref[e, :, ks], kvn_ref[e, :, vs]], axis=0).astype(BF16)
            qp = q_ref[e, pair * rows:(pair + 1) * rows, :]
            outs = _sample_pair(qp, kc, vc, delta, valid, sink_ref, pair, t_new)
            o_ref[e, pair * rows:(pair + 1) * rows, :] = outs


def _sample_pair(qp, kc, vc, delta, valid, sink_ref, pair, t_new):
    rows = qp.shape[0]
    lane = lax.broadcasted_iota(jnp.int32, (rows, LANES), 1)
    low = lane < HEAD_DIM
    zero = jnp.zeros((rows, LANES), BF16)
    qs = jnp.concatenate([jnp.where(low, qp, zero), jnp.where(low, zero, qp)], axis=0)
    scores = lax.dot_general(qs, kc, (((1,), (1,)), ((), ())), preferred_element_type=F32)
    rid = lax.broadcasted_iota(jnp.int32, (rows, 1), 0) // t_new
    halves = []
    for half in range(2):
        slope = jnp.zeros((rows, 1), F32)
        sink = jnp.zeros((rows, 1), F32)
        for r in range(Q_PER_KV):
            head = (2 * pair + half) * Q_PER_KV + r
            slope = jnp.where(rid == r, _alibi_slope(head), slope)
            sink = jnp.where(rid == r, sink_ref[head], sink)
        s = scores[half * rows:(half + 1) * rows] * (HEAD_DIM ** -0.5) - slope * delta
        s = jnp.where(valid, s, NEG_INF)
        m = jnp.maximum(jnp.max(s, axis=-1, keepdims=True), sink)
        p = jnp.exp(s - m)
        den = jnp.sum(p, axis=-1, keepdims=True) + jnp.exp(sink - m)
        pv = jnp.dot(p.astype(BF16), vc, preferred_element_type=F32)
        halves.append(pv * (1.0 / den))
    return jnp.where(low, halves[0], halves[1])


def _sample_attention(sinks, q3, ck, cv, kvn, t_new):
    n = q3.shape[0]
    bn = SAMPLE_ATTN_BATCH
    qrows = q3.shape[1]
    return pl.pallas_call(
        functools.partial(_sample_attn_kernel, t_new=t_new),
        grid=(n // bn,),
        in_specs=[
            pl.BlockSpec(memory_space=pltpu.SMEM),
            pl.BlockSpec((bn, qrows, LANES), lambda b: (b, 0, 0)),
            pl.BlockSpec((bn, WINDOW, KV_WIDTH), lambda b: (b, 0, 0)),
            pl.BlockSpec((bn, WINDOW, KV_WIDTH), lambda b: (b, 0, 0)),
            pl.BlockSpec((bn, SUBLANES, 2 * KV_WIDTH), lambda b: (b, 0, 0)),
        ],
        out_specs=pl.BlockSpec((bn, qrows, LANES), lambda b: (b, 0, 0)),
        out_shape=jax.ShapeDtypeStruct((n, qrows, LANES), F32),
        compiler_params=_cparams(("parallel",)),
        name="sample_attention",
    )(sinks, q3, ck, cv, kvn)


def _gelu_tanh(x):
    c = math.sqrt(2.0 / math.pi)
    return 0.5 * x * (1.0 + jnp.tanh(c * (x + 0.044715 * (x * x * x))))


def _ssm_kernel(u_ref, h0_ref, ar_ref, ai_ref, wb_ref, wc_ref, d_ref, wglu_ref, bglu_ref, gs_ref,
                mix_ref, hlast_ref, s_ref, *, nb, steps):
    @pl.when(pl.program_id(0) == 0)
    def _():
        hlast_ref[...] = h0_ref[...]

    u = u_ref[...]
    ub = u.astype(BF16)
    blk_cols = 2 * BLOCK_STATES
    for j in range(N_SSM_BLOCKS):
        s_ref[:, j * blk_cols:(j + 1) * blk_cols] = jnp.dot(
            ub[:, j * MXU_DIM:(j + 1) * MXU_DIM], wb_ref[j], preferred_element_type=F32)

    for j in range(N_SSM_BLOCKS):
        for part in range(BLOCK_STATES // SCAN_LANES):
            rc = j * blk_cols + part * SCAN_LANES
            ic = rc + BLOCK_STATES
            sc = j * BLOCK_STATES + part * SCAN_LANES
            a_re = ar_ref[:, sc:sc + SCAN_LANES]
            a_im = ai_ref[:, sc:sc + SCAN_LANES]

            def batch_body(bg, carry, rc=rc, ic=ic, a_re=a_re, a_im=a_im):
                b0 = pl.multiple_of(bg * SUBLANES, SUBLANES)
                h_re = hlast_ref[pl.ds(b0, SUBLANES), rc:rc + SCAN_LANES]
                h_im = hlast_ref[pl.ds(b0, SUBLANES), ic:ic + SCAN_LANES]

                def step_body(t, h):
                    h_re, h_im = h
                    row = pl.multiple_of(t * nb + b0, SUBLANES)
                    n_re = a_re * h_re - a_im * h_im + s_ref[pl.ds(row, SUBLANES), rc:rc + SCAN_LANES]
                    n_im = a_re * h_im + a_im * h_re + s_ref[pl.ds(row, SUBLANES), ic:ic + SCAN_LANES]
                    s_ref[pl.ds(row, SUBLANES), rc:rc + SCAN_LANES] = n_re
                    s_ref[pl.ds(row, SUBLANES), ic:ic + SCAN_LANES] = n_im
                    return n_re, n_im

                h_re, h_im = lax.fori_loop(0, steps, step_body, (h_re, h_im), unroll=min(steps, 4))
                hlast_ref[pl.ds(b0, SUBLANES), rc:rc + SCAN_LANES] = h_re
                hlast_ref[pl.ds(b0, SUBLANES), ic:ic + SCAN_LANES] = h_im
                return carry

            lax.fori_loop(0, nb // SUBLANES, batch_body, 0)

    ys = []
    for j in range(N_SSM_BLOCKS):
        hb = s_ref[:, j * blk_cols:(j + 1) * blk_cols].astype(BF16)
        ys.append(jnp.dot(hb, wc_ref[j], preferred_element_type=F32))
    y = jnp.concatenate(ys, axis=1) + d_ref[...] * u
    g = _gelu_tanh(y)
    gate = jnp.dot(g.astype(BF16), wglu_ref[...], preferred_element_type=F32) + bglu_ref[...]
    so = g * jax.nn.sigmoid(gate)
    r = lax.rsqrt(jnp.mean(so * so, axis=-1, keepdims=True) + EPS)
    mix_ref[...] = (so * r * gs_ref[...]).astype(BF16)


SSM_TILE_ROWS = 256


def _ssm_mixer(u_tm, h0, ar8, ai8, wb, wc, d, wglu_b, bglu, gs, nb):
    rows = u_tm.shape[0]
    tr = SSM_TILE_ROWS
    steps = tr // nb
    assert steps * nb == tr and rows % tr == 0
    cols = 2 * N_STATE
    return pl.pallas_call(
        functools.partial(_ssm_kernel, nb=nb, steps=steps),
        grid=(rows // tr,),
        in_specs=[
            pl.BlockSpec((tr, SSM_WIDTH), lambda i: (i, 0)),
            _const_spec((nb, cols)),
            _const_spec((SUBLANES, N_STATE)),
            _const_spec((SUBLANES, N_STATE)),
            _const_spec((N_SSM_BLOCKS, MXU_DIM, 2 * BLOCK_STATES)),
            _const_spec((N_SSM_BLOCKS, 2 * BLOCK_STATES, MXU_DIM)),
            _const_spec((1, SSM_WIDTH)),
            _const_spec((SSM_WIDTH, SSM_WIDTH)),
            _const_spec((1, SSM_WIDTH)),
            _const_spec((1, SSM_WIDTH)),
        ],
        out_specs=[
            pl.BlockSpec((tr, SSM_WIDTH), lambda i: (i, 0)),
            pl.BlockSpec((nb, cols), lambda i: (0, 0)),
        ],
        out_shape=(
            jax.ShapeDtypeStruct((rows, SSM_WIDTH), BF16),
            jax.ShapeDtypeStruct((nb, cols), F32),
        ),
        scratch_shapes=[pltpu.VMEM((tr, cols), F32)],
        compiler_params=_cparams(("arbitrary",)),
        name="ssm_mixer",
    )(u_tm, h0, ar8, ai8, wb, wc, d, wglu_b, bglu, gs)


def _out_mlp_kernel(x_ref, a_ref, ga_ref, ms_ref, woa_ref, wos_ref, gm_ref, wup_ref, wdn_ref,
                    y_ref, hn_ref):
    @pl.when(pl.program_id(1) == 0)
    def _():
        a = a_ref[...]
        ra = lax.rsqrt(jnp.mean(a * a, axis=-1, keepdims=True) + EPS)
        ma = (a * ra * ga_ref[...]).astype(BF16)
        h = (x_ref[...]
             + jnp.dot(ma, woa_ref[...], preferred_element_type=F32)
             + jnp.dot(ms_ref[...], wos_ref[...], preferred_element_type=F32))
        y_ref[...] = h
        rh = lax.rsqrt(jnp.mean(h * h, axis=-1, keepdims=True) + EPS)
        hn_ref[...] = (h * rh * gm_ref[...]).astype(BF16)

    t = jnp.dot(hn_ref[...], wup_ref[...], preferred_element_type=F32)
    t = jnp.maximum(t, 0.0)
    t = (t * t).astype(BF16)
    y_ref[...] += jnp.dot(t, wdn_ref[...], preferred_element_type=F32)


def _out_mlp(x2d, attn, ga, mix_s, woa, wos, gm, wup, wdn, tm, tf):
    rows = x2d.shape[0]
    return pl.pallas_call(
        _out_mlp_kernel,
        grid=(rows // tm, D_FF // tf),
        in_specs=[
            pl.BlockSpec((tm, D_MODEL), lambda i, j: (i, 0)),
            pl.BlockSpec((tm, ATTN_WIDTH), lambda i, j: (i, 0)),
            _const_spec((1, ATTN_WIDTH)),
            pl.BlockSpec((tm, SSM_WIDTH), lambda i, j: (i, 0)),
            _const_spec((ATTN_WIDTH, D_MODEL)),
            _const_spec((SSM_WIDTH, D_MODEL)),
            _const_spec((1, D_MODEL)),
            pl.BlockSpec((D_MODEL, tf), lambda i, j: (0, j)),
            pl.BlockSpec((tf, D_MODEL), lambda i, j: (j, 0)),
        ],
        out_specs=pl.BlockSpec((tm, D_MODEL), lambda i, j: (i, 0)),
        out_shape=jax.ShapeDtypeStruct((rows, D_MODEL), F32),
        scratch_shapes=[pltpu.VMEM((tm, D_MODEL), BF16)],
        compiler_params=_cparams(("parallel", "arbitrary")),
        name="out_mlp",
    )(x2d, attn, ga, mix_s, woa, wos, gm, wup, wdn)


def _q_column_order():
    cols = []
    for pair in range(N_KV_HEADS // 2):
        for r in range(Q_PER_KV):
            for half in range(2):
                head = (2 * pair + half) * Q_PER_KV + r
                cols.extend(range(head * HEAD_DIM, (head + 1) * HEAD_DIM))
    return jnp.asarray(cols, dtype=jnp.int32)


def _block_diag_weights(bb_re, bb_im, c_re, c_im):
    gpb = GROUPS_PER_BLOCK
    eye = jnp.eye(gpb, dtype=F32)

    def b_blocks(bb):
        b = bb.reshape(SSM_GROUP, N_SSM_BLOCKS, gpb, STATE_DIM)
        w = jnp.einsum('hjgp,gk->jghkp', b, eye)
        return w.reshape(N_SSM_BLOCKS, gpb * SSM_GROUP, BLOCK_STATES)

    def c_blocks(c):
        cc = c.reshape(N_SSM_BLOCKS, gpb, SSM_GROUP, STATE_DIM)
        w = jnp.einsum('jghp,gk->jgpkh', cc, eye)
        return w.reshape(N_SSM_BLOCKS, BLOCK_STATES, gpb * SSM_GROUP)

    wb = jnp.concatenate([b_blocks(bb_re), b_blocks(bb_im)], axis=2).astype(BF16)
    wc = jnp.concatenate([c_blocks(c_re), -c_blocks(c_im)], axis=1).astype(BF16)
    return wb, wc


def _state_to_cols(h_re, h_im):
    n = h_re.shape[0]
    re = h_re.reshape(n, N_SSM_BLOCKS, 1, BLOCK_STATES)
    im = h_im.reshape(n, N_SSM_BLOCKS, 1, BLOCK_STATES)
    return jnp.concatenate([re, im], axis=2).reshape(n, 2 * N_STATE)


def _cols_to_state(h):
    n = h.shape[0]
    h4 = h.reshape(n, N_SSM_BLOCKS, 2, BLOCK_STATES)
    re = h4[:, :, 0].reshape(n, N_SSM_GROUPS, STATE_DIM)
    im = h4[:, :, 1].reshape(n, N_SSM_GROUPS, STATE_DIM)
    return re, im


def _layer(x, cache_k, cache_v, h0_re, h0_im, p):
    n, t = x.shape[:2]
    rows = n * t
    tm = 512
    x2d = x.reshape(rows, D_MODEL)
    q, kv, u = _in_proj(x2d, p['gn'], p['w_in'], p['gq'], p['gk'], tm)
    k = kv[:, :KV_WIDTH].reshape(n, t, N_KV_HEADS, HEAD_DIM)
    v = kv[:, KV_WIDTH:].reshape(n, t, N_KV_HEADS, HEAD_DIM)

    if cache_k is None:
        attn = _prompt_attention(p['sinks'], q, kv, n, t)
        new_k = k[:, t - WINDOW:]
        new_v = v[:, t - WINDOW:]
    else:
        nblk = ATTN_WIDTH // LANES
        q3 = q.reshape(n, t, nblk, LANES).transpose(0, 2, 1, 3).reshape(n, nblk * t, LANES)
        kvn = jnp.pad(kv.reshape(n, t, 2 * KV_WIDTH), ((0, 0), (0, SUBLANES - t), (0, 0)))
        ck = cache_k.reshape(n, WINDOW, KV_WIDTH)
        cv = cache_v.reshape(n, WINDOW, KV_WIDTH)
        a3 = _sample_attention(p['sinks'], q3, ck, cv, kvn, t)
        attn = a3.reshape(n, nblk, t, LANES).transpose(0, 2, 1, 3).reshape(rows, ATTN_WIDTH)
        new_k = jnp.concatenate([cache_k[:, t:], k], axis=1)
        new_v = jnp.concatenate([cache_v[:, t:], v], axis=1)

    u_tm = u.reshape(n, t, SSM_WIDTH).transpose(1, 0, 2).reshape(rows, SSM_WIDTH)
    mix_tm, h_last = _ssm_mixer(u_tm, _state_to_cols(h0_re, h0_im), p['ar8'], p['ai8'], p['wb'], p['wc'],
                                p['d'], p['w_glu'], p['b_glu'], p['gs'], n)
    mix_s = mix_tm.reshape(t, n, SSM_WIDTH).transpose(1, 0, 2).reshape(rows, SSM_WIDTH)
    h_re, h_im = _cols_to_state(h_last)

    y = _out_mlp(x2d, attn, p['ga'], mix_s, p['woa'], p['wos'], p['gm'], p['w_up'], p['w_down'], tm, 512)
    return y.reshape(n, t, D_MODEL), new_k, new_v, h_re, h_im


def _prepare_params(l, attn_norm_g, w_in, q_norm_g, k_norm_g, attn_sinks,
                    ssm_A_re, ssm_A_im, ssm_log_dt, ssm_B_re, ssm_B_im, ssm_C_re, ssm_C_im, ssm_D,
                    w_glu, b_glu, attn_out_g, ssm_out_g, w_out, mlp_norm_g, w_mlp_up, w_mlp_down):
    qcols = _q_column_order()
    w_in_l = w_in[l]
    w_in_b = jnp.concatenate([w_in_l[:, :ATTN_WIDTH][:, qcols], w_in_l[:, ATTN_WIDTH:]], axis=1).astype(BF16)
    heads_per_blk = MXU_DIM // HEAD_DIM

    a_re = ssm_A_re[l].reshape(1, N_STATE)
    a_im = ssm_A_im[l].reshape(1, N_STATE)
    ldt = jnp.broadcast_to(ssm_log_dt[l][:, None], (N_SSM_GROUPS, STATE_DIM)).reshape(1, N_STATE)
    b_re = ssm_B_re[l].transpose(2, 0, 1).reshape(SSM_GROUP, N_STATE)
    b_im = ssm_B_im[l].transpose(2, 0, 1).reshape(SSM_GROUP, N_STATE)
    ab_re, ab_im, bb_re, bb_im = _ssm_discretize(a_re, a_im, ldt, b_re, b_im)
    wb, wc = _block_diag_weights(bb_re, bb_im, ssm_C_re[l], ssm_C_im[l])

    w_out_l = w_out[l]
    return dict(
        gn=attn_norm_g[l].reshape(1, D_MODEL),
        w_in=w_in_b,
        gq=jnp.tile(q_norm_g[l], heads_per_blk).reshape(1, MXU_DIM),
        gk=jnp.tile(k_norm_g[l], heads_per_blk).reshape(1, MXU_DIM),
        sinks=attn_sinks[l].astype(F32),
        ar8=jnp.broadcast_to(ab_re, (SUBLANES, N_STATE)),
        ai8=jnp.broadcast_to(ab_im, (SUBLANES, N_STATE)),
        wb=wb, wc=wc,
        d=ssm_D[l].reshape(1, SSM_WIDTH),
        w_glu=w_glu[l].astype(BF16),
        b_glu=b_glu[l].reshape(1, SSM_WIDTH),
        gs=ssm_out_g[l].reshape(1, SSM_WIDTH),
        ga=attn_out_g[l][qcols].reshape(1, ATTN_WIDTH),
        woa=w_out_l[:ATTN_WIDTH][qcols].astype(BF16),
        wos=w_out_l[ATTN_WIDTH:].astype(BF16),
        gm=mlp_norm_g[l].reshape(1, D_MODEL),
        w_up=w_mlp_up[l].astype(BF16),
        w_down=w_mlp_down[l].astype(BF16),
    )


def kernel(x_prompt, x_sample, cache_k, cache_v, state_ssm_re, state_ssm_im, attn_norm_g, w_in, q_norm_g, k_norm_g, attn_sinks, ssm_A_re, ssm_A_im, ssm_log_dt, ssm_B_re, ssm_B_im, ssm_C_re, ssm_C_im, ssm_D, w_glu, b_glu, attn_out_g, ssm_out_g, w_out, mlp_norm_g, w_mlp_up, w_mlp_down):
    depth = w_in.shape[0]
    xp, xs = x_prompt, x_sample
    zeros_state = jnp.zeros((x_prompt.shape[0], N_SSM_GROUPS, STATE_DIM), F32)
    outs = [[] for _ in range(8)]
    for l in range(depth):
        p = _prepare_params(l, attn_norm_g, w_in, q_norm_g, k_norm_g, attn_sinks,
                            ssm_A_re, ssm_A_im, ssm_log_dt, ssm_B_re, ssm_B_im, ssm_C_re, ssm_C_im, ssm_D,
                            w_glu, b_glu, attn_out_g, ssm_out_g, w_out, mlp_norm_g, w_mlp_up, w_mlp_down)
        xp, kp, vp, hrp, hip = _layer(xp, None, None, zeros_state, zeros_state, p)
        xs, ks, vs, hrs, his = _layer(xs, cache_k[l], cache_v[l], state_ssm_re[l], state_ssm_im[l], p)
        for lst, val in zip(outs, (kp, vp, hrp, hip, ks, vs, hrs, his)):
            lst.append(val)
    return (xp, xs) + tuple(jnp.stack(o) for o in outs)
```

```python
import functools
import math

import jax
import jax.numpy as jnp
from jax import lax
from jax.experimental import pallas as pl
from jax.experimental.pallas import tpu as pltpu

D_MODEL = 2048
ATTN_WIDTH = 1024
SSM_WIDTH = 1024
HEAD_DIM = 64
N_HEADS = 16
N_KV_HEADS = 4
Q_PER_KV = 4
KV_WIDTH = 256
WINDOW = 128
SSM_GROUP = 16
N_SSM_GROUPS = 64
STATE_DIM = 64
N_STATE = N_SSM_GROUPS * STATE_DIM
D_FF = 8192
PROJ_WIDTH = ATTN_WIDTH + 2 * KV_WIDTH + SSM_WIDTH
EPS = 1e-6
NEG_INF = -1e30

LANES = 128
SUBLANES = 8
MXU_DIM = 256
VMEM_LIMIT = 56 * 1024 * 1024

GROUPS_PER_BLOCK = MXU_DIM // SSM_GROUP
N_SSM_BLOCKS = N_SSM_GROUPS // GROUPS_PER_BLOCK
BLOCK_STATES = GROUPS_PER_BLOCK * STATE_DIM
SCAN_LANES = 512
SSM_TILE_ROWS = 256
MLP_FF_TILE = 1024

F32 = jnp.float32
BF16 = jnp.bfloat16


def _cparams(sem):
    return pltpu.CompilerParams(dimension_semantics=sem, vmem_limit_bytes=VMEM_LIMIT)


def _const_spec(shape):
    nd = len(shape)
    return pl.BlockSpec(shape, lambda *_: (0,) * nd, pipeline_mode=pl.Buffered(1))


def _discretize_kernel(are_ref, aim_ref, ldt_ref, bre_ref, bim_ref,
                       abr_ref, abi_ref, bbr_ref, bbi_ref):
    a_re = are_ref[...]
    a_im = aim_ref[...]
    dt = jnp.exp(ldt_ref[...])
    mag = jnp.exp(a_re * dt)
    ab_re = mag * jnp.cos(a_im * dt)
    ab_im = mag * jnp.sin(a_im * dt)
    abr_ref[...] = ab_re
    abi_ref[...] = ab_im
    x = ab_re - 1.0
    y = ab_im
    den = a_re * a_re + a_im * a_im
    c_re = (x * a_re + y * a_im) / den
    c_im = (y * a_re - x * a_im) / den
    b_re = bre_ref[...]
    b_im = bim_ref[...]
    bbr_ref[...] = c_re * b_re - c_im * b_im
    bbi_ref[...] = c_re * b_im + c_im * b_re


def _ssm_discretize(a_re, a_im, log_dt, b_re, b_im):
    row = jax.ShapeDtypeStruct((1, N_STATE), F32)
    mat = jax.ShapeDtypeStruct((SSM_GROUP, N_STATE), F32)
    return pl.pallas_call(
        _discretize_kernel,
        out_shape=(row, row, mat, mat),
        name="ssm_discretize",
    )(a_re, a_im, log_dt, b_re, b_im)


def _head_rmsnorm(zc, gain, ones_blk):
    sq = zc * zc
    hi = sq.astype(BF16)
    lo = (sq - hi.astype(F32)).astype(BF16)
    ss = (jnp.dot(hi, ones_blk, preferred_element_type=F32)
          + jnp.dot(lo, ones_blk, preferred_element_type=F32))
    return zc * lax.rsqrt(ss * (1.0 / HEAD_DIM) + EPS) * gain


def _in_proj_kernel(x_ref, gn_ref, wq_ref, wr_ref, gq_ref, gk_ref, q_ref, kv_ref, u_ref):
    x = x_ref[...]
    r = lax.rsqrt(jnp.mean(x * x, axis=-1, keepdims=True) + EPS)
    xn = (x * r * gn_ref[...]).astype(BF16)
    zq = jnp.dot(xn, wq_ref[...], preferred_element_type=F32)
    zr = jnp.dot(xn, wr_ref[...], preferred_element_type=F32)
    ri = lax.broadcasted_iota(jnp.int32, (MXU_DIM, MXU_DIM), 0) // HEAD_DIM
    ci = lax.broadcasted_iota(jnp.int32, (MXU_DIM, MXU_DIM), 1) // HEAD_DIM
    ones_blk = (ri == ci).astype(BF16)
    gq = gq_ref[...]
    for c in range(ATTN_WIDTH // MXU_DIM):
        sl = slice(c * MXU_DIM, (c + 1) * MXU_DIM)
        q_ref[:, sl] = _head_rmsnorm(zq[:, sl], gq, ones_blk).astype(BF16)
    kv_ref[:, :KV_WIDTH] = _head_rmsnorm(zr[:, :KV_WIDTH], gk_ref[...], ones_blk)
    kv_ref[:, KV_WIDTH:] = zr[:, KV_WIDTH:2 * KV_WIDTH]
    u_ref[...] = zr[:, 2 * KV_WIDTH:]


def _in_proj(x2d, gn, wq, wr, gq, gk, tm):
    rows = x2d.shape[0]
    return pl.pallas_call(
        _in_proj_kernel,
        grid=(rows // tm,),
        in_specs=[
            pl.BlockSpec((tm, D_MODEL), lambda i: (i, 0)),
            _const_spec((1, D_MODEL)),
            _const_spec((D_MODEL, ATTN_WIDTH)),
            _const_spec((D_MODEL, PROJ_WIDTH - ATTN_WIDTH)),
            _const_spec((1, MXU_DIM)),
            _const_spec((1, MXU_DIM)),
        ],
        out_specs=[
            pl.BlockSpec((tm, ATTN_WIDTH), lambda i: (i, 0)),
            pl.BlockSpec((tm, 2 * KV_WIDTH), lambda i: (i, 0)),
            pl.BlockSpec((tm, SSM_WIDTH), lambda i: (i, 0)),
        ],
        out_shape=(
            jax.ShapeDtypeStruct((rows, ATTN_WIDTH), BF16),
            jax.ShapeDtypeStruct((rows, 2 * KV_WIDTH), F32),
            jax.ShapeDtypeStruct((rows, SSM_WIDTH), F32),
        ),
        compiler_params=_cparams(("parallel",)),
        name="in_proj",
    )(x2d, gn, wq, wr, gq, gk)


def _alibi_slope(head):
    return 2.0 ** (-8.0 * (head + 1) / N_HEADS)


def _pair_attention(q_blocks, kc, vc, delta, valid, sink_ref, pair):
    tq = q_blocks[0].shape[0]
    lane = lax.broadcasted_iota(jnp.int32, (tq, LANES), 1)
    low = lane < HEAD_DIM
    zero = jnp.zeros((tq, LANES), BF16)
    stacked = [jnp.where(low, qb, zero) for qb in q_blocks] + [jnp.where(low, zero, qb) for qb in q_blocks]
    qs = jnp.concatenate(stacked, axis=0)
    scores = lax.dot_general(qs, kc, (((1,), (1,)), ((), ())), preferred_element_type=F32)
    probs = []
    inv_den = []
    for hh in range(2 * Q_PER_KV):
        head = (2 * pair + hh // Q_PER_KV) * Q_PER_KV + hh % Q_PER_KV
        s = scores[hh * tq:(hh + 1) * tq] * (HEAD_DIM ** -0.5) - _alibi_slope(head) * delta
        s = jnp.where(valid, s, NEG_INF)
        sink = sink_ref[head]
        m = jnp.maximum(jnp.max(s, axis=-1, keepdims=True), sink)
        p = jnp.exp(s - m)
        den = jnp.sum(p, axis=-1, keepdims=True) + jnp.exp(sink - m)
        probs.append(p.astype(BF16))
        inv_den.append(1.0 / den)
    pv = jnp.dot(jnp.concatenate(probs, axis=0), vc, preferred_element_type=F32)
    outs = []
    for r in range(Q_PER_KV):
        o_low = pv[r * tq:(r + 1) * tq] * inv_den[r]
        o_high = pv[(Q_PER_KV + r) * tq:(Q_PER_KV + r + 1) * tq] * inv_den[Q_PER_KV + r]
        outs.append(jnp.where(low, o_low, o_high))
    return outs


def _prompt_attn_kernel(sink_ref, q_ref, kvp_ref, kvc_ref, o_ref):
    n = pl.program_id(1)
    i = lax.broadcasted_iota(jnp.int32, (WINDOW, 2 * WINDOW), 0)
    j = lax.broadcasted_iota(jnp.int32, (WINDOW, 2 * WINDOW), 1)
    d = i + WINDOW - j
    valid = (d >= 0) & (d <= WINDOW) & ((j >= WINDOW) | (n > 0))
    delta = d.astype(F32)
    for pair in range(N_KV_HEADS // 2):
        ks = slice(pair * LANES, (pair + 1) * LANES)
        vs = slice(KV_WIDTH + pair * LANES, KV_WIDTH + (pair + 1) * LANES)
        kc = jnp.concatenate([kvp_ref[:, ks], kvc_ref[:, ks]], axis=0).astype(BF16)
        vc = jnp.concatenate([kvp_ref[:, vs], kvc_ref[:, vs]], axis=0).astype(BF16)
        qb = [q_ref[:, (pair * Q_PER_KV + r) * LANES:(pair * Q_PER_KV + r + 1) * LANES] for r in range(Q_PER_KV)]
        outs = _pair_attention(qb, kc, vc, delta, valid, sink_ref, pair)
        for r in range(Q_PER_KV):
            o_ref[:, (pair * Q_PER_KV + r) * LANES:(pair * Q_PER_KV + r + 1) * LANES] = outs[r].astype(BF16)


def _prompt_attention(sinks, q, kv, batch, seq):
    nb = seq // WINDOW
    return pl.pallas_call(
        _prompt_attn_kernel,
        grid=(batch, nb),
        in_specs=[
            pl.BlockSpec(memory_space=pltpu.SMEM),
            pl.BlockSpec((WINDOW, ATTN_WIDTH), lambda b, n: (b * nb + n, 0)),
            pl.BlockSpec((WINDOW, 2 * KV_WIDTH), lambda b, n: (b * nb + jnp.maximum(n - 1, 0), 0)),
            pl.BlockSpec((WINDOW, 2 * KV_WIDTH), lambda b, n: (b * nb + n, 0)),
        ],
        out_specs=pl.BlockSpec((WINDOW, ATTN_WIDTH), lambda b, n: (b * nb + n, 0)),
        out_shape=jax.ShapeDtypeStruct((batch * seq, ATTN_WIDTH), BF16),
        compiler_params=_cparams(("parallel", "parallel")),
        name="prompt_attention",
    )(sinks, q, kv, kv)


SAMPLE_ATTN_BATCH = 8


def _sample_attn_kernel(sink_ref, q_ref, ck_ref, cv_ref, kvn_ref, o_ref, *, t_new):
    rows = Q_PER_KV * t_new
    tk = WINDOW + SUBLANES
    i = lax.broadcasted_iota(jnp.int32, (rows, tk), 0) % t_new
    j = lax.broadcasted_iota(jnp.int32, (rows, tk), 1)
    d = i + WINDOW - j
    valid = (d >= 0) & (d <= WINDOW)
    delta = d.astype(F32)
    for e in range(SAMPLE_ATTN_BATCH):
        for pair in range(N_KV_HEADS // 2):
            ks = slice(pair * LANES, (pair + 1) * LANES)
            vs = slice(KV_WIDTH + pair * LANES, KV_WIDTH + (pair + 1) * LANES)
            kc = jnp.concatenate([ck_ref[e, :, ks], kvn_ref[e, :, ks]], axis=0).astype(BF16)
            vc = jnp.concatenate([cv_ref[e, :, ks], kvn_ref[e, :, vs]], axis=0).astype(BF16)
            qp = q_ref[e, pair * rows:(pair + 1) * rows, :]
            outs = _sample_pair(qp, kc, vc, delta, valid, sink_ref, pair, t_new)
            o_ref[e, pair * rows:(pair + 1) * rows, :] = outs.astype(BF16)


def _sample_pair(qp, kc, vc, delta, valid, sink_ref, pair, t_new):
    rows = qp.shape[0]
    lane = lax.broadcasted_iota(jnp.int32, (rows, LANES), 1)
    low = lane < HEAD_DIM
    zero = jnp.zeros((rows, LANES), BF16)
    qs = jnp.concatenate([jnp.where(low, qp, zero), jnp.where(low, zero, qp)], axis=0)
    scores = lax.dot_general(qs, kc, (((1,), (1,)), ((), ())), preferred_element_type=F32)
    rid = lax.broadcasted_iota(jnp.int32, (rows, 1), 0) // t_new
    halves = []
    for half in range(2):
        slope = jnp.zeros((rows, 1), F32)
        sink = jnp.zeros((rows, 1), F32)
        for r in range(Q_PER_KV):
            head = (2 * pair + half) * Q_PER_KV + r
            slope = jnp.where(rid == r, _alibi_slope(head), slope)
            sink = jnp.where(rid == r, sink_ref[head], sink)
        s = scores[half * rows:(half + 1) * rows] * (HEAD_DIM ** -0.5) - slope * delta
        s = jnp.where(valid, s, NEG_INF)
        m = jnp.maximum(jnp.max(s, axis=-1, keepdims=True), sink)
        p = jnp.exp(s - m)
        den = jnp.sum(p, axis=-1, keepdims=True) + jnp.exp(sink - m)
        pv = jnp.dot(p.astype(BF16), vc, preferred_element_type=F32)
        halves.append(pv * (1.0 / den))
    return jnp.where(low, halves[0], halves[1])


def _sample_attention(sinks, q3, ck, cv, kvn, t_new):
    n = q3.shape[0]
    bn = SAMPLE_ATTN_BATCH
    qrows = q3.shape[1]
    return pl.pallas_call(
        functools.partial(_sample_attn_kernel, t_new=t_new),
        grid=(n // bn,),
        in_specs=[
            pl.BlockSpec(memory_space=pltpu.SMEM),
            pl.BlockSpec((bn, qrows, LANES), lambda b: (b, 0, 0)),
            pl.BlockSpec((bn, WINDOW, KV_WIDTH), lambda b: (b, 0, 0)),
            pl.BlockSpec((bn, WINDOW, KV_WIDTH), lambda b: (b, 0, 0)),
            pl.BlockSpec((bn, SUBLANES, 2 * KV_WIDTH), lambda b: (b, 0, 0)),
        ],
        out_specs=pl.BlockSpec((bn, qrows, LANES), lambda b: (b, 0, 0)),
        out_shape=jax.ShapeDtypeStruct((n, qrows, LANES), BF16),
        compiler_params=_cparams(("parallel",)),
        name="sample_attention",
    )(sinks, q3, ck, cv, kvn)


def _gelu_tanh(x):
    c = math.sqrt(2.0 / math.pi)
    return 0.5 * x * (1.0 + jnp.tanh(c * (x + 0.044715 * (x * x * x))))


def _ssm_kernel(u_ref, h0_ref, ar_ref, ai_ref, wb_ref, wc_ref, d_ref, wglu_ref, bglu_ref, gs_ref,
                mix_ref, hlast_ref, s_ref, *, nb, steps):
    @pl.when(pl.program_id(0) == 0)
    def _():
        hlast_ref[...] = h0_ref[...]

    rows = nb * steps
    tm_row = lax.broadcasted_iota(jnp.int32, (rows, rows), 0)
    bm_col = lax.broadcasted_iota(jnp.int32, (rows, rows), 1)
    to_time_major = (bm_col == (tm_row % nb) * steps + tm_row // nb).astype(BF16)
    bm_row = lax.broadcasted_iota(jnp.int32, (rows, rows), 0)
    tm_col = lax.broadcasted_iota(jnp.int32, (rows, rows), 1)
    to_batch_major = (bm_row == (tm_col % nb) * steps + tm_col // nb).astype(BF16)

    u_bm = u_ref[...].reshape(rows, SSM_WIDTH)
    u_hi = u_bm.astype(BF16)
    u_lo = (u_bm - u_hi.astype(F32)).astype(BF16)
    u_hi_tm = jnp.dot(to_time_major, u_hi, preferred_element_type=F32)
    u = u_hi_tm + jnp.dot(to_time_major, u_lo, preferred_element_type=F32)
    ub = u_hi_tm.astype(BF16)
    blk_cols = 2 * BLOCK_STATES
    for j in range(N_SSM_BLOCKS):
        s_ref[:, j * blk_cols:(j + 1) * blk_cols] = jnp.dot(
            ub[:, j * MXU_DIM:(j + 1) * MXU_DIM], wb_ref[j], preferred_element_type=F32)

    for j in range(N_SSM_BLOCKS):
        for part in range(BLOCK_STATES // SCAN_LANES):
            rc = j * blk_cols + part * SCAN_LANES
            ic = rc + BLOCK_STATES
            sc = j * BLOCK_STATES + part * SCAN_LANES
            a_re = ar_ref[:, sc:sc + SCAN_LANES]
            a_im = ai_ref[:, sc:sc + SCAN_LANES]

            def batch_body(bg, carry, rc=rc, ic=ic, a_re=a_re, a_im=a_im):
                b0 = pl.multiple_of(bg * SUBLANES, SUBLANES)
                h_re = hlast_ref[pl.ds(b0, SUBLANES), rc:rc + SCAN_LANES]
                h_im = hlast_ref[pl.ds(b0, SUBLANES), ic:ic + SCAN_LANES]

                def step_body(t, h):
                    h_re, h_im = h
                    row = pl.multiple_of(t * nb + b0, SUBLANES)
                    n_re = a_re * h_re - a_im * h_im + s_ref[pl.ds(row, SUBLANES), rc:rc + SCAN_LANES]
                    n_im = a_re * h_im + a_im * h_re + s_ref[pl.ds(row, SUBLANES), ic:ic + SCAN_LANES]
                    s_ref[pl.ds(row, SUBLANES), rc:rc + SCAN_LANES] = n_re
                    s_ref[pl.ds(row, SUBLANES), ic:ic + SCAN_LANES] = n_im
                    return n_re, n_im

                h_re, h_im = lax.fori_loop(0, steps, step_body, (h_re, h_im), unroll=min(steps, 4))
                hlast_ref[pl.ds(b0, SUBLANES), rc:rc + SCAN_LANES] = h_re
                hlast_ref[pl.ds(b0, SUBLANES), ic:ic + SCAN_LANES] = h_im
                return carry

            lax.fori_loop(0, nb // SUBLANES, batch_body, 0)

    ys = []
    for j in range(N_SSM_BLOCKS):
        hb = s_ref[:, j * blk_cols:(j + 1) * blk_cols].astype(BF16)
        ys.append(jnp.dot(hb, wc_ref[j], preferred_element_type=F32))
    y = jnp.concatenate(ys, axis=1) + d_ref[...] * u
    g = _gelu_tanh(y)
    gate = jnp.dot(g.astype(BF16), wglu_ref[...], preferred_element_type=F32) + bglu_ref[...]
    so = g * jax.nn.sigmoid(gate)
    r = lax.rsqrt(jnp.mean(so * so, axis=-1, keepdims=True) + EPS)
    mix_tm = (so * r * gs_ref[...]).astype(BF16)
    mix_bm = jnp.dot(to_batch_major, mix_tm, preferred_element_type=F32).astype(BF16)
    mix_ref[...] = mix_bm.reshape(mix_ref.shape)


def _ssm_mixer(u3, h0, ar8, ai8, wb, wc, d, wglu_b, bglu, gs, nb, steps):
    nbv, seq, _ = u3.shape
    blk_rows = nb * steps // nbv
    assert blk_rows % SUBLANES == 0 and seq % blk_rows == 0
    cols = 2 * N_STATE
    tr = nb * steps
    return pl.pallas_call(
        functools.partial(_ssm_kernel, nb=nb, steps=steps),
        grid=(seq // blk_rows,),
        in_specs=[
            pl.BlockSpec((nbv, blk_rows, SSM_WIDTH), lambda i: (0, i, 0)),
            _const_spec((nb, cols)),
            _const_spec((SUBLANES, N_STATE)),
            _const_spec((SUBLANES, N_STATE)),
            _const_spec((N_SSM_BLOCKS, MXU_DIM, 2 * BLOCK_STATES)),
            _const_spec((N_SSM_BLOCKS, 2 * BLOCK_STATES, MXU_DIM)),
            _const_spec((1, SSM_WIDTH)),
            _const_spec((SSM_WIDTH, SSM_WIDTH)),
            _const_spec((1, SSM_WIDTH)),
            _const_spec((1, SSM_WIDTH)),
        ],
        out_specs=[
            pl.BlockSpec((nbv, blk_rows, SSM_WIDTH), lambda i: (0, i, 0)),
            pl.BlockSpec((nb, cols), lambda i: (0, 0)),
        ],
        out_shape=(
            jax.ShapeDtypeStruct((nbv, seq, SSM_WIDTH), BF16),
            jax.ShapeDtypeStruct((nb, cols), F32),
        ),
        scratch_shapes=[pltpu.VMEM((tr, cols), F32)],
        compiler_params=_cparams(("arbitrary",)),
        name="ssm_mixer",
    )(u3, h0, ar8, ai8, wb, wc, d, wglu_b, bglu, gs)


def _out_mlp_kernel(x_ref, a_ref, ga_ref, ms_ref, woa_ref, wos_ref, gm_ref, wup_ref, wdn_ref,
                    y_ref, hn_ref):
    @pl.when(pl.program_id(1) == 0)
    def _():
        a = a_ref[...].astype(F32)
        ra = lax.rsqrt(jnp.mean(a * a, axis=-1, keepdims=True) + EPS)
        ma = (a * ra * ga_ref[...]).astype(BF16)
        h = (x_ref[...]
             + jnp.dot(ma, woa_ref[...], preferred_element_type=F32)
             + jnp.dot(ms_ref[...], wos_ref[...], preferred_element_type=F32))
        y_ref[...] = h
        rh = lax.rsqrt(jnp.mean(h * h, axis=-1, keepdims=True) + EPS)
        hn_ref[...] = (h * rh * gm_ref[...]).astype(BF16)

    t = jnp.dot(hn_ref[...], wup_ref[...], preferred_element_type=F32)
    t = jnp.maximum(t, 0.0)
    t = (t * t).astype(BF16)
    y_ref[...] += jnp.dot(t, wdn_ref[...], preferred_element_type=F32)


def _out_mlp(x2d, attn, ga, mix_s, woa, wos, gm, wup, wdn, tm, tf):
    rows = x2d.shape[0]
    return pl.pallas_call(
        _out_mlp_kernel,
        grid=(rows // tm, D_FF // tf),
        in_specs=[
            pl.BlockSpec((tm, D_MODEL), lambda i, j: (i, 0)),
            pl.BlockSpec((tm, ATTN_WIDTH), lambda i, j: (i, 0)),
            _const_spec((1, ATTN_WIDTH)),
            pl.BlockSpec((tm, SSM_WIDTH), lambda i, j: (i, 0)),
            _const_spec((ATTN_WIDTH, D_MODEL)),
            _const_spec((SSM_WIDTH, D_MODEL)),
            _const_spec((1, D_MODEL)),
            pl.BlockSpec((D_MODEL, tf), lambda i, j: (0, j)),
            pl.BlockSpec((tf, D_MODEL), lambda i, j: (j, 0)),
        ],
        out_specs=pl.BlockSpec((tm, D_MODEL), lambda i, j: (i, 0)),
        out_shape=jax.ShapeDtypeStruct((rows, D_MODEL), F32),
        scratch_shapes=[pltpu.VMEM((tm, D_MODEL), BF16)],
        compiler_params=_cparams(("parallel", "arbitrary")),
        name="out_mlp",
    )(x2d, attn, ga, mix_s, woa, wos, gm, wup, wdn)


def _pair_heads(a, axis):
    shape = a.shape
    split = shape[:axis] + (N_KV_HEADS // 2, 2, Q_PER_KV, HEAD_DIM) + shape[axis + 1:]
    return jnp.swapaxes(a.reshape(split), axis + 1, axis + 2).reshape(shape)


def _block_diag_weights(bb_re, bb_im, c_re, c_im):
    gpb = GROUPS_PER_BLOCK
    eye = jnp.eye(gpb, dtype=F32)

    def b_blocks(bb):
        b = bb.reshape(SSM_GROUP, N_SSM_BLOCKS, gpb, STATE_DIM)
        w = jnp.einsum('hjgp,gk->jghkp', b, eye)
        return w.reshape(N_SSM_BLOCKS, gpb * SSM_GROUP, BLOCK_STATES)

    def c_blocks(c):
        cc = c.reshape(N_SSM_BLOCKS, gpb, SSM_GROUP, STATE_DIM)
        w = jnp.einsum('jghp,gk->jgpkh', cc, eye)
        return w.reshape(N_SSM_BLOCKS, BLOCK_STATES, gpb * SSM_GROUP)

    wb = jnp.concatenate([b_blocks(bb_re), b_blocks(bb_im)], axis=2).astype(BF16)
    wc = jnp.concatenate([c_blocks(c_re), -c_blocks(c_im)], axis=1).astype(BF16)
    return wb, wc


def _state_to_cols(h_re, h_im):
    n = h_re.shape[0]
    re = h_re.reshape(n, N_SSM_BLOCKS, 1, BLOCK_STATES)
    im = h_im.reshape(n, N_SSM_BLOCKS, 1, BLOCK_STATES)
    return jnp.concatenate([re, im], axis=2).reshape(n, 2 * N_STATE)


def _cols_to_state(h):
    n = h.shape[0]
    h4 = h.reshape(n, N_SSM_BLOCKS, 2, BLOCK_STATES)
    re = h4[:, :, 0].reshape(n, N_SSM_GROUPS, STATE_DIM)
    im = h4[:, :, 1].reshape(n, N_SSM_GROUPS, STATE_DIM)
    return re, im


def _layer(x, cache_k, cache_v, h0_re, h0_im, p):
    n, t = x.shape[:2]
    rows = n * t
    tm = 512
    x2d = x.reshape(rows, D_MODEL)
    q, kv, u = _in_proj(x2d, p['gn'], p['wq'], p['wr'], p['gq'], p['gk'], tm)
    kv3 = kv.reshape(n, t, 2 * KV_WIDTH)

    if cache_k is None:
        attn = _prompt_attention(p['sinks'], q, kv, n, t)
        tail = kv3[:, t - WINDOW:]
        new_k = tail[..., :KV_WIDTH].reshape(n, WINDOW, N_KV_HEADS, HEAD_DIM)
        new_v = tail[..., KV_WIDTH:].reshape(n, WINDOW, N_KV_HEADS, HEAD_DIM)
        u3 = u.reshape(n, t, SSM_WIDTH)
        ssm_steps = SSM_TILE_ROWS // n
    else:
        k = kv3[..., :KV_WIDTH].reshape(n, t, N_KV_HEADS, HEAD_DIM)
        v = kv3[..., KV_WIDTH:].reshape(n, t, N_KV_HEADS, HEAD_DIM)
        nblk = ATTN_WIDTH // LANES
        q3 = q.reshape(n, t, nblk, LANES).transpose(0, 2, 1, 3).reshape(n, nblk * t, LANES)
        kvn = jnp.pad(kv.reshape(n, t, 2 * KV_WIDTH), ((0, 0), (0, SUBLANES - t), (0, 0)))
        ck = cache_k.reshape(n, WINDOW, KV_WIDTH)
        cv = cache_v.reshape(n, WINDOW, KV_WIDTH)
        a3 = _sample_attention(p['sinks'], q3, ck, cv, kvn, t)
        attn = a3.reshape(n, nblk, t, LANES).transpose(0, 2, 1, 3).reshape(rows, ATTN_WIDTH)
        new_k = jnp.concatenate([cache_k[:, t:], k], axis=1)
        new_v = jnp.concatenate([cache_v[:, t:], v], axis=1)
        u3 = u.reshape(1, rows, SSM_WIDTH)
        ssm_steps = t

    mix3, h_last = _ssm_mixer(u3, _state_to_cols(h0_re, h0_im), p['ar8'], p['ai8'], p['wb'], p['wc'],
                              p['d'], p['w_glu'], p['b_glu'], p['gs'], n, ssm_steps)
    mix_s = mix3.reshape(rows, SSM_WIDTH)
    h_re, h_im = _cols_to_state(h_last)

    y = _out_mlp(x2d, attn, p['ga'], mix_s, p['woa'], p['wos'], p['gm'], p['w_up'], p['w_down'], tm, MLP_FF_TILE)
    return y.reshape(n, t, D_MODEL), new_k, new_v, h_re, h_im


def _prepare_params(l, attn_norm_g, w_in, q_norm_g, k_norm_g, attn_sinks,
                    ssm_A_re, ssm_A_im, ssm_log_dt, ssm_B_re, ssm_B_im, ssm_C_re, ssm_C_im, ssm_D,
                    w_glu, b_glu, attn_out_g, ssm_out_g, w_out, mlp_norm_g, w_mlp_up, w_mlp_down):
    w_in_l = w_in[l]
    heads_per_blk = MXU_DIM // HEAD_DIM

    a_re = ssm_A_re[l].reshape(1, N_STATE)
    a_im = ssm_A_im[l].reshape(1, N_STATE)
    ldt = jnp.broadcast_to(ssm_log_dt[l][:, None], (N_SSM_GROUPS, STATE_DIM)).reshape(1, N_STATE)
    b_re = ssm_B_re[l].transpose(2, 0, 1).reshape(SSM_GROUP, N_STATE)
    b_im = ssm_B_im[l].transpose(2, 0, 1).reshape(SSM_GROUP, N_STATE)
    ab_re, ab_im, bb_re, bb_im = _ssm_discretize(a_re, a_im, ldt, b_re, b_im)
    wb, wc = _block_diag_weights(bb_re, bb_im, ssm_C_re[l], ssm_C_im[l])

    w_out_l = w_out[l]
    return dict(
        gn=attn_norm_g[l].reshape(1, D_MODEL),
        wq=_pair_heads(w_in_l[:, :ATTN_WIDTH], 1).astype(BF16),
        wr=w_in_l[:, ATTN_WIDTH:].astype(BF16),
        gq=jnp.tile(q_norm_g[l], heads_per_blk).reshape(1, MXU_DIM),
        gk=jnp.tile(k_norm_g[l], heads_per_blk).reshape(1, MXU_DIM),
        sinks=attn_sinks[l].astype(F32),
        ar8=jnp.broadcast_to(ab_re, (SUBLANES, N_STATE)),
        ai8=jnp.broadcast_to(ab_im, (SUBLANES, N_STATE)),
        wb=wb, wc=wc,
        d=ssm_D[l].reshape(1, SSM_WIDTH),
        w_glu=w_glu[l].astype(BF16),
        b_glu=b_glu[l].reshape(1, SSM_WIDTH),
        gs=ssm_out_g[l].reshape(1, SSM_WIDTH),
        ga=_pair_heads(attn_out_g[l], 0).reshape(1, ATTN_WIDTH),
        woa=_pair_heads(w_out_l[:ATTN_WIDTH], 0).astype(BF16),
        wos=w_out_l[ATTN_WIDTH:].astype(BF16),
        gm=mlp_norm_g[l].reshape(1, D_MODEL),
        w_up=w_mlp_up[l].astype(BF16),
        w_down=w_mlp_down[l].astype(BF16),
    )


def kernel(x_prompt, x_sample, cache_k, cache_v, state_ssm_re, state_ssm_im, attn_norm_g, w_in, q_norm_g, k_norm_g, attn_sinks, ssm_A_re, ssm_A_im, ssm_log_dt, ssm_B_re, ssm_B_im, ssm_C_re, ssm_C_im, ssm_D, w_glu, b_glu, attn_out_g, ssm_out_g, w_out, mlp_norm_g, w_mlp_up, w_mlp_down):
    depth = w_in.shape[0]
    xp, xs = x_prompt, x_sample
    zeros_state = jnp.zeros((x_prompt.shape[0], N_SSM_GROUPS, STATE_DIM), F32)
    outs = [[] for _ in range(8)]
    for l in range(depth):
        p = _prepare_params(l, attn_norm_g, w_in, q_norm_g, k_norm_g, attn_sinks,
                            ssm_A_re, ssm_A_im, ssm_log_dt, ssm_B_re, ssm_B_im, ssm_C_re, ssm_C_im, ssm_D,
                            w_glu, b_glu, attn_out_g, ssm_out_g, w_out, mlp_norm_g, w_mlp_up, w_mlp_down)
        xp, kp, vp, hrp, hip = _layer(xp, None, None, zeros_state, zeros_state, p)
        xs, ks, vs, hrs, his = _layer(xs, cache_k[l], cache_v[l], state_ssm_re[l], state_ssm_im[l], p)
        for lst, val in zip(outs, (kp, vp, hrp, hip, ks, vs, hrs, his)):
            lst.append(val)
    return (xp, xs) + tuple(jnp.stack(o) for o in outs)
```

```python
import functools
import math

import jax
import jax.numpy as jnp
from jax import lax
from jax.experimental import pallas as pl
from jax.experimental.pallas import tpu as pltpu

D_MODEL = 2048
ATTN_WIDTH = 1024
SSM_WIDTH = 1024
HEAD_DIM = 64
N_HEADS = 16
N_KV_HEADS = 4
Q_PER_KV = 4
KV_WIDTH = 256
WINDOW = 128
SSM_GROUP = 16
N_SSM_GROUPS = 64
STATE_DIM = 64
N_STATE = N_SSM_GROUPS * STATE_DIM
D_FF = 8192
PROJ_WIDTH = ATTN_WIDTH + 2 * KV_WIDTH + SSM_WIDTH
EPS = 1e-6
NEG_INF = -1e30

LANES = 128
SUBLANES = 8
MXU_DIM = 256
VMEM_LIMIT = 56 * 1024 * 1024

GROUPS_PER_BLOCK = MXU_DIM // SSM_GROUP
N_SSM_BLOCKS = N_SSM_GROUPS // GROUPS_PER_BLOCK
BLOCK_STATES = GROUPS_PER_BLOCK * STATE_DIM
SCAN_LANES = 512
SSM_SUB_TILE_ROWS = 256
SSM_SUB_TILES = 2
MLP_FF_TILE = 1024

F32 = jnp.float32
BF16 = jnp.bfloat16


def _cparams(sem):
    return pltpu.CompilerParams(dimension_semantics=sem, vmem_limit_bytes=VMEM_LIMIT)


def _const_spec(shape):
    nd = len(shape)
    return pl.BlockSpec(shape, lambda *_: (0,) * nd, pipeline_mode=pl.Buffered(1))


def _discretize_kernel(are_ref, aim_ref, ldt_ref, bre_ref, bim_ref,
                       abr_ref, abi_ref, bbr_ref, bbi_ref):
    a_re = are_ref[...]
    a_im = aim_ref[...]
    dt = jnp.exp(ldt_ref[...])
    mag = jnp.exp(a_re * dt)
    ab_re = mag * jnp.cos(a_im * dt)
    ab_im = mag * jnp.sin(a_im * dt)
    abr_ref[...] = ab_re
    abi_ref[...] = ab_im
    x = ab_re - 1.0
    y = ab_im
    den = a_re * a_re + a_im * a_im
    c_re = (x * a_re + y * a_im) / den
    c_im = (y * a_re - x * a_im) / den
    b_re = bre_ref[...]
    b_im = bim_ref[...]
    bbr_ref[...] = c_re * b_re - c_im * b_im
    bbi_ref[...] = c_re * b_im + c_im * b_re


def _ssm_discretize(a_re, a_im, log_dt, b_re, b_im):
    row = jax.ShapeDtypeStruct((1, N_STATE), F32)
    mat = jax.ShapeDtypeStruct((SSM_GROUP, N_STATE), F32)
    return pl.pallas_call(
        _discretize_kernel,
        out_shape=(row, row, mat, mat),
        name="ssm_discretize",
    )(a_re, a_im, log_dt, b_re, b_im)


def _head_rmsnorm(zc, gain, ones_blk):
    sq = zc * zc
    hi = sq.astype(BF16)
    lo = (sq - hi.astype(F32)).astype(BF16)
    ss = (jnp.dot(hi, ones_blk, preferred_element_type=F32)
          + jnp.dot(lo, ones_blk, preferred_element_type=F32))
    return zc * lax.rsqrt(ss * (1.0 / HEAD_DIM) + EPS) * gain


def _in_proj_kernel(x_ref, gn_ref, wq_ref, wr_ref, gq_ref, gk_ref, q_ref, kv_ref, u_ref):
    x = x_ref[...]
    r = lax.rsqrt(jnp.mean(x * x, axis=-1, keepdims=True) + EPS)
    xn = (x * r * gn_ref[...]).astype(BF16)
    zq = jnp.dot(xn, wq_ref[...], preferred_element_type=F32)
    zr = jnp.dot(xn, wr_ref[...], preferred_element_type=F32)
    ri = lax.broadcasted_iota(jnp.int32, (MXU_DIM, MXU_DIM), 0) // HEAD_DIM
    ci = lax.broadcasted_iota(jnp.int32, (MXU_DIM, MXU_DIM), 1) // HEAD_DIM
    ones_blk = (ri == ci).astype(BF16)
    gq = gq_ref[...]
    for c in range(ATTN_WIDTH // MXU_DIM):
        sl = slice(c * MXU_DIM, (c + 1) * MXU_DIM)
        q_ref[:, sl] = _head_rmsnorm(zq[:, sl], gq, ones_blk).astype(BF16)
    kv_ref[:, :KV_WIDTH] = _head_rmsnorm(zr[:, :KV_WIDTH], gk_ref[...], ones_blk)
    kv_ref[:, KV_WIDTH:] = zr[:, KV_WIDTH:2 * KV_WIDTH]
    u_ref[...] = zr[:, 2 * KV_WIDTH:]


def _in_proj(x2d, gn, wq, wr, gq, gk, tm):
    rows = x2d.shape[0]
    return pl.pallas_call(
        _in_proj_kernel,
        grid=(rows // tm,),
        in_specs=[
            pl.BlockSpec((tm, D_MODEL), lambda i: (i, 0)),
            _const_spec((1, D_MODEL)),
            _const_spec((D_MODEL, ATTN_WIDTH)),
            _const_spec((D_MODEL, PROJ_WIDTH - ATTN_WIDTH)),
            _const_spec((1, MXU_DIM)),
            _const_spec((1, MXU_DIM)),
        ],
        out_specs=[
            pl.BlockSpec((tm, ATTN_WIDTH), lambda i: (i, 0)),
            pl.BlockSpec((tm, 2 * KV_WIDTH), lambda i: (i, 0)),
            pl.BlockSpec((tm, SSM_WIDTH), lambda i: (i, 0)),
        ],
        out_shape=(
            jax.ShapeDtypeStruct((rows, ATTN_WIDTH), BF16),
            jax.ShapeDtypeStruct((rows, 2 * KV_WIDTH), F32),
            jax.ShapeDtypeStruct((rows, SSM_WIDTH), F32),
        ),
        compiler_params=_cparams(("parallel",)),
        name="in_proj",
    )(x2d, gn, wq, wr, gq, gk)


def _alibi_slope(head):
    return 2.0 ** (-8.0 * (head + 1) / N_HEADS)


def _pair_attention(q_blocks, kc, vc, delta, valid, sink_ref, pair):
    tq = q_blocks[0].shape[0]
    lane = lax.broadcasted_iota(jnp.int32, (tq, LANES), 1)
    low = lane < HEAD_DIM
    zero = jnp.zeros((tq, LANES), BF16)
    stacked = [jnp.where(low, qb, zero) for qb in q_blocks] + [jnp.where(low, zero, qb) for qb in q_blocks]
    qs = jnp.concatenate(stacked, axis=0)
    scores = lax.dot_general(qs, kc, (((1,), (1,)), ((), ())), preferred_element_type=F32)
    probs = []
    inv_den = []
    for hh in range(2 * Q_PER_KV):
        head = (2 * pair + hh // Q_PER_KV) * Q_PER_KV + hh % Q_PER_KV
        s = scores[hh * tq:(hh + 1) * tq] * (HEAD_DIM ** -0.5) - _alibi_slope(head) * delta
        s = jnp.where(valid, s, NEG_INF)
        sink = sink_ref[head]
        m = jnp.maximum(jnp.max(s, axis=-1, keepdims=True), sink)
        p = jnp.exp(s - m)
        den = jnp.sum(p, axis=-1, keepdims=True) + jnp.exp(sink - m)
        probs.append(p.astype(BF16))
        inv_den.append(1.0 / den)
    pv = jnp.dot(jnp.concatenate(probs, axis=0), vc, preferred_element_type=F32)
    outs = []
    for r in range(Q_PER_KV):
        o_low = pv[r * tq:(r + 1) * tq] * inv_den[r]
        o_high = pv[(Q_PER_KV + r) * tq:(Q_PER_KV + r + 1) * tq] * inv_den[Q_PER_KV + r]
        outs.append(jnp.where(low, o_low, o_high))
    return outs


def _prompt_attn_kernel(sink_ref, q_ref, kvp_ref, kvc_ref, o_ref):
    n = pl.program_id(1)
    i = lax.broadcasted_iota(jnp.int32, (WINDOW, 2 * WINDOW), 0)
    j = lax.broadcasted_iota(jnp.int32, (WINDOW, 2 * WINDOW), 1)
    d = i + WINDOW - j
    valid = (d >= 0) & (d <= WINDOW) & ((j >= WINDOW) | (n > 0))
    delta = d.astype(F32)
    for pair in range(N_KV_HEADS // 2):
        ks = slice(pair * LANES, (pair + 1) * LANES)
        vs = slice(KV_WIDTH + pair * LANES, KV_WIDTH + (pair + 1) * LANES)
        kc = jnp.concatenate([kvp_ref[:, ks], kvc_ref[:, ks]], axis=0).astype(BF16)
        vc = jnp.concatenate([kvp_ref[:, vs], kvc_ref[:, vs]], axis=0).astype(BF16)
        qb = [q_ref[:, (pair * Q_PER_KV + r) * LANES:(pair * Q_PER_KV + r + 1) * LANES] for r in range(Q_PER_KV)]
        outs = _pair_attention(qb, kc, vc, delta, valid, sink_ref, pair)
        for r in range(Q_PER_KV):
            o_ref[:, (pair * Q_PER_KV + r) * LANES:(pair * Q_PER_KV + r + 1) * LANES] = outs[r].astype(BF16)


def _prompt_attention(sinks, q, kv, batch, seq):
    nb = seq // WINDOW
    return pl.pallas_call(
        _prompt_attn_kernel,
        grid=(batch, nb),
        in_specs=[
            pl.BlockSpec(memory_space=pltpu.SMEM),
            pl.BlockSpec((WINDOW, ATTN_WIDTH), lambda b, n: (b * nb + n, 0)),
            pl.BlockSpec((WINDOW, 2 * KV_WIDTH), lambda b, n: (b * nb + jnp.maximum(n - 1, 0), 0)),
            pl.BlockSpec((WINDOW, 2 * KV_WIDTH), lambda b, n: (b * nb + n, 0)),
        ],
        out_specs=pl.BlockSpec((WINDOW, ATTN_WIDTH), lambda b, n: (b * nb + n, 0)),
        out_shape=jax.ShapeDtypeStruct((batch * seq, ATTN_WIDTH), BF16),
        compiler_params=_cparams(("parallel", "parallel")),
        name="prompt_attention",
    )(sinks, q, kv, kv)


SAMPLE_ATTN_BATCH = 8


def _sample_attn_kernel(sink_ref, q_ref, ck_ref, cv_ref, kvn_ref, o_ref, *, t_new):
    rows = Q_PER_KV * t_new
    tk = WINDOW + SUBLANES
    i = lax.broadcasted_iota(jnp.int32, (rows, tk), 0) % t_new
    j = lax.broadcasted_iota(jnp.int32, (rows, tk), 1)
    d = i + WINDOW - j
    valid = (d >= 0) & (d <= WINDOW)
    delta = d.astype(F32)
    for e in range(SAMPLE_ATTN_BATCH):
        for pair in range(N_KV_HEADS // 2):
            ks = slice(pair * LANES, (pair + 1) * LANES)
            vs = slice(KV_WIDTH + pair * LANES, KV_WIDTH + (pair + 1) * LANES)
            kc = jnp.concatenate([ck_ref[e, :, ks], kvn_ref[e, :, ks]], axis=0).astype(BF16)
            vc = jnp.concatenate([cv_ref[e, :, ks], kvn_ref[e, :, vs]], axis=0).astype(BF16)
            qp = q_ref[e, pair * rows:(pair + 1) * rows, :]
            outs = _sample_pair(qp, kc, vc, delta, valid, sink_ref, pair, t_new)
            o_ref[e, pair * rows:(pair + 1) * rows, :] = outs.astype(BF16)


def _sample_pair(qp, kc, vc, delta, valid, sink_ref, pair, t_new):
    rows = qp.shape[0]
    lane = lax.broadcasted_iota(jnp.int32, (rows, LANES), 1)
    low = lane < HEAD_DIM
    zero = jnp.zeros((rows, LANES), BF16)
    qs = jnp.concatenate([jnp.where(low, qp, zero), jnp.where(low, zero, qp)], axis=0)
    scores = lax.dot_general(qs, kc, (((1,), (1,)), ((), ())), preferred_element_type=F32)
    rid = lax.broadcasted_iota(jnp.int32, (rows, 1), 0) // t_new
    halves = []
    for half in range(2):
        slope = jnp.zeros((rows, 1), F32)
        sink = jnp.zeros((rows, 1), F32)
        for r in range(Q_PER_KV):
            head = (2 * pair + half) * Q_PER_KV + r
            slope = jnp.where(rid == r, _alibi_slope(head), slope)
            sink = jnp.where(rid == r, sink_ref[head], sink)
        s = scores[half * rows:(half + 1) * rows] * (HEAD_DIM ** -0.5) - slope * delta
        s = jnp.where(valid, s, NEG_INF)
        m = jnp.maximum(jnp.max(s, axis=-1, keepdims=True), sink)
        p = jnp.exp(s - m)
        den = jnp.sum(p, axis=-1, keepdims=True) + jnp.exp(sink - m)
        pv = jnp.dot(p.astype(BF16), vc, preferred_element_type=F32)
        halves.append(pv * (1.0 / den))
    return jnp.where(low, halves[0], halves[1])


def _sample_attention(sinks, q3, ck, cv, kvn, t_new):
    n = q3.shape[0]
    bn = SAMPLE_ATTN_BATCH
    qrows = q3.shape[1]
    return pl.pallas_call(
        functools.partial(_sample_attn_kernel, t_new=t_new),
        grid=(n // bn,),
        in_specs=[
            pl.BlockSpec(memory_space=pltpu.SMEM),
            pl.BlockSpec((bn, qrows, LANES), lambda b: (b, 0, 0)),
            pl.BlockSpec((bn, WINDOW, KV_WIDTH), lambda b: (b, 0, 0)),
            pl.BlockSpec((bn, WINDOW, KV_WIDTH), lambda b: (b, 0, 0)),
            pl.BlockSpec((bn, SUBLANES, 2 * KV_WIDTH), lambda b: (b, 0, 0)),
        ],
        out_specs=pl.BlockSpec((bn, qrows, LANES), lambda b: (b, 0, 0)),
        out_shape=jax.ShapeDtypeStruct((n, qrows, LANES), BF16),
        compiler_params=_cparams(("parallel",)),
        name="sample_attention",
    )(sinks, q3, ck, cv, kvn)


def _gelu_tanh(x):
    c = math.sqrt(2.0 / math.pi)
    return 0.5 * x * (1.0 + jnp.tanh(c * (x + 0.044715 * (x * x * x))))


def _ssm_kernel(u_ref, h0_ref, ar_ref, ai_ref, wb_ref, wc_ref, d_ref, wglu_ref, bglu_ref, gs_ref,
                mix_ref, hlast_ref, s_ref, *, nb, steps, n_sub):
    @pl.when(pl.program_id(0) == 0)
    def _():
        hlast_ref[...] = h0_ref[...]

    for sub in range(n_sub):
        _ssm_sub_tile(u_ref, ar_ref, ai_ref, wb_ref, wc_ref, d_ref, wglu_ref, bglu_ref, gs_ref,
                      mix_ref, hlast_ref, s_ref, nb=nb, steps=steps, sub=sub)


def _ssm_sub_tile(u_ref, ar_ref, ai_ref, wb_ref, wc_ref, d_ref, wglu_ref, bglu_ref, gs_ref,
                  mix_ref, hlast_ref, s_ref, *, nb, steps, sub):
    rows = nb * steps
    row0 = sub * rows
    pos = slice(sub * (rows // u_ref.shape[0]), (sub + 1) * (rows // u_ref.shape[0]))
    tm_row = lax.broadcasted_iota(jnp.int32, (rows, rows), 0)
    bm_col = lax.broadcasted_iota(jnp.int32, (rows, rows), 1)
    to_time_major = (bm_col == (tm_row % nb) * steps + tm_row // nb).astype(BF16)
    bm_row = lax.broadcasted_iota(jnp.int32, (rows, rows), 0)
    tm_col = lax.broadcasted_iota(jnp.int32, (rows, rows), 1)
    to_batch_major = (bm_row == (tm_col % nb) * steps + tm_col // nb).astype(BF16)

    u_bm = u_ref[:, pos, :].reshape(rows, SSM_WIDTH)
    u_hi = u_bm.astype(BF16)
    u_lo = (u_bm - u_hi.astype(F32)).astype(BF16)
    u_hi_tm = jnp.dot(to_time_major, u_hi, preferred_element_type=F32)
    u = u_hi_tm + jnp.dot(to_time_major, u_lo, preferred_element_type=F32)
    ub = u_hi_tm.astype(BF16)
    blk_cols = 2 * BLOCK_STATES
    ys = []
    for j in range(N_SSM_BLOCKS):
        s_ref[row0:row0 + rows, j * blk_cols:(j + 1) * blk_cols] = jnp.dot(
            ub[:, j * MXU_DIM:(j + 1) * MXU_DIM], wb_ref[j], preferred_element_type=F32)
        for part in range(BLOCK_STATES // SCAN_LANES):
            rc = j * blk_cols + part * SCAN_LANES
            ic = rc + BLOCK_STATES
            sc = j * BLOCK_STATES + part * SCAN_LANES
            a_re = ar_ref[:, sc:sc + SCAN_LANES]
            a_im = ai_ref[:, sc:sc + SCAN_LANES]
            for bg in range(nb // SUBLANES):
                b0 = bg * SUBLANES
                h_re = hlast_ref[b0:b0 + SUBLANES, rc:rc + SCAN_LANES]
                h_im = hlast_ref[b0:b0 + SUBLANES, ic:ic + SCAN_LANES]
                for t in range(steps):
                    row = row0 + t * nb + b0
                    n_re = a_re * h_re - a_im * h_im + s_ref[row:row + SUBLANES, rc:rc + SCAN_LANES]
                    n_im = a_re * h_im + a_im * h_re + s_ref[row:row + SUBLANES, ic:ic + SCAN_LANES]
                    s_ref[row:row + SUBLANES, rc:rc + SCAN_LANES] = n_re
                    s_ref[row:row + SUBLANES, ic:ic + SCAN_LANES] = n_im
                    h_re, h_im = n_re, n_im
                hlast_ref[b0:b0 + SUBLANES, rc:rc + SCAN_LANES] = h_re
                hlast_ref[b0:b0 + SUBLANES, ic:ic + SCAN_LANES] = h_im
        hb = s_ref[row0:row0 + rows, j * blk_cols:(j + 1) * blk_cols].astype(BF16)
        ys.append(jnp.dot(hb, wc_ref[j], preferred_element_type=F32))

    y = jnp.concatenate(ys, axis=1) + d_ref[...] * u
    g = _gelu_tanh(y)
    gate = jnp.dot(g.astype(BF16), wglu_ref[...], preferred_element_type=F32) + bglu_ref[...]
    so = g * jax.nn.sigmoid(gate)
    r = lax.rsqrt(jnp.mean(so * so, axis=-1, keepdims=True) + EPS)
    mix_tm = (so * r * gs_ref[...]).astype(BF16)
    mix_bm = jnp.dot(to_batch_major, mix_tm, preferred_element_type=F32).astype(BF16)
    mix_ref[:, pos, :] = mix_bm.reshape(mix_ref.shape[0], rows // mix_ref.shape[0], SSM_WIDTH)


def _ssm_mixer(u3, h0, ar8, ai8, wb, wc, d, wglu_b, bglu, gs, nb, steps, n_sub):
    nbv, seq, _ = u3.shape
    blk_rows = n_sub * nb * steps // nbv
    assert blk_rows % SUBLANES == 0 and seq % blk_rows == 0
    cols = 2 * N_STATE
    tr = n_sub * nb * steps
    return pl.pallas_call(
        functools.partial(_ssm_kernel, nb=nb, steps=steps, n_sub=n_sub),
        grid=(seq // blk_rows,),
        in_specs=[
            pl.BlockSpec((nbv, blk_rows, SSM_WIDTH), lambda i: (0, i, 0)),
            _const_spec((nb, cols)),
            _const_spec((SUBLANES, N_STATE)),
            _const_spec((SUBLANES, N_STATE)),
            _const_spec((N_SSM_BLOCKS, MXU_DIM, 2 * BLOCK_STATES)),
            _const_spec((N_SSM_BLOCKS, 2 * BLOCK_STATES, MXU_DIM)),
            _const_spec((1, SSM_WIDTH)),
            _const_spec((SSM_WIDTH, SSM_WIDTH)),
            _const_spec((1, SSM_WIDTH)),
            _const_spec((1, SSM_WIDTH)),
        ],
        out_specs=[
            pl.BlockSpec((nbv, blk_rows, SSM_WIDTH), lambda i: (0, i, 0)),
            pl.BlockSpec((nb, cols), lambda i: (0, 0)),
        ],
        out_shape=(
            jax.ShapeDtypeStruct((nbv, seq, SSM_WIDTH), BF16),
            jax.ShapeDtypeStruct((nb, cols), F32),
        ),
        scratch_shapes=[pltpu.VMEM((tr, cols), F32)],
        compiler_params=_cparams(("arbitrary",)),
        name="ssm_mixer",
    )(u3, h0, ar8, ai8, wb, wc, d, wglu_b, bglu, gs)


def _out_mlp_kernel(x_ref, a_ref, ga_ref, ms_ref, woa_ref, wos_ref, gm_ref, wup_ref, wdn_ref,
                    y_ref, hn_ref):
    @pl.when(pl.program_id(1) == 0)
    def _():
        a = a_ref[...].astype(F32)
        ra = lax.rsqrt(jnp.mean(a * a, axis=-1, keepdims=True) + EPS)
        ma = (a * ra * ga_ref[...]).astype(BF16)
        h = (x_ref[...]
             + jnp.dot(ma, woa_ref[...], preferred_element_type=F32)
             + jnp.dot(ms_ref[...], wos_ref[...], preferred_element_type=F32))
        y_ref[...] = h
        rh = lax.rsqrt(jnp.mean(h * h, axis=-1, keepdims=True) + EPS)
        hn_ref[...] = (h * rh * gm_ref[...]).astype(BF16)

    t = jnp.dot(hn_ref[...], wup_ref[...], preferred_element_type=F32)
    t = jnp.maximum(t, 0.0)
    t = (t * t).astype(BF16)
    y_ref[...] += jnp.dot(t, wdn_ref[...], preferred_element_type=F32)


def _out_mlp(x2d, attn, ga, mix_s, woa, wos, gm, wup, wdn, tm, tf):
    rows = x2d.shape[0]
    return pl.pallas_call(
        _out_mlp_kernel,
        grid=(rows // tm, D_FF // tf),
        in_specs=[
            pl.BlockSpec((tm, D_MODEL), lambda i, j: (i, 0)),
            pl.BlockSpec((tm, ATTN_WIDTH), lambda i, j: (i, 0)),
            _const_spec((1, ATTN_WIDTH)),
            pl.BlockSpec((tm, SSM_WIDTH), lambda i, j: (i, 0)),
            _const_spec((ATTN_WIDTH, D_MODEL)),
            _const_spec((SSM_WIDTH, D_MODEL)),
            _const_spec((1, D_MODEL)),
            pl.BlockSpec((D_MODEL, tf), lambda i, j: (0, j)),
            pl.BlockSpec((tf, D_MODEL), lambda i, j: (j, 0)),
        ],
        out_specs=pl.BlockSpec((tm, D_MODEL), lambda i, j: (i, 0)),
        out_shape=jax.ShapeDtypeStruct((rows, D_MODEL), F32),
        scratch_shapes=[pltpu.VMEM((tm, D_MODEL), BF16)],
        compiler_params=_cparams(("parallel", "arbitrary")),
        name="out_mlp",
    )(x2d, attn, ga, mix_s, woa, wos, gm, wup, wdn)


def _pair_heads(a, axis):
    shape = a.shape
    split = shape[:axis] + (N_KV_HEADS // 2, 2, Q_PER_KV, HEAD_DIM) + shape[axis + 1:]
    return jnp.swapaxes(a.reshape(split), axis + 1, axis + 2).reshape(shape)


def _block_diag_weights(bb_re, bb_im, c_re, c_im):
    gpb = GROUPS_PER_BLOCK
    eye = jnp.eye(gpb, dtype=F32)

    def b_blocks(bb):
        b = bb.reshape(SSM_GROUP, N_SSM_BLOCKS, gpb, STATE_DIM)
        w = jnp.einsum('hjgp,gk->jghkp', b, eye)
        return w.reshape(N_SSM_BLOCKS, gpb * SSM_GROUP, BLOCK_STATES)

    def c_blocks(c):
        cc = c.reshape(N_SSM_BLOCKS, gpb, SSM_GROUP, STATE_DIM)
        w = jnp.einsum('jghp,gk->jgpkh', cc, eye)
        return w.reshape(N_SSM_BLOCKS, BLOCK_STATES, gpb * SSM_GROUP)

    wb = jnp.concatenate([b_blocks(bb_re), b_blocks(bb_im)], axis=2).astype(BF16)
    wc = jnp.concatenate([c_blocks(c_re), -c_blocks(c_im)], axis=1).astype(BF16)
    return wb, wc


def _state_to_cols(h_re, h_im):
    n = h_re.shape[0]
    re = h_re.reshape(n, N_SSM_BLOCKS, 1, BLOCK_STATES)
    im = h_im.reshape(n, N_SSM_BLOCKS, 1, BLOCK_STATES)
    return jnp.concatenate([re, im], axis=2).reshape(n, 2 * N_STATE)


def _cols_to_state(h):
    n = h.shape[0]
    h4 = h.reshape(n, N_SSM_BLOCKS, 2, BLOCK_STATES)
    re = h4[:, :, 0].reshape(n, N_SSM_GROUPS, STATE_DIM)
    im = h4[:, :, 1].reshape(n, N_SSM_GROUPS, STATE_DIM)
    return re, im


def _layer(x, cache_k, cache_v, h0_re, h0_im, p):
    n, t = x.shape[:2]
    rows = n * t
    tm = 512
    x2d = x.reshape(rows, D_MODEL)
    q, kv, u = _in_proj(x2d, p['gn'], p['wq'], p['wr'], p['gq'], p['gk'], tm)
    kv3 = kv.reshape(n, t, 2 * KV_WIDTH)

    if cache_k is None:
        attn = _prompt_attention(p['sinks'], q, kv, n, t)
        tail = kv3[:, t - WINDOW:]
        new_k = tail[..., :KV_WIDTH].reshape(n, WINDOW, N_KV_HEADS, HEAD_DIM)
        new_v = tail[..., KV_WIDTH:].reshape(n, WINDOW, N_KV_HEADS, HEAD_DIM)
        u3 = u.reshape(n, t, SSM_WIDTH)
        ssm_steps, ssm_subs = SSM_SUB_TILE_ROWS // n, SSM_SUB_TILES
    else:
        k = kv3[..., :KV_WIDTH].reshape(n, t, N_KV_HEADS, HEAD_DIM)
        v = kv3[..., KV_WIDTH:].reshape(n, t, N_KV_HEADS, HEAD_DIM)
        nblk = ATTN_WIDTH // LANES
        q3 = q.reshape(n, t, nblk, LANES).transpose(0, 2, 1, 3).reshape(n, nblk * t, LANES)
        kvn = jnp.pad(kv.reshape(n, t, 2 * KV_WIDTH), ((0, 0), (0, SUBLANES - t), (0, 0)))
        ck = cache_k.reshape(n, WINDOW, KV_WIDTH)
        cv = cache_v.reshape(n, WINDOW, KV_WIDTH)
        a3 = _sample_attention(p['sinks'], q3, ck, cv, kvn, t)
        attn = a3.reshape(n, nblk, t, LANES).transpose(0, 2, 1, 3).reshape(rows, ATTN_WIDTH)
        new_k = jnp.concatenate([cache_k[:, t:], k], axis=1)
        new_v = jnp.concatenate([cache_v[:, t:], v], axis=1)
        u3 = u.reshape(1, rows, SSM_WIDTH)
        ssm_steps, ssm_subs = t, 1

    mix3, h_last = _ssm_mixer(u3, _state_to_cols(h0_re, h0_im), p['ar8'], p['ai8'], p['wb'], p['wc'],
                              p['d'], p['w_glu'], p['b_glu'], p['gs'], n, ssm_steps, ssm_subs)
    mix_s = mix3.reshape(rows, SSM_WIDTH)
    h_re, h_im = _cols_to_state(h_last)

    y = _out_mlp(x2d, attn, p['ga'], mix_s, p['woa'], p['wos'], p['gm'], p['w_up'], p['w_down'], tm, MLP_FF_TILE)
    return y.reshape(n, t, D_MODEL), new_k, new_v, h_re, h_im


def _prepare_params(l, attn_norm_g, w_in, q_norm_g, k_norm_g, attn_sinks,
                    ssm_A_re, ssm_A_im, ssm_log_dt, ssm_B_re, ssm_B_im, ssm_C_re, ssm_C_im, ssm_D,
                    w_glu, b_glu, attn_out_g, ssm_out_g, w_out, mlp_norm_g, w_mlp_up, w_mlp_down):
    w_in_l = w_in[l]
    heads_per_blk = MXU_DIM // HEAD_DIM

    a_re = ssm_A_re[l].reshape(1, N_STATE)
    a_im = ssm_A_im[l].reshape(1, N_STATE)
    ldt = jnp.broadcast_to(ssm_log_dt[l][:, None], (N_SSM_GROUPS, STATE_DIM)).reshape(1, N_STATE)
    b_re = ssm_B_re[l].transpose(2, 0, 1).reshape(SSM_GROUP, N_STATE)
    b_im = ssm_B_im[l].transpose(2, 0, 1).reshape(SSM_GROUP, N_STATE)
    ab_re, ab_im, bb_re, bb_im = _ssm_discretize(a_re, a_im, ldt, b_re, b_im)
    wb, wc = _block_diag_weights(bb_re, bb_im, ssm_C_re[l], ssm_C_im[l])

    w_out_l = w_out[l]
    return dict(
        gn=attn_norm_g[l].reshape(1, D_MODEL),
        wq=_pair_heads(w_in_l[:, :ATTN_WIDTH], 1).astype(BF16),
        wr=w_in_l[:, ATTN_WIDTH:].astype(BF16),
        gq=jnp.tile(q_norm_g[l], heads_per_blk).reshape(1, MXU_DIM),
        gk=jnp.tile(k_norm_g[l], heads_per_blk).reshape(1, MXU_DIM),
        sinks=attn_sinks[l].astype(F32),
        ar8=jnp.broadcast_to(ab_re, (SUBLANES, N_STATE)),
        ai8=jnp.broadcast_to(ab_im, (SUBLANES, N_STATE)),
        wb=wb, wc=wc,
        d=ssm_D[l].reshape(1, SSM_WIDTH),
        w_glu=w_glu[l].astype(BF16),
        b_glu=b_glu[l].reshape(1, SSM_WIDTH),
        gs=ssm_out_g[l].reshape(1, SSM_WIDTH),
        ga=_pair_heads(attn_out_g[l], 0).reshape(1, ATTN_WIDTH),
        woa=_pair_heads(w_out_l[:ATTN_WIDTH], 0).astype(BF16),
        wos=w_out_l[ATTN_WIDTH:].astype(BF16),
        gm=mlp_norm_g[l].reshape(1, D_MODEL),
        w_up=w_mlp_up[l].astype(BF16),
        w_down=w_mlp_down[l].astype(BF16),
    )


def kernel(x_prompt, x_sample, cache_k, cache_v, state_ssm_re, state_ssm_im, attn_norm_g, w_in, q_norm_g, k_norm_g, attn_sinks, ssm_A_re, ssm_A_im, ssm_log_dt, ssm_B_re, ssm_B_im, ssm_C_re, ssm_C_im, ssm_D, w_glu, b_glu, attn_out_g, ssm_out_g, w_out, mlp_norm_g, w_mlp_up, w_mlp_down):
    depth = w_in.shape[0]
    xp, xs = x_prompt, x_sample
    zeros_state = jnp.zeros((x_prompt.shape[0], N_SSM_GROUPS, STATE_DIM), F32)
    outs = [[] for _ in range(8)]
    for l in range(depth):
        p = _prepare_params(l, attn_norm_g, w_in, q_norm_g, k_norm_g, attn_sinks,
                            ssm_A_re, ssm_A_im, ssm_log_dt, ssm_B_re, ssm_B_im, ssm_C_re, ssm_C_im, ssm_D,
                            w_glu, b_glu, attn_out_g, ssm_out_g, w_out, mlp_norm_g, w_mlp_up, w_mlp_down)
        xp, kp, vp, hrp, hip = _layer(xp, None, None, zeros_state, zeros_state, p)
        xs, ks, vs, hrs, his = _layer(xs, cache_k[l], cache_v[l], state_ssm_re[l], state_ssm_im[l], p)
        for lst, val in zip(outs, (kp, vp, hrp, hip, ks, vs, hrs, his)):
            lst.append(val)
    return (xp, xs) + tuple(jnp.stack(o) for o in outs)
```

```python
import functools
import math

import jax
import jax.numpy as jnp
from jax import lax
from jax.experimental import pallas as pl
from jax.experimental.pallas import tpu as pltpu

D_MODEL = 2048
ATTN_WIDTH = 1024
SSM_WIDTH = 1024
HEAD_DIM = 64
N_HEADS = 16
N_KV_HEADS = 4
Q_PER_KV = 4
KV_WIDTH = 256
WINDOW = 128
SSM_GROUP = 16
N_SSM_GROUPS = 64
STATE_DIM = 64
N_STATE = N_SSM_GROUPS * STATE_DIM
D_FF = 8192
PROJ_WIDTH = ATTN_WIDTH + 2 * KV_WIDTH + SSM_WIDTH
EPS = 1e-6
NEG_INF = -1e30

LANES = 128
SUBLANES = 8
MXU_DIM = 256
VMEM_LIMIT = 56 * 1024 * 1024

GROUPS_PER_BLOCK = MXU_DIM // SSM_GROUP
N_SSM_BLOCKS = N_SSM_GROUPS // GROUPS_PER_BLOCK
BLOCK_STATES = GROUPS_PER_BLOCK * STATE_DIM
SCAN_LANES = 512
SSM_SUB_TILE_ROWS = 256
SSM_SUB_TILES = 2
MLP_FF_TILE = 1024

F32 = jnp.float32
BF16 = jnp.bfloat16


def _cparams(sem):
    return pltpu.CompilerParams(dimension_semantics=sem, vmem_limit_bytes=VMEM_LIMIT)


def _const_spec(shape):
    nd = len(shape)
    return pl.BlockSpec(shape, lambda *_: (0,) * nd, pipeline_mode=pl.Buffered(1))


def _discretize_kernel(are_ref, aim_ref, ldt_ref, bre_ref, bim_ref,
                       abr_ref, abi_ref, bbr_ref, bbi_ref):
    a_re = are_ref[...]
    a_im = aim_ref[...]
    dt = jnp.exp(ldt_ref[...])
    mag = jnp.exp(a_re * dt)
    ab_re = mag * jnp.cos(a_im * dt)
    ab_im = mag * jnp.sin(a_im * dt)
    abr_ref[...] = ab_re
    abi_ref[...] = ab_im
    x = ab_re - 1.0
    y = ab_im
    den = a_re * a_re + a_im * a_im
    c_re = (x * a_re + y * a_im) / den
    c_im = (y * a_re - x * a_im) / den
    b_re = bre_ref[...]
    b_im = bim_ref[...]
    bbr_ref[...] = c_re * b_re - c_im * b_im
    bbi_ref[...] = c_re * b_im + c_im * b_re


def _ssm_discretize(a_re, a_im, log_dt, b_re, b_im):
    row = jax.ShapeDtypeStruct((1, N_STATE), F32)
    mat = jax.ShapeDtypeStruct((SSM_GROUP, N_STATE), F32)
    return pl.pallas_call(
        _discretize_kernel,
        out_shape=(row, row, mat, mat),
        name="ssm_discretize",
    )(a_re, a_im, log_dt, b_re, b_im)


def _head_rmsnorm(zc, gain, ones_blk):
    sq = zc * zc
    hi = sq.astype(BF16)
    lo = (sq - hi.astype(F32)).astype(BF16)
    ss = (jnp.dot(hi, ones_blk, preferred_element_type=F32)
          + jnp.dot(lo, ones_blk, preferred_element_type=F32))
    return zc * lax.rsqrt(ss * (1.0 / HEAD_DIM) + EPS) * gain


def _in_proj_kernel(x_ref, gn_ref, wq_ref, wr_ref, gq_ref, gk_ref, q_ref, kv_ref, u_ref):
    x = x_ref[...]
    r = lax.rsqrt(jnp.mean(x * x, axis=-1, keepdims=True) + EPS)
    xn = (x * r * gn_ref[...]).astype(BF16)
    zq = jnp.dot(xn, wq_ref[...], preferred_element_type=F32)
    zr = jnp.dot(xn, wr_ref[...], preferred_element_type=F32)
    ri = lax.broadcasted_iota(jnp.int32, (MXU_DIM, MXU_DIM), 0) // HEAD_DIM
    ci = lax.broadcasted_iota(jnp.int32, (MXU_DIM, MXU_DIM), 1) // HEAD_DIM
    ones_blk = (ri == ci).astype(BF16)
    gq = gq_ref[...]
    for c in range(ATTN_WIDTH // MXU_DIM):
        sl = slice(c * MXU_DIM, (c + 1) * MXU_DIM)
        q_ref[:, sl] = _head_rmsnorm(zq[:, sl], gq, ones_blk).astype(BF16)
    kv_ref[:, :KV_WIDTH] = _head_rmsnorm(zr[:, :KV_WIDTH], gk_ref[...], ones_blk)
    kv_ref[:, KV_WIDTH:] = zr[:, KV_WIDTH:2 * KV_WIDTH]
    u_ref[...] = zr[:, 2 * KV_WIDTH:]


def _in_proj(x2d, gn, wq, wr, gq, gk, tm):
    rows = x2d.shape[0]
    return pl.pallas_call(
        _in_proj_kernel,
        grid=(rows // tm,),
        in_specs=[
            pl.BlockSpec((tm, D_MODEL), lambda i: (i, 0)),
            _const_spec((1, D_MODEL)),
            _const_spec((D_MODEL, ATTN_WIDTH)),
            _const_spec((D_MODEL, PROJ_WIDTH - ATTN_WIDTH)),
            _const_spec((1, MXU_DIM)),
            _const_spec((1, MXU_DIM)),
        ],
        out_specs=[
            pl.BlockSpec((tm, ATTN_WIDTH), lambda i: (i, 0)),
            pl.BlockSpec((tm, 2 * KV_WIDTH), lambda i: (i, 0)),
            pl.BlockSpec((tm, SSM_WIDTH), lambda i: (i, 0)),
        ],
        out_shape=(
            jax.ShapeDtypeStruct((rows, ATTN_WIDTH), BF16),
            jax.ShapeDtypeStruct((rows, 2 * KV_WIDTH), F32),
            jax.ShapeDtypeStruct((rows, SSM_WIDTH), F32),
        ),
        compiler_params=_cparams(("parallel",)),
        name="in_proj",
    )(x2d, gn, wq, wr, gq, gk)


def _alibi_slope(head):
    return 2.0 ** (-8.0 * (head + 1) / N_HEADS)


def _pair_attention(q_blocks, kc, vc, delta, valid, sink_ref, pair):
    tq = q_blocks[0].shape[0]
    lane = lax.broadcasted_iota(jnp.int32, (tq, LANES), 1)
    low = lane < HEAD_DIM
    zero = jnp.zeros((tq, LANES), BF16)
    stacked = [jnp.where(low, qb, zero) for qb in q_blocks] + [jnp.where(low, zero, qb) for qb in q_blocks]
    qs = jnp.concatenate(stacked, axis=0)
    scores = lax.dot_general(qs, kc, (((1,), (1,)), ((), ())), preferred_element_type=F32)
    probs = []
    inv_den = []
    for hh in range(2 * Q_PER_KV):
        head = (2 * pair + hh // Q_PER_KV) * Q_PER_KV + hh % Q_PER_KV
        s = scores[hh * tq:(hh + 1) * tq] * (HEAD_DIM ** -0.5) - _alibi_slope(head) * delta
        s = jnp.where(valid, s, NEG_INF)
        sink = sink_ref[head]
        m = jnp.maximum(jnp.max(s, axis=-1, keepdims=True), sink)
        p = jnp.exp(s - m)
        den = jnp.sum(p, axis=-1, keepdims=True) + jnp.exp(sink - m)
        probs.append(p.astype(BF16))
        inv_den.append(1.0 / den)
    pv = jnp.dot(jnp.concatenate(probs, axis=0), vc, preferred_element_type=F32)
    outs = []
    for r in range(Q_PER_KV):
        o_low = pv[r * tq:(r + 1) * tq] * inv_den[r]
        o_high = pv[(Q_PER_KV + r) * tq:(Q_PER_KV + r + 1) * tq] * inv_den[Q_PER_KV + r]
        outs.append(jnp.where(low, o_low, o_high))
    return outs


def _prompt_attn_kernel(sink_ref, q_ref, kvp_ref, kvc_ref, o_ref):
    n = pl.program_id(1)
    i = lax.broadcasted_iota(jnp.int32, (WINDOW, 2 * WINDOW), 0)
    j = lax.broadcasted_iota(jnp.int32, (WINDOW, 2 * WINDOW), 1)
    d = i + WINDOW - j
    valid = (d >= 0) & (d <= WINDOW) & ((j >= WINDOW) | (n > 0))
    delta = d.astype(F32)
    for pair in range(N_KV_HEADS // 2):
        ks = slice(pair * LANES, (pair + 1) * LANES)
        vs = slice(KV_WIDTH + pair * LANES, KV_WIDTH + (pair + 1) * LANES)
        kc = jnp.concatenate([kvp_ref[:, ks], kvc_ref[:, ks]], axis=0).astype(BF16)
        vc = jnp.concatenate([kvp_ref[:, vs], kvc_ref[:, vs]], axis=0).astype(BF16)
        qb = [q_ref[:, (pair * Q_PER_KV + r) * LANES:(pair * Q_PER_KV + r + 1) * LANES] for r in range(Q_PER_KV)]
        outs = _pair_attention(qb, kc, vc, delta, valid, sink_ref, pair)
        for r in range(Q_PER_KV):
            o_ref[:, (pair * Q_PER_KV + r) * LANES:(pair * Q_PER_KV + r + 1) * LANES] = outs[r].astype(BF16)


def _prompt_attention(sinks, q, kv, batch, seq):
    nb = seq // WINDOW
    return pl.pallas_call(
        _prompt_attn_kernel,
        grid=(batch, nb),
        in_specs=[
            pl.BlockSpec(memory_space=pltpu.SMEM),
            pl.BlockSpec((WINDOW, ATTN_WIDTH), lambda b, n: (b * nb + n, 0)),
            pl.BlockSpec((WINDOW, 2 * KV_WIDTH), lambda b, n: (b * nb + jnp.maximum(n - 1, 0), 0)),
            pl.BlockSpec((WINDOW, 2 * KV_WIDTH), lambda b, n: (b * nb + n, 0)),
        ],
        out_specs=pl.BlockSpec((WINDOW, ATTN_WIDTH), lambda b, n: (b * nb + n, 0)),
        out_shape=jax.ShapeDtypeStruct((batch * seq, ATTN_WIDTH), BF16),
        compiler_params=_cparams(("parallel", "parallel")),
        name="prompt_attention",
    )(sinks, q, kv, kv)


SAMPLE_ATTN_BATCH = 16
SAMPLE_ATTN_UNROLL = 2


def _sample_attn_kernel(sink_ref, q_ref, ckt_ref, cvt_ref, kvn_ref, o_ref, nkt_ref, nvt_ref, *, t_new):
    rows = Q_PER_KV * t_new
    tk = 2 * WINDOW
    i = lax.broadcasted_iota(jnp.int32, (rows, tk), 0) % t_new
    c = lax.broadcasted_iota(jnp.int32, (rows, tk), 1)
    is_new = c >= tk - t_new
    j = jnp.where(c < WINDOW, c, c - (WINDOW - t_new))
    d = i + WINDOW - j
    valid = (d >= 0) & (d <= WINDOW) & ((c < WINDOW) | is_new)
    delta = d.astype(F32)
    lane_q = lax.broadcasted_iota(jnp.int32, (rows, LANES), 1)
    low = lane_q < HEAD_DIM
    zero_q = jnp.zeros((rows, LANES), BF16)
    lane_w = lax.broadcasted_iota(jnp.int32, (WINDOW, LANES), 1)
    tail = lane_w >= WINDOW - t_new
    rid = lax.broadcasted_iota(jnp.int32, (rows, 1), 0) // t_new
    alibi, sinks = [], []
    for pair in range(N_KV_HEADS // 2):
        sl, sk = [], []
        for half in range(2):
            slope = jnp.zeros((rows, 1), F32)
            sink = jnp.zeros((rows, 1), F32)
            for r in range(Q_PER_KV):
                head = (2 * pair + half) * Q_PER_KV + r
                slope = jnp.where(rid == r, _alibi_slope(head), slope)
                sink = jnp.where(rid == r, sink_ref[head], sink)
            sl.append(slope)
            sk.append(sink)
        alibi.append(jnp.concatenate(sl, axis=0) * jnp.concatenate([delta, delta], axis=0))
        sinks.append(jnp.concatenate(sk, axis=0))
    valid2 = jnp.concatenate([valid, valid], axis=0)

    lead_zero = jnp.zeros((WINDOW - SUBLANES, LANES), F32)

    def new_rows_transposed(e, lanes):
        return jnp.concatenate([lead_zero, kvn_ref[e, :, lanes]], axis=0).T

    def shifted(old_t, new_t):
        return jnp.where(tail, new_t, pltpu.roll(old_t, WINDOW - t_new, axis=1))

    def body(e, carry):
        for pair in range(N_KV_HEADS // 2):
            heads = slice(2 * pair, 2 * pair + 2)
            kt = ckt_ref[e, heads].reshape(2 * HEAD_DIM, WINDOW)
            vt = cvt_ref[e, heads].reshape(2 * HEAD_DIM, WINDOW)
            knt = new_rows_transposed(e, slice(pair * LANES, (pair + 1) * LANES))
            vnt = new_rows_transposed(e, slice(KV_WIDTH + pair * LANES, KV_WIDTH + (pair + 1) * LANES))
            nkt_ref[e, heads] = shifted(kt, knt).reshape(2, HEAD_DIM, WINDOW)
            nvt_ref[e, heads] = shifted(vt, vnt).reshape(2, HEAD_DIM, WINDOW)

            qp = q_ref[e, pair * rows:(pair + 1) * rows, :]
            qs = jnp.concatenate([jnp.where(low, qp, zero_q), jnp.where(low, zero_q, qp)], axis=0)
            kt_all = jnp.concatenate([kt, knt], axis=1).astype(BF16)
            vt_all = jnp.concatenate([vt, vnt], axis=1).astype(BF16)
            scores = jnp.dot(qs, kt_all, preferred_element_type=F32)
            s = jnp.where(valid2, scores * (HEAD_DIM ** -0.5) - alibi[pair], NEG_INF)
            m = jnp.maximum(jnp.max(s, axis=-1, keepdims=True), sinks[pair])
            p = jnp.exp(s - m)
            den = jnp.sum(p, axis=-1, keepdims=True) + jnp.exp(sinks[pair] - m)
            pv = lax.dot_general(p.astype(BF16), vt_all, (((1,), (1,)), ((), ())),
                                 preferred_element_type=F32)
            pv = pv * (1.0 / den)
            o_ref[e, pair * rows:(pair + 1) * rows, :] = jnp.where(low, pv[:rows], pv[rows:]).astype(BF16)
        return carry

    lax.fori_loop(0, q_ref.shape[0], body, 0, unroll=SAMPLE_ATTN_UNROLL)


def _sample_attention(sinks, q3, ckt, cvt, kvn, t_new):
    n = q3.shape[0]
    bn = SAMPLE_ATTN_BATCH
    qrows = q3.shape[1]
    cache_spec = pl.BlockSpec((bn, N_KV_HEADS, HEAD_DIM, WINDOW), lambda b: (b, 0, 0, 0))
    cache_shape = jax.ShapeDtypeStruct((n, N_KV_HEADS, HEAD_DIM, WINDOW), F32)
    return pl.pallas_call(
        functools.partial(_sample_attn_kernel, t_new=t_new),
        grid=(n // bn,),
        in_specs=[
            pl.BlockSpec(memory_space=pltpu.SMEM),
            pl.BlockSpec((bn, qrows, LANES), lambda b: (b, 0, 0)),
            cache_spec,
            cache_spec,
            pl.BlockSpec((bn, SUBLANES, 2 * KV_WIDTH), lambda b: (b, 0, 0)),
        ],
        out_specs=[pl.BlockSpec((bn, qrows, LANES), lambda b: (b, 0, 0)), cache_spec, cache_spec],
        out_shape=(jax.ShapeDtypeStruct((n, qrows, LANES), BF16), cache_shape, cache_shape),
        compiler_params=_cparams(("parallel",)),
        name="sample_attention",
    )(sinks, q3, ckt, cvt, kvn)


def _gelu_tanh(x):
    c = math.sqrt(2.0 / math.pi)
    return 0.5 * x * (1.0 + jnp.tanh(c * (x + 0.044715 * (x * x * x))))


def _ssm_kernel(u_ref, h0re_ref, h0im_ref, ar_ref, ai_ref, wb_ref, wc_ref, d_ref, wglu_ref, bglu_ref, gs_ref,
                mix_ref, hre_ref, him_ref, s_ref, *, nb, steps, n_sub):
    @pl.when(pl.program_id(0) == 0)
    def _():
        hre_ref[...] = h0re_ref[...]
        him_ref[...] = h0im_ref[...]

    for sub in range(n_sub):
        _ssm_sub_tile(u_ref, ar_ref, ai_ref, wb_ref, wc_ref, d_ref, wglu_ref, bglu_ref, gs_ref,
                      mix_ref, hre_ref, him_ref, s_ref, nb=nb, steps=steps, sub=sub)


def _ssm_sub_tile(u_ref, ar_ref, ai_ref, wb_ref, wc_ref, d_ref, wglu_ref, bglu_ref, gs_ref,
                  mix_ref, hre_ref, him_ref, s_ref, *, nb, steps, sub):
    rows = nb * steps
    row0 = sub * rows
    pos = slice(sub * (rows // u_ref.shape[0]), (sub + 1) * (rows // u_ref.shape[0]))
    tm_row = lax.broadcasted_iota(jnp.int32, (rows, rows), 0)
    bm_col = lax.broadcasted_iota(jnp.int32, (rows, rows), 1)
    to_time_major = (bm_col == (tm_row % nb) * steps + tm_row // nb).astype(BF16)
    bm_row = lax.broadcasted_iota(jnp.int32, (rows, rows), 0)
    tm_col = lax.broadcasted_iota(jnp.int32, (rows, rows), 1)
    to_batch_major = (bm_row == (tm_col % nb) * steps + tm_col // nb).astype(BF16)

    u_bm = u_ref[:, pos, :].reshape(rows, SSM_WIDTH)
    u_hi = u_bm.astype(BF16)
    u_lo = (u_bm - u_hi.astype(F32)).astype(BF16)
    u_hi_tm = jnp.dot(to_time_major, u_hi, preferred_element_type=F32)
    u = u_hi_tm + jnp.dot(to_time_major, u_lo, preferred_element_type=F32)
    ub = u_hi_tm.astype(BF16)
    blk_cols = 2 * BLOCK_STATES
    ys = []
    for j in range(N_SSM_BLOCKS):
        s_ref[row0:row0 + rows, j * blk_cols:(j + 1) * blk_cols] = jnp.dot(
            ub[:, j * MXU_DIM:(j + 1) * MXU_DIM], wb_ref[j], preferred_element_type=F32)
        for part in range(BLOCK_STATES // SCAN_LANES):
            rc = j * blk_cols + part * SCAN_LANES
            ic = rc + BLOCK_STATES
            sc = j * BLOCK_STATES + part * SCAN_LANES
            a_re = ar_ref[:, sc:sc + SCAN_LANES]
            a_im = ai_ref[:, sc:sc + SCAN_LANES]
            for bg in range(nb // SUBLANES):
                b0 = bg * SUBLANES
                h_re = hre_ref[b0:b0 + SUBLANES, sc:sc + SCAN_LANES]
                h_im = him_ref[b0:b0 + SUBLANES, sc:sc + SCAN_LANES]
                for t in range(steps):
                    row = row0 + t * nb + b0
                    n_re = a_re * h_re - a_im * h_im + s_ref[row:row + SUBLANES, rc:rc + SCAN_LANES]
                    n_im = a_re * h_im + a_im * h_re + s_ref[row:row + SUBLANES, ic:ic + SCAN_LANES]
                    s_ref[row:row + SUBLANES, rc:rc + SCAN_LANES] = n_re
                    s_ref[row:row + SUBLANES, ic:ic + SCAN_LANES] = n_im
                    h_re, h_im = n_re, n_im
                hre_ref[b0:b0 + SUBLANES, sc:sc + SCAN_LANES] = h_re
                him_ref[b0:b0 + SUBLANES, sc:sc + SCAN_LANES] = h_im
        hb = s_ref[row0:row0 + rows, j * blk_cols:(j + 1) * blk_cols].astype(BF16)
        ys.append(jnp.dot(hb, wc_ref[j], preferred_element_type=F32))

    y = jnp.concatenate(ys, axis=1) + d_ref[...] * u
    g = _gelu_tanh(y)
    gate = jnp.dot(g.astype(BF16), wglu_ref[...], preferred_element_type=F32) + bglu_ref[...]
    so = g * jax.nn.sigmoid(gate)
    r = lax.rsqrt(jnp.mean(so * so, axis=-1, keepdims=True) + EPS)
    mix_tm = (so * r * gs_ref[...]).astype(BF16)
    mix_bm = jnp.dot(to_batch_major, mix_tm, preferred_element_type=F32).astype(BF16)
    mix_ref[:, pos, :] = mix_bm.reshape(mix_ref.shape[0], rows // mix_ref.shape[0], SSM_WIDTH)


def _ssm_mixer(u3, h0_re, h0_im, ar8, ai8, wb, wc, d, wglu_b, bglu, gs, nb, steps, n_sub):
    nbv, seq, _ = u3.shape
    blk_rows = n_sub * nb * steps // nbv
    assert blk_rows % SUBLANES == 0 and seq % blk_rows == 0
    cols = 2 * N_STATE
    tr = n_sub * nb * steps
    return pl.pallas_call(
        functools.partial(_ssm_kernel, nb=nb, steps=steps, n_sub=n_sub),
        grid=(seq // blk_rows,),
        in_specs=[
            pl.BlockSpec((nbv, blk_rows, SSM_WIDTH), lambda i: (0, i, 0)),
            _const_spec((nb, N_STATE)),
            _const_spec((nb, N_STATE)),
            _const_spec((SUBLANES, N_STATE)),
            _const_spec((SUBLANES, N_STATE)),
            _const_spec((N_SSM_BLOCKS, MXU_DIM, 2 * BLOCK_STATES)),
            _const_spec((N_SSM_BLOCKS, 2 * BLOCK_STATES, MXU_DIM)),
            _const_spec((1, SSM_WIDTH)),
            _const_spec((SSM_WIDTH, SSM_WIDTH)),
            _const_spec((1, SSM_WIDTH)),
            _const_spec((1, SSM_WIDTH)),
        ],
        out_specs=[
            pl.BlockSpec((nbv, blk_rows, SSM_WIDTH), lambda i: (0, i, 0)),
            pl.BlockSpec((nb, N_STATE), lambda i: (0, 0)),
            pl.BlockSpec((nb, N_STATE), lambda i: (0, 0)),
        ],
        out_shape=(
            jax.ShapeDtypeStruct((nbv, seq, SSM_WIDTH), BF16),
            jax.ShapeDtypeStruct((nb, N_STATE), F32),
            jax.ShapeDtypeStruct((nb, N_STATE), F32),
        ),
        scratch_shapes=[pltpu.VMEM((tr, cols), F32)],
        compiler_params=_cparams(("arbitrary",)),
        name="ssm_mixer",
    )(u3, h0_re, h0_im, ar8, ai8, wb, wc, d, wglu_b, bglu, gs)


def _out_mlp_kernel(x_ref, a_ref, ga_ref, ms_ref, woa_ref, wos_ref, gm_ref, wup_ref, wdn_ref,
                    y_ref, hn_ref):
    @pl.when(pl.program_id(1) == 0)
    def _():
        a = a_ref[...].astype(F32)
        ra = lax.rsqrt(jnp.mean(a * a, axis=-1, keepdims=True) + EPS)
        ma = (a * ra * ga_ref[...]).astype(BF16)
        h = (x_ref[...]
             + jnp.dot(ma, woa_ref[...], preferred_element_type=F32)
             + jnp.dot(ms_ref[...], wos_ref[...], preferred_element_type=F32))
        y_ref[...] = h
        rh = lax.rsqrt(jnp.mean(h * h, axis=-1, keepdims=True) + EPS)
        hn_ref[...] = (h * rh * gm_ref[...]).astype(BF16)

    t = jnp.dot(hn_ref[...], wup_ref[...], preferred_element_type=F32)
    t = jnp.maximum(t, 0.0)
    t = (t * t).astype(BF16)
    y_ref[...] += jnp.dot(t, wdn_ref[...], preferred_element_type=F32)


def _out_mlp(x2d, attn, ga, mix_s, woa, wos, gm, wup, wdn, tm, tf):
    rows = x2d.shape[0]
    return pl.pallas_call(
        _out_mlp_kernel,
        grid=(rows // tm, D_FF // tf),
        in_specs=[
            pl.BlockSpec((tm, D_MODEL), lambda i, j: (i, 0)),
            pl.BlockSpec((tm, ATTN_WIDTH), lambda i, j: (i, 0)),
            _const_spec((1, ATTN_WIDTH)),
            pl.BlockSpec((tm, SSM_WIDTH), lambda i, j: (i, 0)),
            _const_spec((ATTN_WIDTH, D_MODEL)),
            _const_spec((SSM_WIDTH, D_MODEL)),
            _const_spec((1, D_MODEL)),
            pl.BlockSpec((D_MODEL, tf), lambda i, j: (0, j)),
            pl.BlockSpec((tf, D_MODEL), lambda i, j: (j, 0)),
        ],
        out_specs=pl.BlockSpec((tm, D_MODEL), lambda i, j: (i, 0)),
        out_shape=jax.ShapeDtypeStruct((rows, D_MODEL), F32),
        scratch_shapes=[pltpu.VMEM((tm, D_MODEL), BF16)],
        compiler_params=_cparams(("parallel", "arbitrary")),
        name="out_mlp",
    )(x2d, attn, ga, mix_s, woa, wos, gm, wup, wdn)


def _pair_heads(a, axis):
    shape = a.shape
    split = shape[:axis] + (N_KV_HEADS // 2, 2, Q_PER_KV, HEAD_DIM) + shape[axis + 1:]
    return jnp.swapaxes(a.reshape(split), axis + 1, axis + 2).reshape(shape)


def _block_diag_weights(bb_re, bb_im, c_re, c_im):
    gpb = GROUPS_PER_BLOCK
    eye = jnp.eye(gpb, dtype=F32)

    def b_blocks(bb):
        b = bb.reshape(SSM_GROUP, N_SSM_BLOCKS, gpb, STATE_DIM)
        w = jnp.einsum('hjgp,gk->jghkp', b, eye)
        return w.reshape(N_SSM_BLOCKS, gpb * SSM_GROUP, BLOCK_STATES)

    def c_blocks(c):
        cc = c.reshape(N_SSM_BLOCKS, gpb, SSM_GROUP, STATE_DIM)
        w = jnp.einsum('jghp,gk->jgpkh', cc, eye)
        return w.reshape(N_SSM_BLOCKS, BLOCK_STATES, gpb * SSM_GROUP)

    wb = jnp.concatenate([b_blocks(bb_re), b_blocks(bb_im)], axis=2).astype(BF16)
    wc = jnp.concatenate([c_blocks(c_re), -c_blocks(c_im)], axis=1).astype(BF16)
    return wb, wc


def _layer(x, cache_k, cache_v, h0_re, h0_im, p):
    n, t = x.shape[:2]
    rows = n * t
    tm = 512
    x2d = x.reshape(rows, D_MODEL)
    q, kv, u = _in_proj(x2d, p['gn'], p['wq'], p['wr'], p['gq'], p['gk'], tm)
    kv3 = kv.reshape(n, t, 2 * KV_WIDTH)

    if cache_k is None:
        attn = _prompt_attention(p['sinks'], q, kv, n, t)
        tail = kv3[:, t - WINDOW:]
        new_k = tail[..., :KV_WIDTH].reshape(n, WINDOW, N_KV_HEADS, HEAD_DIM)
        new_v = tail[..., KV_WIDTH:].reshape(n, WINDOW, N_KV_HEADS, HEAD_DIM)
        u3 = u.reshape(n, t, SSM_WIDTH)
        ssm_steps, ssm_subs = SSM_SUB_TILE_ROWS // n, SSM_SUB_TILES
    else:
        nblk = ATTN_WIDTH // LANES
        q3 = q.reshape(n, t, nblk, LANES).transpose(0, 2, 1, 3).reshape(n, nblk * t, LANES)
        kvn = jnp.pad(kv3, ((0, 0), (SUBLANES - t, 0), (0, 0)))
        ckt = cache_k.transpose(0, 2, 3, 1)
        cvt = cache_v.transpose(0, 2, 3, 1)
        a3, nkt, nvt = _sample_attention(p['sinks'], q3, ckt, cvt, kvn, t)
        attn = a3.reshape(n, nblk, t, LANES).transpose(0, 2, 1, 3).reshape(rows, ATTN_WIDTH)
        new_k = nkt.transpose(0, 3, 1, 2)
        new_v = nvt.transpose(0, 3, 1, 2)
        u3 = u.reshape(1, rows, SSM_WIDTH)
        ssm_steps, ssm_subs = t, 1

    mix3, h_re, h_im = _ssm_mixer(u3, h0_re.reshape(n, N_STATE), h0_im.reshape(n, N_STATE), p['ar8'], p['ai8'],
                                  p['wb'], p['wc'], p['d'], p['w_glu'], p['b_glu'], p['gs'], n, ssm_steps, ssm_subs)
    mix_s = mix3.reshape(rows, SSM_WIDTH)
    h_re = h_re.reshape(n, N_SSM_GROUPS, STATE_DIM)
    h_im = h_im.reshape(n, N_SSM_GROUPS, STATE_DIM)

    y = _out_mlp(x2d, attn, p['ga'], mix_s, p['woa'], p['wos'], p['gm'], p['w_up'], p['w_down'], tm, MLP_FF_TILE)
    return y.reshape(n, t, D_MODEL), new_k, new_v, h_re, h_im


def _prepare_params(l, attn_norm_g, w_in, q_norm_g, k_norm_g, attn_sinks,
                    ssm_A_re, ssm_A_im, ssm_log_dt, ssm_B_re, ssm_B_im, ssm_C_re, ssm_C_im, ssm_D,
                    w_glu, b_glu, attn_out_g, ssm_out_g, w_out, mlp_norm_g, w_mlp_up, w_mlp_down):
    w_in_l = w_in[l]
    heads_per_blk = MXU_DIM // HEAD_DIM

    a_re = ssm_A_re[l].reshape(1, N_STATE)
    a_im = ssm_A_im[l].reshape(1, N_STATE)
    ldt = jnp.broadcast_to(ssm_log_dt[l][:, None], (N_SSM_GROUPS, STATE_DIM)).reshape(1, N_STATE)
    b_re = ssm_B_re[l].transpose(2, 0, 1).reshape(SSM_GROUP, N_STATE)
    b_im = ssm_B_im[l].transpose(2, 0, 1).reshape(SSM_GROUP, N_STATE)
    ab_re, ab_im, bb_re, bb_im = _ssm_discretize(a_re, a_im, ldt, b_re, b_im)
    wb, wc = _block_diag_weights(bb_re, bb_im, ssm_C_re[l], ssm_C_im[l])

    w_out_l = w_out[l]
    return dict(
        gn=attn_norm_g[l].reshape(1, D_MODEL),
        wq=_pair_heads(w_in_l[:, :ATTN_WIDTH], 1).astype(BF16),
        wr=w_in_l[:, ATTN_WIDTH:].astype(BF16),
        gq=jnp.tile(q_norm_g[l], heads_per_blk).reshape(1, MXU_DIM),
        gk=jnp.tile(k_norm_g[l], heads_per_blk).reshape(1, MXU_DIM),
        sinks=attn_sinks[l].astype(F32),
        ar8=jnp.broadcast_to(ab_re, (SUBLANES, N_STATE)),
        ai8=jnp.broadcast_to(ab_im, (SUBLANES, N_STATE)),
        wb=wb, wc=wc,
        d=ssm_D[l].reshape(1, SSM_WIDTH),
        w_glu=w_glu[l].astype(BF16),
        b_glu=b_glu[l].reshape(1, SSM_WIDTH),
        gs=ssm_out_g[l].reshape(1, SSM_WIDTH),
        ga=_pair_heads(attn_out_g[l], 0).reshape(1, ATTN_WIDTH),
        woa=_pair_heads(w_out_l[:ATTN_WIDTH], 0).astype(BF16),
        wos=w_out_l[ATTN_WIDTH:].astype(BF16),
        gm=mlp_norm_g[l].reshape(1, D_MODEL),
        w_up=w_mlp_up[l].astype(BF16),
        w_down=w_mlp_down[l].astype(BF16),
    )


def kernel(x_prompt, x_sample, cache_k, cache_v, state_ssm_re, state_ssm_im, attn_norm_g, w_in, q_norm_g, k_norm_g, attn_sinks, ssm_A_re, ssm_A_im, ssm_log_dt, ssm_B_re, ssm_B_im, ssm_C_re, ssm_C_im, ssm_D, w_glu, b_glu, attn_out_g, ssm_out_g, w_out, mlp_norm_g, w_mlp_up, w_mlp_down):
    depth = w_in.shape[0]
    xp, xs = x_prompt, x_sample
    zeros_state = jnp.zeros((x_prompt.shape[0], N_SSM_GROUPS, STATE_DIM), F32)
    outs = [[] for _ in range(8)]
    for l in range(depth):
        p = _prepare_params(l, attn_norm_g, w_in, q_norm_g, k_norm_g, attn_sinks,
                            ssm_A_re, ssm_A_im, ssm_log_dt, ssm_B_re, ssm_B_im, ssm_C_re, ssm_C_im, ssm_D,
                            w_glu, b_glu, attn_out_g, ssm_out_g, w_out, mlp_norm_g, w_mlp_up, w_mlp_down)
        xp, kp, vp, hrp, hip = _layer(xp, None, None, zeros_state, zeros_state, p)
        xs, ks, vs, hrs, his = _layer(xs, cache_k[l], cache_v[l], state_ssm_re[l], state_ssm_im[l], p)
        for lst, val in zip(outs, (kp, vp, hrp, hip, ks, vs, hrs, his)):
            lst.append(val)
    return (xp, xs) + tuple(jnp.stack(o) for o in outs)
```

```python
import functools
import math

import jax
import jax.numpy as jnp
from jax import lax
from jax.experimental import pallas as pl
from jax.experimental.pallas import tpu as pltpu

D_MODEL = 2048
ATTN_WIDTH = 1024
SSM_WIDTH = 1024
HEAD_DIM = 64
N_HEADS = 16
N_KV_HEADS = 4
Q_PER_KV = 4
KV_WIDTH = 256
WINDOW = 128
SSM_GROUP = 16
N_SSM_GROUPS = 64
STATE_DIM = 64
N_STATE = N_SSM_GROUPS * STATE_DIM
D_FF = 8192
PROJ_WIDTH = ATTN_WIDTH + 2 * KV_WIDTH + SSM_WIDTH
EPS = 1e-6
NEG_INF = -1e30

LANES = 128
SUBLANES = 8
MXU_DIM = 256
VMEM_LIMIT = 56 * 1024 * 1024

GROUPS_PER_BLOCK = MXU_DIM // SSM_GROUP
N_SSM_BLOCKS = N_SSM_GROUPS // GROUPS_PER_BLOCK
BLOCK_STATES = GROUPS_PER_BLOCK * STATE_DIM
SCAN_LANES = 512
SSM_SUB_TILE_ROWS = 256
SSM_SUB_TILES = 2
MLP_FF_TILE = 1024

F32 = jnp.float32
BF16 = jnp.bfloat16


def _cparams(sem):
    return pltpu.CompilerParams(dimension_semantics=sem, vmem_limit_bytes=VMEM_LIMIT)


def _const_spec(shape):
    nd = len(shape)
    return pl.BlockSpec(shape, lambda *_: (0,) * nd, pipeline_mode=pl.Buffered(1))


def _discretize_kernel(are_ref, aim_ref, ldt_ref, bre_ref, bim_ref,
                       abr_ref, abi_ref, bbr_ref, bbi_ref):
    a_re = are_ref[...]
    a_im = aim_ref[...]
    dt = jnp.exp(ldt_ref[...])
    mag = jnp.exp(a_re * dt)
    ab_re = mag * jnp.cos(a_im * dt)
    ab_im = mag * jnp.sin(a_im * dt)
    abr_ref[...] = ab_re
    abi_ref[...] = ab_im
    x = ab_re - 1.0
    y = ab_im
    den = a_re * a_re + a_im * a_im
    c_re = (x * a_re + y * a_im) / den
    c_im = (y * a_re - x * a_im) / den
    b_re = bre_ref[...]
    b_im = bim_ref[...]
    bbr_ref[...] = c_re * b_re - c_im * b_im
    bbi_ref[...] = c_re * b_im + c_im * b_re


def _ssm_discretize(a_re, a_im, log_dt, b_re, b_im):
    row = jax.ShapeDtypeStruct((1, N_STATE), F32)
    mat = jax.ShapeDtypeStruct((SSM_GROUP, N_STATE), F32)
    return pl.pallas_call(
        _discretize_kernel,
        out_shape=(row, row, mat, mat),
        name="ssm_discretize",
    )(a_re, a_im, log_dt, b_re, b_im)


def _head_rmsnorm(zc, gain, ones_blk):
    sq = zc * zc
    hi = sq.astype(BF16)
    lo = (sq - hi.astype(F32)).astype(BF16)
    ss = (jnp.dot(hi, ones_blk, preferred_element_type=F32)
          + jnp.dot(lo, ones_blk, preferred_element_type=F32))
    return zc * lax.rsqrt(ss * (1.0 / HEAD_DIM) + EPS) * gain


def _in_proj_kernel(x_ref, gn_ref, wq_ref, wr_ref, gq_ref, gk_ref, *rest):
    n_cast = (len(rest) - 3) // 2
    cast_in = rest[:n_cast]
    q_ref, kv_ref, u_ref = rest[n_cast:n_cast + 3]
    cast_out = rest[n_cast + 3:]
    for src, dst in zip(cast_in, cast_out):
        dst[...] = src[...].astype(BF16)
    x = x_ref[...]
    r = lax.rsqrt(jnp.mean(x * x, axis=-1, keepdims=True) + EPS)
    xn = (x * r * gn_ref[...]).astype(BF16)
    zq = jnp.dot(xn, wq_ref[...], preferred_element_type=F32)
    zr = jnp.dot(xn, wr_ref[...], preferred_element_type=F32)
    ri = lax.broadcasted_iota(jnp.int32, (MXU_DIM, MXU_DIM), 0) // HEAD_DIM
    ci = lax.broadcasted_iota(jnp.int32, (MXU_DIM, MXU_DIM), 1) // HEAD_DIM
    ones_blk = (ri == ci).astype(BF16)
    gq = gq_ref[...]
    for c in range(ATTN_WIDTH // MXU_DIM):
        sl = slice(c * MXU_DIM, (c + 1) * MXU_DIM)
        q_ref[:, sl] = (_head_rmsnorm(zq[:, sl], gq, ones_blk) * (HEAD_DIM ** -0.5)).astype(BF16)
    kv_ref[:, :KV_WIDTH] = _head_rmsnorm(zr[:, :KV_WIDTH], gk_ref[...], ones_blk)
    kv_ref[:, KV_WIDTH:] = zr[:, KV_WIDTH:2 * KV_WIDTH]
    u_ref[...] = zr[:, 2 * KV_WIDTH:]


def _in_proj(x2d, gn, wq, wr, gq, gk, tm, to_bf16=()):
    rows = x2d.shape[0]
    steps = rows // tm
    slab_specs = [pl.BlockSpec((w.shape[0] // steps, w.shape[1]), lambda i: (i, 0)) for w in to_bf16]
    assert all(w.shape[0] % (steps * 2 * SUBLANES) == 0 for w in to_bf16)
    return pl.pallas_call(
        _in_proj_kernel,
        grid=(steps,),
        in_specs=[
            pl.BlockSpec((tm, D_MODEL), lambda i: (i, 0)),
            _const_spec((1, D_MODEL)),
            _const_spec((D_MODEL, ATTN_WIDTH)),
            _const_spec((D_MODEL, PROJ_WIDTH - ATTN_WIDTH)),
            _const_spec((1, MXU_DIM)),
            _const_spec((1, MXU_DIM)),
        ] + slab_specs,
        out_specs=[
            pl.BlockSpec((tm, ATTN_WIDTH), lambda i: (i, 0)),
            pl.BlockSpec((tm, 2 * KV_WIDTH), lambda i: (i, 0)),
            pl.BlockSpec((tm, SSM_WIDTH), lambda i: (i, 0)),
        ] + slab_specs,
        out_shape=(
            jax.ShapeDtypeStruct((rows, ATTN_WIDTH), BF16),
            jax.ShapeDtypeStruct((rows, 2 * KV_WIDTH), F32),
            jax.ShapeDtypeStruct((rows, SSM_WIDTH), F32),
        ) + tuple(jax.ShapeDtypeStruct(w.shape, BF16) for w in to_bf16),
        compiler_params=_cparams(("arbitrary",)),
        name="in_proj",
    )(x2d, gn, wq, wr, gq, gk, *to_bf16)


def _alibi_slope(head):
    return 2.0 ** (-8.0 * (head + 1) / N_HEADS)


def _prompt_attn_kernel(q_ref, kvp_ref, kvc_ref, bias_ref, sink_ref, o_ref):
    has_prev = (pl.program_id(1) > 0).astype(jnp.int32)
    tq = WINDOW
    lane = lax.broadcasted_iota(jnp.int32, (tq, LANES), 1)
    low = lane < HEAD_DIM
    zero = jnp.zeros((tq, LANES), BF16)
    pairs = range(N_KV_HEADS // 2)

    def scores_t(pair):
        ks = slice(pair * LANES, (pair + 1) * LANES)
        kc = jnp.concatenate([kvp_ref[:, ks], kvc_ref[:, ks]], axis=0).astype(BF16)
        qb = [q_ref[:, (pair * Q_PER_KV + r) * LANES:(pair * Q_PER_KV + r + 1) * LANES] for r in range(Q_PER_KV)]
        qs = jnp.concatenate([jnp.where(low, b, zero) for b in qb] + [jnp.where(low, zero, b) for b in qb], axis=0)
        return lax.dot_general(kc, qs, (((1,), (1,)), ((), ())), preferred_element_type=F32)

    def weighted_values_t(pair, st):
        vs = slice(KV_WIDTH + pair * LANES, KV_WIDTH + (pair + 1) * LANES)
        vct = jnp.concatenate([kvp_ref[:, vs], kvc_ref[:, vs]], axis=0).T.astype(BF16)
        s = st + bias_ref[has_prev, pair]
        sink = sink_ref[pair]
        m = jnp.maximum(jnp.max(s, axis=0, keepdims=True), sink)
        p = jnp.exp(s - m)
        den = jnp.sum(p, axis=0, keepdims=True) + jnp.exp(sink - m)
        return jnp.dot(vct, p.astype(BF16), preferred_element_type=F32) * (1.0 / den)

    def store(pair, ot):
        for r in range(Q_PER_KV):
            both = jnp.concatenate([ot[:HEAD_DIM, r * tq:(r + 1) * tq],
                                    ot[HEAD_DIM:, (Q_PER_KV + r) * tq:(Q_PER_KV + r + 1) * tq]], axis=0)
            o_ref[:, (pair * Q_PER_KV + r) * LANES:(pair * Q_PER_KV + r + 1) * LANES] = both.T.astype(BF16)

    sts = [scores_t(pair) for pair in pairs]
    ots = [weighted_values_t(pair, sts[pair]) for pair in pairs]
    for pair in pairs:
        store(pair, ots[pair])


def _prompt_attn_tables(sinks):
    w = WINDOW
    heads = jnp.arange(N_HEADS, dtype=F32).reshape(N_KV_HEADS // 2, 2 * Q_PER_KV)
    slope = jnp.exp2(-8.0 * (heads + 1.0) / N_HEADS)
    i = jnp.arange(w)[None, :]
    j = jnp.arange(2 * w)[:, None]
    d = i + w - j
    ok = (d >= 0) & (d <= w)
    ok = jnp.stack([ok & (j >= w), ok])
    bias = jnp.where(ok[:, None, :, None, :], -slope[None, :, None, :, None] * d.astype(F32)[None, None, :, None, :],
                     NEG_INF)
    bias = bias.reshape(2, N_KV_HEADS // 2, 2 * w, 2 * Q_PER_KV * w)
    sink_rows = jnp.repeat(sinks.reshape(N_KV_HEADS // 2, 1, 2 * Q_PER_KV), w, axis=2)
    return bias, sink_rows


def _prompt_attention(sinks, q, kv, batch, seq):
    nb = seq // WINDOW
    bias, sink_rows = _prompt_attn_tables(sinks)
    return pl.pallas_call(
        _prompt_attn_kernel,
        grid=(batch, nb),
        in_specs=[
            pl.BlockSpec((WINDOW, ATTN_WIDTH), lambda b, n: (b * nb + n, 0)),
            pl.BlockSpec((WINDOW, 2 * KV_WIDTH), lambda b, n: (b * nb + jnp.maximum(n - 1, 0), 0)),
            pl.BlockSpec((WINDOW, 2 * KV_WIDTH), lambda b, n: (b * nb + n, 0)),
            _const_spec(bias.shape),
            _const_spec(sink_rows.shape),
        ],
        out_specs=pl.BlockSpec((WINDOW, ATTN_WIDTH), lambda b, n: (b * nb + n, 0)),
        out_shape=jax.ShapeDtypeStruct((batch * seq, ATTN_WIDTH), BF16),
        compiler_params=_cparams(("parallel", "parallel")),
        name="prompt_attention",
    )(q, kv, kv, bias, sink_rows)


SAMPLE_ATTN_BATCH = 16
SAMPLE_ATTN_UNROLL = 2


def _sample_attn_kernel(sink_ref, q_ref, ckt_ref, cvt_ref, kvn_ref, o_ref, nkt_ref, nvt_ref, *, t_new):
    rows = Q_PER_KV * t_new
    tk = 2 * WINDOW
    i = lax.broadcasted_iota(jnp.int32, (rows, tk), 0) % t_new
    c = lax.broadcasted_iota(jnp.int32, (rows, tk), 1)
    is_new = c >= tk - t_new
    j = jnp.where(c < WINDOW, c, c - (WINDOW - t_new))
    d = i + WINDOW - j
    valid = (d >= 0) & (d <= WINDOW) & ((c < WINDOW) | is_new)
    delta = d.astype(F32)
    lane_q = lax.broadcasted_iota(jnp.int32, (rows, LANES), 1)
    low = lane_q < HEAD_DIM
    zero_q = jnp.zeros((rows, LANES), BF16)
    lane_w = lax.broadcasted_iota(jnp.int32, (WINDOW, LANES), 1)
    tail = lane_w >= WINDOW - t_new
    rid = lax.broadcasted_iota(jnp.int32, (rows, 1), 0) // t_new
    alibi, sinks = [], []
    for pair in range(N_KV_HEADS // 2):
        sl, sk = [], []
        for half in range(2):
            slope = jnp.zeros((rows, 1), F32)
            sink = jnp.zeros((rows, 1), F32)
            for r in range(Q_PER_KV):
                head = (2 * pair + half) * Q_PER_KV + r
                slope = jnp.where(rid == r, _alibi_slope(head), slope)
                sink = jnp.where(rid == r, sink_ref[head], sink)
            sl.append(slope)
            sk.append(sink)
        alibi.append(jnp.concatenate(sl, axis=0) * jnp.concatenate([delta, delta], axis=0))
        sinks.append(jnp.concatenate(sk, axis=0))
    valid2 = jnp.concatenate([valid, valid], axis=0)

    lead_zero = jnp.zeros((WINDOW - SUBLANES, LANES), F32)

    def new_rows_transposed(e, lanes):
        return jnp.concatenate([lead_zero, kvn_ref[e, :, lanes]], axis=0).T

    def shifted(old_t, new_t):
        return jnp.where(tail, new_t, pltpu.roll(old_t, WINDOW - t_new, axis=1))

    def body(e, carry):
        for pair in range(N_KV_HEADS // 2):
            heads = slice(2 * pair, 2 * pair + 2)
            kt = ckt_ref[e, heads].reshape(2 * HEAD_DIM, WINDOW)
            vt = cvt_ref[e, heads].reshape(2 * HEAD_DIM, WINDOW)
            knt = new_rows_transposed(e, slice(pair * LANES, (pair + 1) * LANES))
            vnt = new_rows_transposed(e, slice(KV_WIDTH + pair * LANES, KV_WIDTH + (pair + 1) * LANES))
            nkt_ref[e, heads] = shifted(kt, knt).reshape(2, HEAD_DIM, WINDOW)
            nvt_ref[e, heads] = shifted(vt, vnt).reshape(2, HEAD_DIM, WINDOW)

            qp = q_ref[e, pair * rows:(pair + 1) * rows, :]
            qs = jnp.concatenate([jnp.where(low, qp, zero_q), jnp.where(low, zero_q, qp)], axis=0)
            kt_all = jnp.concatenate([kt, knt], axis=1).astype(BF16)
            vt_all = jnp.concatenate([vt, vnt], axis=1).astype(BF16)
            scores = jnp.dot(qs, kt_all, preferred_element_type=F32)
            s = jnp.where(valid2, scores - alibi[pair], NEG_INF)
            m = jnp.maximum(jnp.max(s, axis=-1, keepdims=True), sinks[pair])
            p = jnp.exp(s - m)
            den = jnp.sum(p, axis=-1, keepdims=True) + jnp.exp(sinks[pair] - m)
            pv = lax.dot_general(p.astype(BF16), vt_all, (((1,), (1,)), ((), ())),
                                 preferred_element_type=F32)
            pv = pv * (1.0 / den)
            o_ref[e, pair * rows:(pair + 1) * rows, :] = jnp.where(low, pv[:rows], pv[rows:]).astype(BF16)
        return carry

    lax.fori_loop(0, q_ref.shape[0], body, 0, unroll=SAMPLE_ATTN_UNROLL)


def _sample_attention(sinks, q3, ckt, cvt, kvn, t_new):
    n = q3.shape[0]
    bn = SAMPLE_ATTN_BATCH
    qrows = q3.shape[1]
    cache_spec = pl.BlockSpec((bn, N_KV_HEADS, HEAD_DIM, WINDOW), lambda b: (b, 0, 0, 0))
    cache_shape = jax.ShapeDtypeStruct((n, N_KV_HEADS, HEAD_DIM, WINDOW), F32)
    return pl.pallas_call(
        functools.partial(_sample_attn_kernel, t_new=t_new),
        grid=(n // bn,),
        in_specs=[
            pl.BlockSpec(memory_space=pltpu.SMEM),
            pl.BlockSpec((bn, qrows, LANES), lambda b: (b, 0, 0)),
            cache_spec,
            cache_spec,
            pl.BlockSpec((bn, SUBLANES, 2 * KV_WIDTH), lambda b: (b, 0, 0)),
        ],
        out_specs=[pl.BlockSpec((bn, qrows, LANES), lambda b: (b, 0, 0)), cache_spec, cache_spec],
        out_shape=(jax.ShapeDtypeStruct((n, qrows, LANES), BF16), cache_shape, cache_shape),
        compiler_params=_cparams(("parallel",)),
        name="sample_attention",
    )(sinks, q3, ckt, cvt, kvn)


def _gelu_tanh(x):
    c = math.sqrt(2.0 / math.pi)
    return 0.5 * x * (1.0 + jnp.tanh(c * (x + 0.044715 * (x * x * x))))


def _ssm_kernel(u_ref, h0re_ref, h0im_ref, ar_ref, ai_ref, wb_ref, wc_ref, d_ref, wglu_ref, bglu_ref, gs_ref,
                mix_ref, hre_ref, him_ref, s_ref, *, nb, steps, n_sub):
    @pl.when(pl.program_id(0) == 0)
    def _():
        hre_ref[...] = h0re_ref[...]
        him_ref[...] = h0im_ref[...]

    for sub in range(n_sub):
        _ssm_sub_tile(u_ref, ar_ref, ai_ref, wb_ref, wc_ref, d_ref, wglu_ref, bglu_ref, gs_ref,
                      mix_ref, hre_ref, him_ref, s_ref, nb=nb, steps=steps, sub=sub)


def _ssm_sub_tile(u_ref, ar_ref, ai_ref, wb_ref, wc_ref, d_ref, wglu_ref, bglu_ref, gs_ref,
                  mix_ref, hre_ref, him_ref, s_ref, *, nb, steps, sub):
    rows = nb * steps
    row0 = sub * rows
    pos = slice(sub * (rows // u_ref.shape[0]), (sub + 1) * (rows // u_ref.shape[0]))
    tm_row = lax.broadcasted_iota(jnp.int32, (rows, rows), 0)
    bm_col = lax.broadcasted_iota(jnp.int32, (rows, rows), 1)
    to_time_major = (bm_col == (tm_row % nb) * steps + tm_row // nb).astype(BF16)
    bm_row = lax.broadcasted_iota(jnp.int32, (rows, rows), 0)
    tm_col = lax.broadcasted_iota(jnp.int32, (rows, rows), 1)
    to_batch_major = (bm_row == (tm_col % nb) * steps + tm_col // nb).astype(BF16)

    u_bm = u_ref[:, pos, :].reshape(rows, SSM_WIDTH)
    u_hi = u_bm.astype(BF16)
    u_lo = (u_bm - u_hi.astype(F32)).astype(BF16)
    u_hi_tm = jnp.dot(to_time_major, u_hi, preferred_element_type=F32)
    u = u_hi_tm + jnp.dot(to_time_major, u_lo, preferred_element_type=F32)
    ub = u_hi_tm.astype(BF16)
    blk_cols = 2 * BLOCK_STATES
    ys = []
    for j in range(N_SSM_BLOCKS):
        s_ref[row0:row0 + rows, j * blk_cols:(j + 1) * blk_cols] = jnp.dot(
            ub[:, j * MXU_DIM:(j + 1) * MXU_DIM], wb_ref[j], preferred_element_type=F32)
        for part in range(BLOCK_STATES // SCAN_LANES):
            rc = j * blk_cols + part * SCAN_LANES
            ic = rc + BLOCK_STATES
            sc = j * BLOCK_STATES + part * SCAN_LANES
            a_re = ar_ref[:, sc:sc + SCAN_LANES]
            a_im = ai_ref[:, sc:sc + SCAN_LANES]
            for bg in range(nb // SUBLANES):
                b0 = bg * SUBLANES
                h_re = hre_ref[b0:b0 + SUBLANES, sc:sc + SCAN_LANES]
                h_im = him_ref[b0:b0 + SUBLANES, sc:sc + SCAN_LANES]
                for t in range(steps):
                    row = row0 + t * nb + b0
                    n_re = a_re * h_re - a_im * h_im + s_ref[row:row + SUBLANES, rc:rc + SCAN_LANES]
                    n_im = a_re * h_im + a_im * h_re + s_ref[row:row + SUBLANES, ic:ic + SCAN_LANES]
                    s_ref[row:row + SUBLANES, rc:rc + SCAN_LANES] = n_re
                    s_ref[row:row + SUBLANES, ic:ic + SCAN_LANES] = n_im
                    h_re, h_im = n_re, n_im
                hre_ref[b0:b0 + SUBLANES, sc:sc + SCAN_LANES] = h_re
                him_ref[b0:b0 + SUBLANES, sc:sc + SCAN_LANES] = h_im
        hb = s_ref[row0:row0 + rows, j * blk_cols:(j + 1) * blk_cols].astype(BF16)
        ys.append(jnp.dot(hb, wc_ref[j], preferred_element_type=F32))

    y = jnp.concatenate(ys, axis=1) + d_ref[...] * u
    g = _gelu_tanh(y)
    gate = jnp.dot(g.astype(BF16), wglu_ref[...], preferred_element_type=F32) + bglu_ref[...]
    so = g * jax.nn.sigmoid(gate)
    r = lax.rsqrt(jnp.mean(so * so, axis=-1, keepdims=True) + EPS)
    mix_tm = (so * r * gs_ref[...]).astype(BF16)
    mix_bm = jnp.dot(to_batch_major, mix_tm, preferred_element_type=F32).astype(BF16)
    mix_ref[:, pos, :] = mix_bm.reshape(mix_ref.shape[0], rows // mix_ref.shape[0], SSM_WIDTH)


def _ssm_mixer(u3, h0_re, h0_im, ar8, ai8, wb, wc, d, wglu_b, bglu, gs, nb, steps, n_sub):
    nbv, seq, _ = u3.shape
    blk_rows = n_sub * nb * steps // nbv
    assert blk_rows % SUBLANES == 0 and seq % blk_rows == 0
    cols = 2 * N_STATE
    tr = n_sub * nb * steps
    return pl.pallas_call(
        functools.partial(_ssm_kernel, nb=nb, steps=steps, n_sub=n_sub),
        grid=(seq // blk_rows,),
        in_specs=[
            pl.BlockSpec((nbv, blk_rows, SSM_WIDTH), lambda i: (0, i, 0)),
            _const_spec((nb, N_STATE)),
            _const_spec((nb, N_STATE)),
            _const_spec((SUBLANES, N_STATE)),
            _const_spec((SUBLANES, N_STATE)),
            _const_spec((N_SSM_BLOCKS, MXU_DIM, 2 * BLOCK_STATES)),
            _const_spec((N_SSM_BLOCKS, 2 * BLOCK_STATES, MXU_DIM)),
            _const_spec((1, SSM_WIDTH)),
            _const_spec((SSM_WIDTH, SSM_WIDTH)),
            _const_spec((1, SSM_WIDTH)),
            _const_spec((1, SSM_WIDTH)),
        ],
        out_specs=[
            pl.BlockSpec((nbv, blk_rows, SSM_WIDTH), lambda i: (0, i, 0)),
            pl.BlockSpec((nb, N_STATE), lambda i: (0, 0)),
            pl.BlockSpec((nb, N_STATE), lambda i: (0, 0)),
        ],
        out_shape=(
            jax.ShapeDtypeStruct((nbv, seq, SSM_WIDTH), BF16),
            jax.ShapeDtypeStruct((nb, N_STATE), F32),
            jax.ShapeDtypeStruct((nb, N_STATE), F32),
        ),
        scratch_shapes=[pltpu.VMEM((tr, cols), F32)],
        compiler_params=_cparams(("arbitrary",)),
        name="ssm_mixer",
    )(u3, h0_re, h0_im, ar8, ai8, wb, wc, d, wglu_b, bglu, gs)


def _out_mlp_kernel(x_ref, a_ref, ga_ref, ms_ref, woa_ref, wos_ref, gm_ref, wup_ref, wdn_ref,
                    y_ref, hn_ref):
    @pl.when(pl.program_id(1) == 0)
    def _():
        a = a_ref[...].astype(F32)
        ra = lax.rsqrt(jnp.mean(a * a, axis=-1, keepdims=True) + EPS)
        ma = (a * ra * ga_ref[...]).astype(BF16)
        h = (x_ref[...]
             + jnp.dot(ma, woa_ref[...], preferred_element_type=F32)
             + jnp.dot(ms_ref[...], wos_ref[...], preferred_element_type=F32))
        y_ref[...] = h
        rh = lax.rsqrt(jnp.mean(h * h, axis=-1, keepdims=True) + EPS)
        hn_ref[...] = (h * rh * gm_ref[...]).astype(BF16)

    t = jnp.dot(hn_ref[...], wup_ref[...], preferred_element_type=F32)
    t = jnp.maximum(t, 0.0)
    t = (t * t).astype(BF16)
    y_ref[...] += jnp.dot(t, wdn_ref[...], preferred_element_type=F32)


def _out_mlp(x2d, attn, ga, mix_s, woa, wos, gm, wup, wdn, tm, tf):
    rows = x2d.shape[0]
    return pl.pallas_call(
        _out_mlp_kernel,
        grid=(rows // tm, D_FF // tf),
        in_specs=[
            pl.BlockSpec((tm, D_MODEL), lambda i, j: (i, 0)),
            pl.BlockSpec((tm, ATTN_WIDTH), lambda i, j: (i, 0)),
            _const_spec((1, ATTN_WIDTH)),
            pl.BlockSpec((tm, SSM_WIDTH), lambda i, j: (i, 0)),
            _const_spec((ATTN_WIDTH, D_MODEL)),
            _const_spec((SSM_WIDTH, D_MODEL)),
            _const_spec((1, D_MODEL)),
            pl.BlockSpec((D_MODEL, tf), lambda i, j: (0, j)),
            pl.BlockSpec((tf, D_MODEL), lambda i, j: (j, 0)),
        ],
        out_specs=pl.BlockSpec((tm, D_MODEL), lambda i, j: (i, 0)),
        out_shape=jax.ShapeDtypeStruct((rows, D_MODEL), F32),
        scratch_shapes=[pltpu.VMEM((tm, D_MODEL), BF16)],
        compiler_params=_cparams(("parallel", "arbitrary")),
        name="out_mlp",
    )(x2d, attn, ga, mix_s, woa, wos, gm, wup, wdn)


def _pair_heads(a, axis):
    shape = a.shape
    split = shape[:axis] + (N_KV_HEADS // 2, 2, Q_PER_KV, HEAD_DIM) + shape[axis + 1:]
    return jnp.swapaxes(a.reshape(split), axis + 1, axis + 2).reshape(shape)


def _block_diag_weights(bb_re, bb_im, c_re, c_im):
    gpb = GROUPS_PER_BLOCK
    eye = jnp.eye(gpb, dtype=F32)

    def b_blocks(bb):
        b = bb.reshape(SSM_GROUP, N_SSM_BLOCKS, gpb, STATE_DIM)
        w = jnp.einsum('hjgp,gk->jghkp', b, eye)
        return w.reshape(N_SSM_BLOCKS, gpb * SSM_GROUP, BLOCK_STATES)

    def c_blocks(c):
        cc = c.reshape(N_SSM_BLOCKS, gpb, SSM_GROUP, STATE_DIM)
        w = jnp.einsum('jghp,gk->jgpkh', cc, eye)
        return w.reshape(N_SSM_BLOCKS, BLOCK_STATES, gpb * SSM_GROUP)

    wb = jnp.concatenate([b_blocks(bb_re), b_blocks(bb_im)], axis=2).astype(BF16)
    wc = jnp.concatenate([c_blocks(c_re), -c_blocks(c_im)], axis=1).astype(BF16)
    return wb, wc


def _layer(x, cache_k, cache_v, h0_re, h0_im, p, mlp_w):
    n, t = x.shape[:2]
    rows = n * t
    tm = 512
    x2d = x.reshape(rows, D_MODEL)
    if mlp_w is None:
        q, kv, u, *mlp_w = _in_proj(x2d, p['gn'], p['wq'], p['wr'], p['gq'], p['gk'], tm,
                                    to_bf16=(p['w_up_f32'], p['w_down_f32']))
    else:
        q, kv, u = _in_proj(x2d, p['gn'], p['wq'], p['wr'], p['gq'], p['gk'], tm)
    kv3 = kv.reshape(n, t, 2 * KV_WIDTH)

    if cache_k is None:
        attn = _prompt_attention(p['sinks'], q, kv, n, t)
        tail = kv3[:, t - WINDOW:]
        new_k = tail[..., :KV_WIDTH].reshape(n, WINDOW, N_KV_HEADS, HEAD_DIM)
        new_v = tail[..., KV_WIDTH:].reshape(n, WINDOW, N_KV_HEADS, HEAD_DIM)
        u3 = u.reshape(n, t, SSM_WIDTH)
        ssm_steps, ssm_subs = SSM_SUB_TILE_ROWS // n, SSM_SUB_TILES
    else:
        nblk = ATTN_WIDTH // LANES
        q3 = q.reshape(n, t, nblk, LANES).transpose(0, 2, 1, 3).reshape(n, nblk * t, LANES)
        kvn = jnp.pad(kv3, ((0, 0), (SUBLANES - t, 0), (0, 0)))
        ckt = cache_k.transpose(0, 2, 3, 1)
        cvt = cache_v.transpose(0, 2, 3, 1)
        a3, nkt, nvt = _sample_attention(p['sinks'], q3, ckt, cvt, kvn, t)
        attn = a3.reshape(n, nblk, t, LANES).transpose(0, 2, 1, 3).reshape(rows, ATTN_WIDTH)
        new_k = nkt.transpose(0, 3, 1, 2)
        new_v = nvt.transpose(0, 3, 1, 2)
        u3 = u.reshape(1, rows, SSM_WIDTH)
        ssm_steps, ssm_subs = t, 1

    mix3, h_re, h_im = _ssm_mixer(u3, h0_re.reshape(n, N_STATE), h0_im.reshape(n, N_STATE), p['ar8'], p['ai8'],
                                  p['wb'], p['wc'], p['d'], p['w_glu'], p['b_glu'], p['gs'], n, ssm_steps, ssm_subs)
    mix_s = mix3.reshape(rows, SSM_WIDTH)
    h_re = h_re.reshape(n, N_SSM_GROUPS, STATE_DIM)
    h_im = h_im.reshape(n, N_SSM_GROUPS, STATE_DIM)

    y = _out_mlp(x2d, attn, p['ga'], mix_s, p['woa'], p['wos'], p['gm'], mlp_w[0], mlp_w[1], tm, MLP_FF_TILE)
    return y.reshape(n, t, D_MODEL), new_k, new_v, h_re, h_im, mlp_w


def _prepare_params(l, attn_norm_g, w_in, q_norm_g, k_norm_g, attn_sinks,
                    ssm_A_re, ssm_A_im, ssm_log_dt, ssm_B_re, ssm_B_im, ssm_C_re, ssm_C_im, ssm_D,
                    w_glu, b_glu, attn_out_g, ssm_out_g, w_out, mlp_norm_g, w_mlp_up, w_mlp_down):
    w_in_l = w_in[l]
    heads_per_blk = MXU_DIM // HEAD_DIM

    a_re = ssm_A_re[l].reshape(1, N_STATE)
    a_im = ssm_A_im[l].reshape(1, N_STATE)
    ldt = jnp.broadcast_to(ssm_log_dt[l][:, None], (N_SSM_GROUPS, STATE_DIM)).reshape(1, N_STATE)
    b_re = ssm_B_re[l].transpose(2, 0, 1).reshape(SSM_GROUP, N_STATE)
    b_im = ssm_B_im[l].transpose(2, 0, 1).reshape(SSM_GROUP, N_STATE)
    ab_re, ab_im, bb_re, bb_im = _ssm_discretize(a_re, a_im, ldt, b_re, b_im)
    wb, wc = _block_diag_weights(bb_re, bb_im, ssm_C_re[l], ssm_C_im[l])

    w_out_l = w_out[l]
    return dict(
        gn=attn_norm_g[l].reshape(1, D_MODEL),
        wq=_pair_heads(w_in_l[:, :ATTN_WIDTH], 1).astype(BF16),
        wr=w_in_l[:, ATTN_WIDTH:].astype(BF16),
        gq=jnp.tile(q_norm_g[l], heads_per_blk).reshape(1, MXU_DIM),
        gk=jnp.tile(k_norm_g[l], heads_per_blk).reshape(1, MXU_DIM),
        sinks=attn_sinks[l].astype(F32),
        ar8=jnp.broadcast_to(ab_re, (SUBLANES, N_STATE)),
        ai8=jnp.broadcast_to(ab_im, (SUBLANES, N_STATE)),
        wb=wb, wc=wc,
        d=ssm_D[l].reshape(1, SSM_WIDTH),
        w_glu=w_glu[l].astype(BF16),
        b_glu=b_glu[l].reshape(1, SSM_WIDTH),
        gs=ssm_out_g[l].reshape(1, SSM_WIDTH),
        ga=_pair_heads(attn_out_g[l], 0).reshape(1, ATTN_WIDTH),
        woa=_pair_heads(w_out_l[:ATTN_WIDTH], 0).astype(BF16),
        wos=w_out_l[ATTN_WIDTH:].astype(BF16),
        gm=mlp_norm_g[l].reshape(1, D_MODEL),
        w_up_f32=w_mlp_up[l],
        w_down_f32=w_mlp_down[l],
    )


def kernel(x_prompt, x_sample, cache_k, cache_v, state_ssm_re, state_ssm_im, attn_norm_g, w_in, q_norm_g, k_norm_g, attn_sinks, ssm_A_re, ssm_A_im, ssm_log_dt, ssm_B_re, ssm_B_im, ssm_C_re, ssm_C_im, ssm_D, w_glu, b_glu, attn_out_g, ssm_out_g, w_out, mlp_norm_g, w_mlp_up, w_mlp_down):
    depth = w_in.shape[0]
    xp, xs = x_prompt, x_sample
    zeros_state = jnp.zeros((x_prompt.shape[0], N_SSM_GROUPS, STATE_DIM), F32)
    outs = [[] for _ in range(8)]
    for l in range(depth):
        p = _prepare_params(l, attn_norm_g, w_in, q_norm_g, k_norm_g, attn_sinks,
                            ssm_A_re, ssm_A_im, ssm_log_dt, ssm_B_re, ssm_B_im, ssm_C_re, ssm_C_im, ssm_D,
                            w_glu, b_glu, attn_out_g, ssm_out_g, w_out, mlp_norm_g, w_mlp_up, w_mlp_down)
        xp, kp, vp, hrp, hip, mlp_w = _layer(xp, None, None, zeros_state, zeros_state, p, None)
        xs, ks, vs, hrs, his, _ = _layer(xs, cache_k[l], cache_v[l], state_ssm_re[l], state_ssm_im[l], p, mlp_w)
        for lst, val in zip(outs, (kp, vp, hrp, hip, ks, vs, hrs, his)):
            lst.append(val)
    return (xp, xs) + tuple(jnp.stack(o) for o in outs)
```

```python
import functools
import math

import jax
import jax.numpy as jnp
from jax import lax
from jax.experimental import pallas as pl
from jax.experimental.pallas import tpu as pltpu

D_MODEL = 2048
ATTN_WIDTH = 1024
SSM_WIDTH = 1024
HEAD_DIM = 64
N_HEADS = 16
N_KV_HEADS = 4
Q_PER_KV = 4
KV_WIDTH = 256
WINDOW = 128
SSM_GROUP = 16
N_SSM_GROUPS = 64
STATE_DIM = 64
N_STATE = N_SSM_GROUPS * STATE_DIM
D_FF = 8192
PROJ_WIDTH = ATTN_WIDTH + 2 * KV_WIDTH + SSM_WIDTH
EPS = 1e-6
NEG_INF = -1e30

LANES = 128
SUBLANES = 8
MXU_DIM = 256
VMEM_LIMIT = 56 * 1024 * 1024

GROUPS_PER_BLOCK = MXU_DIM // SSM_GROUP
N_SSM_BLOCKS = N_SSM_GROUPS // GROUPS_PER_BLOCK
BLOCK_STATES = GROUPS_PER_BLOCK * STATE_DIM
SCAN_LANES = 512
SSM_SUB_TILE_ROWS = 256
SSM_SUB_TILES = 2
MLP_FF_TILE = 1024

F32 = jnp.float32
BF16 = jnp.bfloat16


def _cparams(sem):
    return pltpu.CompilerParams(dimension_semantics=sem, vmem_limit_bytes=VMEM_LIMIT)


def _const_spec(shape):
    nd = len(shape)
    return pl.BlockSpec(shape, lambda *_: (0,) * nd, pipeline_mode=pl.Buffered(1))


def _discretize_kernel(are_ref, aim_ref, ldt_ref, bre_ref, bim_ref, cre_ref, cim_ref,
                       abr_ref, abi_ref, wb_ref, wct_ref):
    a_re = are_ref[...]
    a_im = aim_ref[...]
    dt = jnp.exp(ldt_ref[...])
    mag = jnp.exp(a_re * dt)
    ab_re = mag * jnp.cos(a_im * dt)
    ab_im = mag * jnp.sin(a_im * dt)
    abr_ref[...] = jnp.broadcast_to(ab_re, abr_ref.shape)
    abi_ref[...] = jnp.broadcast_to(ab_im, abi_ref.shape)
    x = ab_re - 1.0
    y = ab_im
    den = a_re * a_re + a_im * a_im
    k_re = (x * a_re + y * a_im) / den
    k_im = (y * a_re - x * a_im) / den
    b_re = bre_ref[...]
    b_im = bim_ref[...]
    bb_re = k_re * b_re - k_im * b_im
    bb_im = k_re * b_im + k_im * b_re

    rows = GROUPS_PER_BLOCK * SSM_GROUP
    row_group = lax.broadcasted_iota(jnp.int32, (rows, BLOCK_STATES), 0) // SSM_GROUP
    col_group = lax.broadcasted_iota(jnp.int32, (rows, BLOCK_STATES), 1) // STATE_DIM
    same_group = row_group == col_group

    def block_diag(m, j):
        blk = m[:, j * BLOCK_STATES:(j + 1) * BLOCK_STATES]
        return jnp.where(same_group, jnp.concatenate([blk] * GROUPS_PER_BLOCK, axis=0), 0.0).astype(BF16)

    c_re = cre_ref[...]
    c_im_neg = -cim_ref[...]
    for j in range(N_SSM_BLOCKS):
        wb_ref[j, :, :BLOCK_STATES] = block_diag(bb_re, j)
        wb_ref[j, :, BLOCK_STATES:] = block_diag(bb_im, j)
        wct_ref[j, :, :BLOCK_STATES] = block_diag(c_re, j)
        wct_ref[j, :, BLOCK_STATES:] = block_diag(c_im_neg, j)


def _ssm_discretize(a_re, a_im, log_dt, b_re, b_im, c_re, c_im):
    rep = jax.ShapeDtypeStruct((SUBLANES, N_STATE), F32)
    blocks = jax.ShapeDtypeStruct((N_SSM_BLOCKS, GROUPS_PER_BLOCK * SSM_GROUP, 2 * BLOCK_STATES), BF16)
    return pl.pallas_call(
        _discretize_kernel,
        out_shape=(rep, rep, blocks, blocks),
        name="ssm_discretize",
    )(a_re, a_im, log_dt, b_re, b_im, c_re, c_im)


def _head_rmsnorm(zc, gain, ones_blk):
    sq = zc * zc
    hi = sq.astype(BF16)
    lo = (sq - hi.astype(F32)).astype(BF16)
    ss = (jnp.dot(hi, ones_blk, preferred_element_type=F32)
          + jnp.dot(lo, ones_blk, preferred_element_type=F32))
    return zc * lax.rsqrt(ss * (1.0 / HEAD_DIM) + EPS) * gain


def _in_proj_kernel(x_ref, gn_ref, wq_ref, wr_ref, gq_ref, gk_ref, *rest):
    n_cast = (len(rest) - 3) // 2
    cast_in = rest[:n_cast]
    q_ref, kv_ref, u_ref = rest[n_cast:n_cast + 3]
    cast_out = rest[n_cast + 3:]
    for src, dst in zip(cast_in, cast_out):
        dst[...] = src[...].astype(BF16)
    x = x_ref[...]
    r = lax.rsqrt(jnp.mean(x * x, axis=-1, keepdims=True) + EPS)
    xn = (x * r * gn_ref[...]).astype(BF16)
    zq = jnp.dot(xn, wq_ref[...], preferred_element_type=F32)
    zr = jnp.dot(xn, wr_ref[...], preferred_element_type=F32)
    ri = lax.broadcasted_iota(jnp.int32, (MXU_DIM, MXU_DIM), 0) // HEAD_DIM
    ci = lax.broadcasted_iota(jnp.int32, (MXU_DIM, MXU_DIM), 1) // HEAD_DIM
    ones_blk = (ri == ci).astype(BF16)
    gq = gq_ref[...]
    for c in range(ATTN_WIDTH // MXU_DIM):
        sl = slice(c * MXU_DIM, (c + 1) * MXU_DIM)
        q_ref[:, sl] = (_head_rmsnorm(zq[:, sl], gq, ones_blk) * (HEAD_DIM ** -0.5)).astype(BF16)
    kv_ref[:, :KV_WIDTH] = _head_rmsnorm(zr[:, :KV_WIDTH], gk_ref[...], ones_blk)
    kv_ref[:, KV_WIDTH:] = zr[:, KV_WIDTH:2 * KV_WIDTH]
    u_ref[...] = zr[:, 2 * KV_WIDTH:]


def _in_proj(x2d, gn, wq, wr, gq, gk, tm, to_bf16=()):
    rows = x2d.shape[0]
    steps = rows // tm
    slab_specs = [pl.BlockSpec((w.shape[0] // steps, w.shape[1]), lambda i: (i, 0)) for w in to_bf16]
    assert all(w.shape[0] % (steps * 2 * SUBLANES) == 0 for w in to_bf16)
    return pl.pallas_call(
        _in_proj_kernel,
        grid=(steps,),
        in_specs=[
            pl.BlockSpec((tm, D_MODEL), lambda i: (i, 0)),
            _const_spec((1, D_MODEL)),
            _const_spec((D_MODEL, ATTN_WIDTH)),
            _const_spec((D_MODEL, PROJ_WIDTH - ATTN_WIDTH)),
            _const_spec((1, MXU_DIM)),
            _const_spec((1, MXU_DIM)),
        ] + slab_specs,
        out_specs=[
            pl.BlockSpec((tm, ATTN_WIDTH), lambda i: (i, 0)),
            pl.BlockSpec((tm, 2 * KV_WIDTH), lambda i: (i, 0)),
            pl.BlockSpec((tm, SSM_WIDTH), lambda i: (i, 0)),
        ] + slab_specs,
        out_shape=(
            jax.ShapeDtypeStruct((rows, ATTN_WIDTH), BF16),
            jax.ShapeDtypeStruct((rows, 2 * KV_WIDTH), F32),
            jax.ShapeDtypeStruct((rows, SSM_WIDTH), F32),
        ) + tuple(jax.ShapeDtypeStruct(w.shape, BF16) for w in to_bf16),
        compiler_params=_cparams(("arbitrary",)),
        name="in_proj",
    )(x2d, gn, wq, wr, gq, gk, *to_bf16)


def _alibi_slope(head):
    return 2.0 ** (-8.0 * (head + 1) / N_HEADS)


def _prompt_attn_kernel(sink_ref, q_ref, kvp_ref, kvc_ref, bias_ref, o_ref):
    has_prev = (pl.program_id(1) > 0).astype(jnp.int32)
    tq = WINDOW
    lane = lax.broadcasted_iota(jnp.int32, (tq, LANES), 1)
    low = lane < HEAD_DIM
    zero = jnp.zeros((tq, LANES), BF16)
    for pair in range(N_KV_HEADS // 2):
        ks = slice(pair * LANES, (pair + 1) * LANES)
        vs = slice(KV_WIDTH + pair * LANES, KV_WIDTH + (pair + 1) * LANES)
        kc = jnp.concatenate([kvp_ref[:, ks], kvc_ref[:, ks]], axis=0).astype(BF16)
        vc = jnp.concatenate([kvp_ref[:, vs], kvc_ref[:, vs]], axis=0).astype(BF16)
        qb = [q_ref[:, (pair * Q_PER_KV + r) * LANES:(pair * Q_PER_KV + r + 1) * LANES] for r in range(Q_PER_KV)]
        qs = jnp.concatenate([jnp.where(low, b, zero) for b in qb] + [jnp.where(low, zero, b) for b in qb], axis=0)
        scores = lax.dot_general(qs, kc, (((1,), (1,)), ((), ())), preferred_element_type=F32)
        probs = []
        inv_den = []
        for hh in range(2 * Q_PER_KV):
            head = (2 * pair + hh // Q_PER_KV) * Q_PER_KV + hh % Q_PER_KV
            s = scores[hh * tq:(hh + 1) * tq] + bias_ref[has_prev, head]
            sink = sink_ref[head]
            m = jnp.maximum(jnp.max(s, axis=-1, keepdims=True), sink)
            p = jnp.exp(s - m)
            den = jnp.sum(p, axis=-1, keepdims=True) + jnp.exp(sink - m)
            probs.append(p.astype(BF16))
            inv_den.append(1.0 / den)
        pv = jnp.dot(jnp.concatenate(probs, axis=0), vc, preferred_element_type=F32)
        for r in range(Q_PER_KV):
            o_low = pv[r * tq:(r + 1) * tq] * inv_den[r]
            o_high = pv[(Q_PER_KV + r) * tq:(Q_PER_KV + r + 1) * tq] * inv_den[Q_PER_KV + r]
            o_ref[:, (pair * Q_PER_KV + r) * LANES:(pair * Q_PER_KV + r + 1) * LANES] = (
                jnp.where(low, o_low, o_high).astype(BF16))


def _prompt_attn_bias():
    w = WINDOW
    slope = jnp.exp2(-8.0 * (jnp.arange(N_HEADS, dtype=F32) + 1.0) / N_HEADS)
    i = jnp.arange(w)[:, None]
    j = jnp.arange(2 * w)[None, :]
    d = i + w - j
    ok = (d >= 0) & (d <= w)
    ok = jnp.stack([ok & (j >= w), ok])
    return jnp.where(ok[:, None], -slope[None, :, None, None] * d.astype(F32)[None, None], NEG_INF)


def _prompt_attention(sinks, q, kv, batch, seq):
    nb = seq // WINDOW
    bias = _prompt_attn_bias()
    return pl.pallas_call(
        _prompt_attn_kernel,
        grid=(batch, nb),
        in_specs=[
            pl.BlockSpec(memory_space=pltpu.SMEM),
            pl.BlockSpec((WINDOW, ATTN_WIDTH), lambda b, n: (b * nb + n, 0)),
            pl.BlockSpec((WINDOW, 2 * KV_WIDTH), lambda b, n: (b * nb + jnp.maximum(n - 1, 0), 0)),
            pl.BlockSpec((WINDOW, 2 * KV_WIDTH), lambda b, n: (b * nb + n, 0)),
            _const_spec(bias.shape),
        ],
        out_specs=pl.BlockSpec((WINDOW, ATTN_WIDTH), lambda b, n: (b * nb + n, 0)),
        out_shape=jax.ShapeDtypeStruct((batch * seq, ATTN_WIDTH), BF16),
        compiler_params=_cparams(("parallel", "parallel")),
        name="prompt_attention",
    )(sinks, q, kv, kv, bias)


SAMPLE_ATTN_BATCH = 16
SAMPLE_ATTN_UNROLL = 2


def _sample_attn_kernel(sink_ref, q_ref, ckt_ref, cvt_ref, kvn_ref, o_ref, nkt_ref, nvt_ref, *, t_new):
    rows = Q_PER_KV * t_new
    tk = 2 * WINDOW
    i = lax.broadcasted_iota(jnp.int32, (rows, tk), 0) % t_new
    c = lax.broadcasted_iota(jnp.int32, (rows, tk), 1)
    is_new = c >= tk - t_new
    j = jnp.where(c < WINDOW, c, c - (WINDOW - t_new))
    d = i + WINDOW - j
    valid = (d >= 0) & (d <= WINDOW) & ((c < WINDOW) | is_new)
    delta = d.astype(F32)
    lane_q = lax.broadcasted_iota(jnp.int32, (rows, LANES), 1)
    low = lane_q < HEAD_DIM
    zero_q = jnp.zeros((rows, LANES), BF16)
    lane_w = lax.broadcasted_iota(jnp.int32, (WINDOW, LANES), 1)
    tail = lane_w >= WINDOW - t_new
    rid = lax.broadcasted_iota(jnp.int32, (rows, 1), 0) // t_new
    alibi, sinks = [], []
    for pair in range(N_KV_HEADS // 2):
        sl, sk = [], []
        for half in range(2):
            slope = jnp.zeros((rows, 1), F32)
            sink = jnp.zeros((rows, 1), F32)
            for r in range(Q_PER_KV):
                head = (2 * pair + half) * Q_PER_KV + r
                slope = jnp.where(rid == r, _alibi_slope(head), slope)
                sink = jnp.where(rid == r, sink_ref[head], sink)
            sl.append(slope)
            sk.append(sink)
        alibi.append(jnp.concatenate(sl, axis=0) * jnp.concatenate([delta, delta], axis=0))
        sinks.append(jnp.concatenate(sk, axis=0))
    valid2 = jnp.concatenate([valid, valid], axis=0)

    lead_zero = jnp.zeros((WINDOW - SUBLANES, LANES), F32)

    def new_rows_transposed(e, lanes):
        return jnp.concatenate([lead_zero, kvn_ref[e, :, lanes]], axis=0).T

    def shifted(old_t, new_t):
        return jnp.where(tail, new_t, pltpu.roll(old_t, WINDOW - t_new, axis=1))

    def body(e, carry):
        for pair in range(N_KV_HEADS // 2):
            heads = slice(2 * pair, 2 * pair + 2)
            kt = ckt_ref[e, heads].reshape(2 * HEAD_DIM, WINDOW)
            vt = cvt_ref[e, heads].reshape(2 * HEAD_DIM, WINDOW)
            knt = new_rows_transposed(e, slice(pair * LANES, (pair + 1) * LANES))
            vnt = new_rows_transposed(e, slice(KV_WIDTH + pair * LANES, KV_WIDTH + (pair + 1) * LANES))
            nkt_ref[e, heads] = shifted(kt, knt).reshape(2, HEAD_DIM, WINDOW)
            nvt_ref[e, heads] = shifted(vt, vnt).reshape(2, HEAD_DIM, WINDOW)

            qp = q_ref[e, pair * rows:(pair + 1) * rows, :]
            qs = jnp.concatenate([jnp.where(low, qp, zero_q), jnp.where(low, zero_q, qp)], axis=0)
            kt_all = jnp.concatenate([kt, knt], axis=1).astype(BF16)
            vt_all = jnp.concatenate([vt, vnt], axis=1).astype(BF16)
            scores = jnp.dot(qs, kt_all, preferred_element_type=F32)
            s = jnp.where(valid2, scores - alibi[pair], NEG_INF)
            m = jnp.maximum(jnp.max(s, axis=-1, keepdims=True), sinks[pair])
            p = jnp.exp(s - m)
            den = jnp.sum(p, axis=-1, keepdims=True) + jnp.exp(sinks[pair] - m)
            pv = lax.dot_general(p.astype(BF16), vt_all, (((1,), (1,)), ((), ())),
                                 preferred_element_type=F32)
            pv = pv * (1.0 / den)
            o_ref[e, pair * rows:(pair + 1) * rows, :] = jnp.where(low, pv[:rows], pv[rows:]).astype(BF16)
        return carry

    lax.fori_loop(0, q_ref.shape[0], body, 0, unroll=SAMPLE_ATTN_UNROLL)


def _sample_attention(sinks, q3, ckt, cvt, kvn, t_new):
    n = q3.shape[0]
    bn = SAMPLE_ATTN_BATCH
    qrows = q3.shape[1]
    cache_spec = pl.BlockSpec((bn, N_KV_HEADS, HEAD_DIM, WINDOW), lambda b: (b, 0, 0, 0))
    cache_shape = jax.ShapeDtypeStruct((n, N_KV_HEADS, HEAD_DIM, WINDOW), F32)
    return pl.pallas_call(
        functools.partial(_sample_attn_kernel, t_new=t_new),
        grid=(n // bn,),
        in_specs=[
            pl.BlockSpec(memory_space=pltpu.SMEM),
            pl.BlockSpec((bn, qrows, LANES), lambda b: (b, 0, 0)),
            cache_spec,
            cache_spec,
            pl.BlockSpec((bn, SUBLANES, 2 * KV_WIDTH), lambda b: (b, 0, 0)),
        ],
        out_specs=[pl.BlockSpec((bn, qrows, LANES), lambda b: (b, 0, 0)), cache_spec, cache_spec],
        out_shape=(jax.ShapeDtypeStruct((n, qrows, LANES), BF16), cache_shape, cache_shape),
        compiler_params=_cparams(("parallel",)),
        name="sample_attention",
    )(sinks, q3, ckt, cvt, kvn)


def _gelu_tanh(x):
    c = math.sqrt(2.0 / math.pi)
    return 0.5 * x * (1.0 + jnp.tanh(c * (x + 0.044715 * (x * x * x))))


def _ssm_kernel(u_ref, h0re_ref, h0im_ref, ar_ref, ai_ref, wb_ref, wct_ref, d_ref, wglu_ref, bglu_ref, gs_ref,
                mix_ref, hre_ref, him_ref, s_ref, *, nb, steps, n_sub):
    @pl.when(pl.program_id(0) == 0)
    def _():
        hre_ref[...] = h0re_ref[...]
        him_ref[...] = h0im_ref[...]

    for sub in range(n_sub):
        _ssm_sub_tile(u_ref, ar_ref, ai_ref, wb_ref, wct_ref, d_ref, wglu_ref, bglu_ref, gs_ref,
                      mix_ref, hre_ref, him_ref, s_ref, nb=nb, steps=steps, sub=sub)


def _ssm_sub_tile(u_ref, ar_ref, ai_ref, wb_ref, wct_ref, d_ref, wglu_ref, bglu_ref, gs_ref,
                  mix_ref, hre_ref, him_ref, s_ref, *, nb, steps, sub):
    rows = nb * steps
    row0 = sub * rows
    pos = slice(sub * (rows // u_ref.shape[0]), (sub + 1) * (rows // u_ref.shape[0]))
    tm_row = lax.broadcasted_iota(jnp.int32, (rows, rows), 0)
    bm_col = lax.broadcasted_iota(jnp.int32, (rows, rows), 1)
    to_time_major = (bm_col == (tm_row % nb) * steps + tm_row // nb).astype(BF16)
    bm_row = lax.broadcasted_iota(jnp.int32, (rows, rows), 0)
    tm_col = lax.broadcasted_iota(jnp.int32, (rows, rows), 1)
    to_batch_major = (bm_row == (tm_col % nb) * steps + tm_col // nb).astype(BF16)

    u_bm = u_ref[:, pos, :].reshape(rows, SSM_WIDTH)
    u_hi = u_bm.astype(BF16)
    u_lo = (u_bm - u_hi.astype(F32)).astype(BF16)
    u_hi_tm = jnp.dot(to_time_major, u_hi, preferred_element_type=F32)
    u = u_hi_tm + jnp.dot(to_time_major, u_lo, preferred_element_type=F32)
    ub = u_hi_tm.astype(BF16)
    blk_cols = 2 * BLOCK_STATES
    ys = []
    for j in range(N_SSM_BLOCKS):
        s_ref[row0:row0 + rows, j * blk_cols:(j + 1) * blk_cols] = jnp.dot(
            ub[:, j * MXU_DIM:(j + 1) * MXU_DIM], wb_ref[j], preferred_element_type=F32)
        for part in range(BLOCK_STATES // SCAN_LANES):
            rc = j * blk_cols + part * SCAN_LANES
            ic = rc + BLOCK_STATES
            sc = j * BLOCK_STATES + part * SCAN_LANES
            a_re = ar_ref[:, sc:sc + SCAN_LANES]
            a_im = ai_ref[:, sc:sc + SCAN_LANES]
            for bg in range(nb // SUBLANES):
                b0 = bg * SUBLANES
                h_re = hre_ref[b0:b0 + SUBLANES, sc:sc + SCAN_LANES]
                h_im = him_ref[b0:b0 + SUBLANES, sc:sc + SCAN_LANES]
                for t in range(steps):
                    row = row0 + t * nb + b0
                    n_re = a_re * h_re - a_im * h_im + s_ref[row:row + SUBLANES, rc:rc + SCAN_LANES]
                    n_im = a_re * h_im + a_im * h_re + s_ref[row:row + SUBLANES, ic:ic + SCAN_LANES]
                    s_ref[row:row + SUBLANES, rc:rc + SCAN_LANES] = n_re
                    s_ref[row:row + SUBLANES, ic:ic + SCAN_LANES] = n_im
                    h_re, h_im = n_re, n_im
                hre_ref[b0:b0 + SUBLANES, sc:sc + SCAN_LANES] = h_re
                him_ref[b0:b0 + SUBLANES, sc:sc + SCAN_LANES] = h_im
        hb = s_ref[row0:row0 + rows, j * blk_cols:(j + 1) * blk_cols].astype(BF16)
        ys.append(lax.dot_general(hb, wct_ref[j], (((1,), (1,)), ((), ())), preferred_element_type=F32))

    y = jnp.concatenate(ys, axis=1) + d_ref[...] * u
    g = _gelu_tanh(y)
    gate = jnp.dot(g.astype(BF16), wglu_ref[...], preferred_element_type=F32) + bglu_ref[...]
    so = g * jax.nn.sigmoid(gate)
    r = lax.rsqrt(jnp.mean(so * so, axis=-1, keepdims=True) + EPS)
    mix_tm = (so * r * gs_ref[...]).astype(BF16)
    mix_bm = jnp.dot(to_batch_major, mix_tm, preferred_element_type=F32).astype(BF16)
    mix_ref[:, pos, :] = mix_bm.reshape(mix_ref.shape[0], rows // mix_ref.shape[0], SSM_WIDTH)


def _ssm_mixer(u3, h0_re, h0_im, ar8, ai8, wb, wct, d, wglu_b, bglu, gs, nb, steps, n_sub):
    nbv, seq, _ = u3.shape
    blk_rows = n_sub * nb * steps // nbv
    assert blk_rows % SUBLANES == 0 and seq % blk_rows == 0
    cols = 2 * N_STATE
    tr = n_sub * nb * steps
    return pl.pallas_call(
        functools.partial(_ssm_kernel, nb=nb, steps=steps, n_sub=n_sub),
        grid=(seq // blk_rows,),
        in_specs=[
            pl.BlockSpec((nbv, blk_rows, SSM_WIDTH), lambda i: (0, i, 0)),
            _const_spec((nb, N_STATE)),
            _const_spec((nb, N_STATE)),
            _const_spec((SUBLANES, N_STATE)),
            _const_spec((SUBLANES, N_STATE)),
            _const_spec((N_SSM_BLOCKS, MXU_DIM, 2 * BLOCK_STATES)),
            _const_spec((N_SSM_BLOCKS, MXU_DIM, 2 * BLOCK_STATES)),
            _const_spec((1, SSM_WIDTH)),
            _const_spec((SSM_WIDTH, SSM_WIDTH)),
            _const_spec((1, SSM_WIDTH)),
            _const_spec((1, SSM_WIDTH)),
        ],
        out_specs=[
            pl.BlockSpec((nbv, blk_rows, SSM_WIDTH), lambda i: (0, i, 0)),
            pl.BlockSpec((nb, N_STATE), lambda i: (0, 0)),
            pl.BlockSpec((nb, N_STATE), lambda i: (0, 0)),
        ],
        out_shape=(
            jax.ShapeDtypeStruct((nbv, seq, SSM_WIDTH), BF16),
            jax.ShapeDtypeStruct((nb, N_STATE), F32),
            jax.ShapeDtypeStruct((nb, N_STATE), F32),
        ),
        scratch_shapes=[pltpu.VMEM((tr, cols), F32)],
        compiler_params=_cparams(("arbitrary",)),
        name="ssm_mixer",
    )(u3, h0_re, h0_im, ar8, ai8, wb, wct, d, wglu_b, bglu, gs)


def _out_mlp_kernel(x_ref, a_ref, ga_ref, ms_ref, woa_ref, wos_ref, gm_ref, wup_ref, wdn_ref,
                    y_ref, hn_ref):
    @pl.when(pl.program_id(1) == 0)
    def _():
        a = a_ref[...].astype(F32)
        ra = lax.rsqrt(jnp.mean(a * a, axis=-1, keepdims=True) + EPS)
        ma = (a * ra * ga_ref[...]).astype(BF16)
        h = (x_ref[...]
             + jnp.dot(ma, woa_ref[...], preferred_element_type=F32)
             + jnp.dot(ms_ref[...], wos_ref[...], preferred_element_type=F32))
        y_ref[...] = h
        rh = lax.rsqrt(jnp.mean(h * h, axis=-1, keepdims=True) + EPS)
        hn_ref[...] = (h * rh * gm_ref[...]).astype(BF16)

    t = jnp.dot(hn_ref[...], wup_ref[...], preferred_element_type=F32)
    t = jnp.maximum(t, 0.0)
    t = (t * t).astype(BF16)
    y_ref[...] += jnp.dot(t, wdn_ref[...], preferred_element_type=F32)


def _out_mlp(x2d, attn, ga, mix_s, woa, wos, gm, wup, wdn, tm, tf):
    rows = x2d.shape[0]
    return pl.pallas_call(
        _out_mlp_kernel,
        grid=(rows // tm, D_FF // tf),
        in_specs=[
            pl.BlockSpec((tm, D_MODEL), lambda i, j: (i, 0)),
            pl.BlockSpec((tm, ATTN_WIDTH), lambda i, j: (i, 0)),
            _const_spec((1, ATTN_WIDTH)),
            pl.BlockSpec((tm, SSM_WIDTH), lambda i, j: (i, 0)),
            _const_spec((ATTN_WIDTH, D_MODEL)),
            _const_spec((SSM_WIDTH, D_MODEL)),
            _const_spec((1, D_MODEL)),
            pl.BlockSpec((D_MODEL, tf), lambda i, j: (0, j)),
            pl.BlockSpec((tf, D_MODEL), lambda i, j: (j, 0)),
        ],
        out_specs=pl.BlockSpec((tm, D_MODEL), lambda i, j: (i, 0)),
        out_shape=jax.ShapeDtypeStruct((rows, D_MODEL), F32),
        scratch_shapes=[pltpu.VMEM((tm, D_MODEL), BF16)],
        compiler_params=_cparams(("parallel", "arbitrary")),
        name="out_mlp",
    )(x2d, attn, ga, mix_s, woa, wos, gm, wup, wdn)


def _pair_heads(a, axis):
    shape = a.shape
    split = shape[:axis] + (N_KV_HEADS // 2, 2, Q_PER_KV, HEAD_DIM) + shape[axis + 1:]
    return jnp.swapaxes(a.reshape(split), axis + 1, axis + 2).reshape(shape)


def _layer(x, cache_k, cache_v, h0_re, h0_im, p, mlp_w):
    n, t = x.shape[:2]
    rows = n * t
    tm = 512
    x2d = x.reshape(rows, D_MODEL)
    if mlp_w is None:
        q, kv, u, *mlp_w = _in_proj(x2d, p['gn'], p['wq'], p['wr'], p['gq'], p['gk'], tm,
                                    to_bf16=(p['w_up_f32'], p['w_down_f32']))
    else:
        q, kv, u = _in_proj(x2d, p['gn'], p['wq'], p['wr'], p['gq'], p['gk'], tm)
    kv3 = kv.reshape(n, t, 2 * KV_WIDTH)

    if cache_k is None:
        attn = _prompt_attention(p['sinks'], q, kv, n, t)
        tail = kv3[:, t - WINDOW:]
        new_k = tail[..., :KV_WIDTH].reshape(n, WINDOW, N_KV_HEADS, HEAD_DIM)
        new_v = tail[..., KV_WIDTH:].reshape(n, WINDOW, N_KV_HEADS, HEAD_DIM)
        u3 = u.reshape(n, t, SSM_WIDTH)
        ssm_steps, ssm_subs = SSM_SUB_TILE_ROWS // n, SSM_SUB_TILES
    else:
        nblk = ATTN_WIDTH // LANES
        q3 = q.reshape(n, t, nblk, LANES).transpose(0, 2, 1, 3).reshape(n, nblk * t, LANES)
        kvn = jnp.pad(kv3, ((0, 0), (SUBLANES - t, 0), (0, 0)))
        ckt = cache_k.transpose(0, 2, 3, 1)
        cvt = cache_v.transpose(0, 2, 3, 1)
        a3, nkt, nvt = _sample_attention(p['sinks'], q3, ckt, cvt, kvn, t)
        attn = a3.reshape(n, nblk, t, LANES).transpose(0, 2, 1, 3).reshape(rows, ATTN_WIDTH)
        new_k = nkt.transpose(0, 3, 1, 2)
        new_v = nvt.transpose(0, 3, 1, 2)
        u3 = u.reshape(1, rows, SSM_WIDTH)
        ssm_steps, ssm_subs = t, 1

    mix3, h_re, h_im = _ssm_mixer(u3, h0_re.reshape(n, N_STATE), h0_im.reshape(n, N_STATE), p['ar8'], p['ai8'],
                                  p['wb'], p['wct'], p['d'], p['w_glu'], p['b_glu'], p['gs'], n, ssm_steps, ssm_subs)
    mix_s = mix3.reshape(rows, SSM_WIDTH)
    h_re = h_re.reshape(n, N_SSM_GROUPS, STATE_DIM)
    h_im = h_im.reshape(n, N_SSM_GROUPS, STATE_DIM)

    y = _out_mlp(x2d, attn, p['ga'], mix_s, p['woa'], p['wos'], p['gm'], mlp_w[0], mlp_w[1], tm, MLP_FF_TILE)
    return y.reshape(n, t, D_MODEL), new_k, new_v, h_re, h_im, mlp_w


def _prepare_params(l, attn_norm_g, w_in, q_norm_g, k_norm_g, attn_sinks,
                    ssm_A_re, ssm_A_im, ssm_log_dt, ssm_B_re, ssm_B_im, ssm_C_re, ssm_C_im, ssm_D,
                    w_glu, b_glu, attn_out_g, ssm_out_g, w_out, mlp_norm_g, w_mlp_up, w_mlp_down):
    w_in_l = w_in[l]
    heads_per_blk = MXU_DIM // HEAD_DIM

    a_re = ssm_A_re[l].reshape(1, N_STATE)
    a_im = ssm_A_im[l].reshape(1, N_STATE)
    ldt = jnp.broadcast_to(ssm_log_dt[l][:, None], (N_SSM_GROUPS, STATE_DIM)).reshape(1, N_STATE)
    b_re = ssm_B_re[l].transpose(2, 0, 1).reshape(SSM_GROUP, N_STATE)
    b_im = ssm_B_im[l].transpose(2, 0, 1).reshape(SSM_GROUP, N_STATE)
    c_re = ssm_C_re[l].transpose(1, 0, 2).reshape(SSM_GROUP, N_STATE)
    c_im = ssm_C_im[l].transpose(1, 0, 2).reshape(SSM_GROUP, N_STATE)
    ar8, ai8, wb, wct = _ssm_discretize(a_re, a_im, ldt, b_re, b_im, c_re, c_im)

    w_out_l = w_out[l]
    return dict(
        gn=attn_norm_g[l].reshape(1, D_MODEL),
        wq=_pair_heads(w_in_l[:, :ATTN_WIDTH], 1).astype(BF16),
        wr=w_in_l[:, ATTN_WIDTH:].astype(BF16),
        gq=jnp.tile(q_norm_g[l], heads_per_blk).reshape(1, MXU_DIM),
        gk=jnp.tile(k_norm_g[l], heads_per_blk).reshape(1, MXU_DIM),
        sinks=attn_sinks[l].astype(F32),
        ar8=ar8, ai8=ai8, wb=wb, wct=wct,
        d=ssm_D[l].reshape(1, SSM_WIDTH),
        w_glu=w_glu[l].astype(BF16),
        b_glu=b_glu[l].reshape(1, SSM_WIDTH),
        gs=ssm_out_g[l].reshape(1, SSM_WIDTH),
        ga=_pair_heads(attn_out_g[l], 0).reshape(1, ATTN_WIDTH),
        woa=_pair_heads(w_out_l[:ATTN_WIDTH], 0).astype(BF16),
        wos=w_out_l[ATTN_WIDTH:].astype(BF16),
        gm=mlp_norm_g[l].reshape(1, D_MODEL),
        w_up_f32=w_mlp_up[l],
        w_down_f32=w_mlp_down[l],
    )


def kernel(x_prompt, x_sample, cache_k, cache_v, state_ssm_re, state_ssm_im, attn_norm_g, w_in, q_norm_g, k_norm_g, attn_sinks, ssm_A_re, ssm_A_im, ssm_log_dt, ssm_B_re, ssm_B_im, ssm_C_re, ssm_C_im, ssm_D, w_glu, b_glu, attn_out_g, ssm_out_g, w_out, mlp_norm_g, w_mlp_up, w_mlp_down):
    depth = w_in.shape[0]
    xp, xs = x_prompt, x_sample
    zeros_state = jnp.zeros((x_prompt.shape[0], N_SSM_GROUPS, STATE_DIM), F32)
    outs = [[] for _ in range(8)]
    for l in range(depth):
        p = _prepare_params(l, attn_norm_g, w_in, q_norm_g, k_norm_g, attn_sinks,
                            ssm_A_re, ssm_A_im, ssm_log_dt, ssm_B_re, ssm_B_im, ssm_C_re, ssm_C_im, ssm_D,
                            w_glu, b_glu, attn_out_g, ssm_out_g, w_out, mlp_norm_g, w_mlp_up, w_mlp_down)
        xp, kp, vp, hrp, hip, mlp_w = _layer(xp, None, None, zeros_state, zeros_state, p, None)
        xs, ks, vs, hrs, his, _ = _layer(xs, cache_k[l], cache_v[l], state_ssm_re[l], state_ssm_im[l], p, mlp_w)
        for lst, val in zip(outs, (kp, vp, hrp, hip, ks, vs, hrs, his)):
            lst.append(val)
    return (xp, xs) + tuple(jnp.stack(o) for o in outs)
```

```python
import functools
import math

import jax
import jax.numpy as jnp
from jax import lax
from jax.experimental import pallas as pl
from jax.experimental.pallas import tpu as pltpu

D_MODEL = 2048
ATTN_WIDTH = 1024
SSM_WIDTH = 1024
HEAD_DIM = 64
N_HEADS = 16
N_KV_HEADS = 4
Q_PER_KV = 4
KV_WIDTH = 256
WINDOW = 128
SSM_GROUP = 16
N_SSM_GROUPS = 64
STATE_DIM = 64
N_STATE = N_SSM_GROUPS * STATE_DIM
D_FF = 8192
PROJ_WIDTH = ATTN_WIDTH + 2 * KV_WIDTH + SSM_WIDTH
EPS = 1e-6
NEG_INF = -1e30

LANES = 128
SUBLANES = 8
MXU_DIM = 256
VMEM_LIMIT = 56 * 1024 * 1024

GROUPS_PER_BLOCK = MXU_DIM // SSM_GROUP
N_SSM_BLOCKS = N_SSM_GROUPS // GROUPS_PER_BLOCK
BLOCK_STATES = GROUPS_PER_BLOCK * STATE_DIM
SCAN_LANES = 512
SSM_SUB_TILE_ROWS = 256
SSM_SUB_TILES = 2
SSM_TAIL_LAG_BLOCKS = 4
MLP_FF_TILE = 1024

F32 = jnp.float32
BF16 = jnp.bfloat16


def _cparams(sem):
    return pltpu.CompilerParams(dimension_semantics=sem, vmem_limit_bytes=VMEM_LIMIT)


def _const_spec(shape):
    nd = len(shape)
    return pl.BlockSpec(shape, lambda *_: (0,) * nd, pipeline_mode=pl.Buffered(1))


def _discretize_kernel(are_ref, aim_ref, ldt_ref, bre_ref, bim_ref, cre_ref, cim_ref,
                       abr_ref, abi_ref, wb_ref, wct_ref):
    a_re = are_ref[...]
    a_im = aim_ref[...]
    dt = jnp.exp(ldt_ref[...])
    mag = jnp.exp(a_re * dt)
    ab_re = mag * jnp.cos(a_im * dt)
    ab_im = mag * jnp.sin(a_im * dt)
    abr_ref[...] = jnp.broadcast_to(ab_re, abr_ref.shape)
    abi_ref[...] = jnp.broadcast_to(ab_im, abi_ref.shape)
    x = ab_re - 1.0
    y = ab_im
    den = a_re * a_re + a_im * a_im
    k_re = (x * a_re + y * a_im) / den
    k_im = (y * a_re - x * a_im) / den
    b_re = bre_ref[...]
    b_im = bim_ref[...]
    bb_re = k_re * b_re - k_im * b_im
    bb_im = k_re * b_im + k_im * b_re

    rows = GROUPS_PER_BLOCK * SSM_GROUP
    row_group = lax.broadcasted_iota(jnp.int32, (rows, BLOCK_STATES), 0) // SSM_GROUP
    col_group = lax.broadcasted_iota(jnp.int32, (rows, BLOCK_STATES), 1) // STATE_DIM
    same_group = row_group == col_group

    def block_diag(m, j):
        blk = m[:, j * BLOCK_STATES:(j + 1) * BLOCK_STATES]
        return jnp.where(same_group, jnp.concatenate([blk] * GROUPS_PER_BLOCK, axis=0), 0.0).astype(BF16)

    c_re = cre_ref[...]
    c_im_neg = -cim_ref[...]
    for j in range(N_SSM_BLOCKS):
        wb_ref[j, :, :BLOCK_STATES] = block_diag(bb_re, j)
        wb_ref[j, :, BLOCK_STATES:] = block_diag(bb_im, j)
        wct_ref[j, :, :BLOCK_STATES] = block_diag(c_re, j)
        wct_ref[j, :, BLOCK_STATES:] = block_diag(c_im_neg, j)


def _ssm_discretize(a_re, a_im, log_dt, b_re, b_im, c_re, c_im):
    rep = jax.ShapeDtypeStruct((SUBLANES, N_STATE), F32)
    blocks = jax.ShapeDtypeStruct((N_SSM_BLOCKS, GROUPS_PER_BLOCK * SSM_GROUP, 2 * BLOCK_STATES), BF16)
    return pl.pallas_call(
        _discretize_kernel,
        out_shape=(rep, rep, blocks, blocks),
        name="ssm_discretize",
    )(a_re, a_im, log_dt, b_re, b_im, c_re, c_im)


def _head_rmsnorm(zc, gain, ones_blk):
    sq = (zc * zc).astype(BF16)
    ss = jnp.dot(sq, ones_blk, preferred_element_type=F32)
    return zc * lax.rsqrt(ss * (1.0 / HEAD_DIM) + EPS) * gain


def _in_proj_kernel(x_ref, gn_ref, wq_ref, wr_ref, gq_ref, gk_ref, *rest):
    n_cast = (len(rest) - 3) // 2
    cast_in = rest[:n_cast]
    q_ref, kv_ref, u_ref = rest[n_cast:n_cast + 3]
    cast_out = rest[n_cast + 3:]
    for src, dst in zip(cast_in, cast_out):
        dst[...] = src[...].astype(BF16)
    x = x_ref[...]
    r = lax.rsqrt(jnp.mean(x * x, axis=-1, keepdims=True) + EPS)
    xn = (x * r * gn_ref[...]).astype(BF16)
    zq = jnp.dot(xn, wq_ref[...], preferred_element_type=F32)
    zr = jnp.dot(xn, wr_ref[...], preferred_element_type=F32)
    ri = lax.broadcasted_iota(jnp.int32, (MXU_DIM, MXU_DIM), 0) // HEAD_DIM
    ci = lax.broadcasted_iota(jnp.int32, (MXU_DIM, MXU_DIM), 1) // HEAD_DIM
    ones_blk = (ri == ci).astype(BF16)
    gq = gq_ref[...]
    for c in range(ATTN_WIDTH // MXU_DIM):
        sl = slice(c * MXU_DIM, (c + 1) * MXU_DIM)
        q_ref[:, sl] = (_head_rmsnorm(zq[:, sl], gq, ones_blk) * (HEAD_DIM ** -0.5)).astype(BF16)
    kv_ref[:, :KV_WIDTH] = _head_rmsnorm(zr[:, :KV_WIDTH], gk_ref[...], ones_blk)
    kv_ref[:, KV_WIDTH:] = zr[:, KV_WIDTH:2 * KV_WIDTH]
    u_ref[...] = zr[:, 2 * KV_WIDTH:]


def _in_proj(x2d, gn, wq, wr, gq, gk, tm, to_bf16=()):
    rows = x2d.shape[0]
    steps = rows // tm
    slab_specs = [pl.BlockSpec((w.shape[0] // steps, w.shape[1]), lambda i: (i, 0)) for w in to_bf16]
    assert all(w.shape[0] % (steps * 2 * SUBLANES) == 0 for w in to_bf16)
    return pl.pallas_call(
        _in_proj_kernel,
        grid=(steps,),
        in_specs=[
            pl.BlockSpec((tm, D_MODEL), lambda i: (i, 0)),
            _const_spec((1, D_MODEL)),
            _const_spec((D_MODEL, ATTN_WIDTH)),
            _const_spec((D_MODEL, PROJ_WIDTH - ATTN_WIDTH)),
            _const_spec((1, MXU_DIM)),
            _const_spec((1, MXU_DIM)),
        ] + slab_specs,
        out_specs=[
            pl.BlockSpec((tm, ATTN_WIDTH), lambda i: (i, 0)),
            pl.BlockSpec((tm, 2 * KV_WIDTH), lambda i: (i, 0)),
            pl.BlockSpec((tm, SSM_WIDTH), lambda i: (i, 0)),
        ] + slab_specs,
        out_shape=(
            jax.ShapeDtypeStruct((rows, ATTN_WIDTH), BF16),
            jax.ShapeDtypeStruct((rows, 2 * KV_WIDTH), F32),
            jax.ShapeDtypeStruct((rows, SSM_WIDTH), F32),
        ) + tuple(jax.ShapeDtypeStruct(w.shape, BF16) for w in to_bf16),
        compiler_params=_cparams(("arbitrary",)),
        name="in_proj",
    )(x2d, gn, wq, wr, gq, gk, *to_bf16)


def _alibi_slope(head):
    return 2.0 ** (-8.0 * (head + 1) / N_HEADS)


ATTN_BLOCKS_PER_STEP = 4


def _prompt_attn_kernel(sink_ref, q_ref, kvp_ref, kvc_ref, bias_ref, o_ref):
    tq = WINDOW
    lane = lax.broadcasted_iota(jnp.int32, (tq, LANES), 1)
    low = lane < HEAD_DIM
    zero = jnp.zeros((tq, LANES), BF16)

    def keys_or_values(blk, lanes):
        own = kvc_ref[blk * tq:(blk + 1) * tq, lanes]
        prev = kvp_ref[:, lanes] if blk == 0 else kvc_ref[(blk - 1) * tq:blk * tq, lanes]
        return jnp.concatenate([prev, own], axis=0).astype(BF16)

    for blk in range(q_ref.shape[0] // tq):
        rows = slice(blk * tq, (blk + 1) * tq)
        has_prev = (pl.program_id(1) > 0).astype(jnp.int32) if blk == 0 else 1
        for pair in range(N_KV_HEADS // 2):
            kc = keys_or_values(blk, slice(pair * LANES, (pair + 1) * LANES))
            vc = keys_or_values(blk, slice(KV_WIDTH + pair * LANES, KV_WIDTH + (pair + 1) * LANES))
            qb = [q_ref[rows, (pair * Q_PER_KV + r) * LANES:(pair * Q_PER_KV + r + 1) * LANES]
                  for r in range(Q_PER_KV)]
            qs = jnp.concatenate([jnp.where(low, b, zero) for b in qb] + [jnp.where(low, zero, b) for b in qb],
                                 axis=0)
            scores = lax.dot_general(qs, kc, (((1,), (1,)), ((), ())), preferred_element_type=F32)
            probs = []
            inv_den = []
            for hh in range(2 * Q_PER_KV):
                head = (2 * pair + hh // Q_PER_KV) * Q_PER_KV + hh % Q_PER_KV
                s = scores[hh * tq:(hh + 1) * tq] + bias_ref[has_prev, head]
                sink = sink_ref[head]
                m = jnp.maximum(jnp.max(s, axis=-1, keepdims=True), sink)
                p = jnp.exp(s - m)
                den = jnp.sum(p, axis=-1, keepdims=True) + jnp.exp(sink - m)
                probs.append(p.astype(BF16))
                inv_den.append(1.0 / den)
            pv = jnp.dot(jnp.concatenate(probs, axis=0), vc, preferred_element_type=F32)
            for r in range(Q_PER_KV):
                o_low = pv[r * tq:(r + 1) * tq] * inv_den[r]
                o_high = pv[(Q_PER_KV + r) * tq:(Q_PER_KV + r + 1) * tq] * inv_den[Q_PER_KV + r]
                o_ref[rows, (pair * Q_PER_KV + r) * LANES:(pair * Q_PER_KV + r + 1) * LANES] = (
                    jnp.where(low, o_low, o_high).astype(BF16))


def _prompt_attn_bias():
    w = WINDOW
    slope = jnp.exp2(-8.0 * (jnp.arange(N_HEADS, dtype=F32) + 1.0) / N_HEADS)
    i = jnp.arange(w)[:, None]
    j = jnp.arange(2 * w)[None, :]
    d = i + w - j
    ok = (d >= 0) & (d <= w)
    ok = jnp.stack([ok & (j >= w), ok])
    return jnp.where(ok[:, None], -slope[None, :, None, None] * d.astype(F32)[None, None], NEG_INF)


def _prompt_attention(sinks, q, kv, batch, seq):
    per = ATTN_BLOCKS_PER_STEP
    tq = per * WINDOW
    nb = seq // tq
    bias = _prompt_attn_bias()
    return pl.pallas_call(
        _prompt_attn_kernel,
        grid=(batch, nb),
        in_specs=[
            pl.BlockSpec(memory_space=pltpu.SMEM),
            pl.BlockSpec((tq, ATTN_WIDTH), lambda b, n: (b * nb + n, 0)),
            pl.BlockSpec((WINDOW, 2 * KV_WIDTH), lambda b, n: (b * nb * per + jnp.maximum(n * per - 1, 0), 0)),
            pl.BlockSpec((tq, 2 * KV_WIDTH), lambda b, n: (b * nb + n, 0)),
            _const_spec(bias.shape),
        ],
        out_specs=pl.BlockSpec((tq, ATTN_WIDTH), lambda b, n: (b * nb + n, 0)),
        out_shape=jax.ShapeDtypeStruct((batch * seq, ATTN_WIDTH), BF16),
        compiler_params=_cparams(("parallel", "parallel")),
        name="prompt_attention",
    )(sinks, q, kv, kv, bias)


SAMPLE_ATTN_BATCH = 16
SAMPLE_ATTN_UNROLL = 2


def _sample_attn_kernel(sink_ref, q_ref, ckt_ref, cvt_ref, kvn_ref, o_ref, nkt_ref, nvt_ref, *, t_new):
    rows = Q_PER_KV * t_new
    tk = 2 * WINDOW
    i = lax.broadcasted_iota(jnp.int32, (rows, tk), 0) % t_new
    c = lax.broadcasted_iota(jnp.int32, (rows, tk), 1)
    is_new = c >= tk - t_new
    j = jnp.where(c < WINDOW, c, c - (WINDOW - t_new))
    d = i + WINDOW - j
    valid = (d >= 0) & (d <= WINDOW) & ((c < WINDOW) | is_new)
    delta = d.astype(F32)
    lane_q = lax.broadcasted_iota(jnp.int32, (rows, LANES), 1)
    low = lane_q < HEAD_DIM
    zero_q = jnp.zeros((rows, LANES), BF16)
    lane_w = lax.broadcasted_iota(jnp.int32, (WINDOW, LANES), 1)
    tail = lane_w >= WINDOW - t_new
    rid = lax.broadcasted_iota(jnp.int32, (rows, 1), 0) // t_new
    alibi, sinks = [], []
    for pair in range(N_KV_HEADS // 2):
        sl, sk = [], []
        for half in range(2):
            slope = jnp.zeros((rows, 1), F32)
            sink = jnp.zeros((rows, 1), F32)
            for r in range(Q_PER_KV):
                head = (2 * pair + half) * Q_PER_KV + r
                slope = jnp.where(rid == r, _alibi_slope(head), slope)
                sink = jnp.where(rid == r, sink_ref[head], sink)
            sl.append(slope)
            sk.append(sink)
        alibi.append(jnp.concatenate(sl, axis=0) * jnp.concatenate([delta, delta], axis=0))
        sinks.append(jnp.concatenate(sk, axis=0))
    valid2 = jnp.concatenate([valid, valid], axis=0)

    lead_zero = jnp.zeros((WINDOW - SUBLANES, LANES), F32)

    def new_rows_transposed(e, lanes):
        return jnp.concatenate([lead_zero, kvn_ref[e, :, lanes]], axis=0).T

    def shifted(old_t, new_t):
        return jnp.where(tail, new_t, pltpu.roll(old_t, WINDOW - t_new, axis=1))

    def body(e, carry):
        for pair in range(N_KV_HEADS // 2):
            heads = slice(2 * pair, 2 * pair + 2)
            kt = ckt_ref[e, heads].reshape(2 * HEAD_DIM, WINDOW)
            vt = cvt_ref[e, heads].reshape(2 * HEAD_DIM, WINDOW)
            knt = new_rows_transposed(e, slice(pair * LANES, (pair + 1) * LANES))
            vnt = new_rows_transposed(e, slice(KV_WIDTH + pair * LANES, KV_WIDTH + (pair + 1) * LANES))
            nkt_ref[e, heads] = shifted(kt, knt).reshape(2, HEAD_DIM, WINDOW)
            nvt_ref[e, heads] = shifted(vt, vnt).reshape(2, HEAD_DIM, WINDOW)

            qp = q_ref[e, pair * rows:(pair + 1) * rows, :]
            qs = jnp.concatenate([jnp.where(low, qp, zero_q), jnp.where(low, zero_q, qp)], axis=0)
            kt_all = jnp.concatenate([kt, knt], axis=1).astype(BF16)
            vt_all = jnp.concatenate([vt, vnt], axis=1).astype(BF16)
            scores = jnp.dot(qs, kt_all, preferred_element_type=F32)
            s = jnp.where(valid2, scores - alibi[pair], NEG_INF)
            m = jnp.maximum(jnp.max(s, axis=-1, keepdims=True), sinks[pair])
            p = jnp.exp(s - m)
            den = jnp.sum(p, axis=-1, keepdims=True) + jnp.exp(sinks[pair] - m)
            pv = lax.dot_general(p.astype(BF16), vt_all, (((1,), (1,)), ((), ())),
                                 preferred_element_type=F32)
            pv = pv * (1.0 / den)
            o_ref[e, pair * rows:(pair + 1) * rows, :] = jnp.where(low, pv[:rows], pv[rows:]).astype(BF16)
        return carry

    lax.fori_loop(0, q_ref.shape[0], body, 0, unroll=SAMPLE_ATTN_UNROLL)


def _sample_attention(sinks, q3, ckt, cvt, kvn, t_new):
    n = q3.shape[0]
    bn = SAMPLE_ATTN_BATCH
    qrows = q3.shape[1]
    cache_spec = pl.BlockSpec((bn, N_KV_HEADS, HEAD_DIM, WINDOW), lambda b: (b, 0, 0, 0))
    cache_shape = jax.ShapeDtypeStruct((n, N_KV_HEADS, HEAD_DIM, WINDOW), F32)
    return pl.pallas_call(
        functools.partial(_sample_attn_kernel, t_new=t_new),
        grid=(n // bn,),
        in_specs=[
            pl.BlockSpec(memory_space=pltpu.SMEM),
            pl.BlockSpec((bn, qrows, LANES), lambda b: (b, 0, 0)),
            cache_spec,
            cache_spec,
            pl.BlockSpec((bn, SUBLANES, 2 * KV_WIDTH), lambda b: (b, 0, 0)),
        ],
        out_specs=[pl.BlockSpec((bn, qrows, LANES), lambda b: (b, 0, 0)), cache_spec, cache_spec],
        out_shape=(jax.ShapeDtypeStruct((n, qrows, LANES), BF16), cache_shape, cache_shape),
        compiler_params=_cparams(("parallel",)),
        name="sample_attention",
    )(sinks, q3, ckt, cvt, kvn)


def _gelu_tanh(x):
    c = math.sqrt(2.0 / math.pi)
    return 0.5 * x * (1.0 + jnp.tanh(c * (x + 0.044715 * (x * x * x))))


def _ssm_kernel(u_ref, h0re_ref, h0im_ref, ar_ref, ai_ref, wb_ref, wct_ref, d_ref, wglu_ref, bglu_ref, gs_ref,
                mix_ref, hre_ref, him_ref, s_ref, *, nb, steps, n_sub):
    @pl.when(pl.program_id(0) == 0)
    def _():
        hre_ref[...] = h0re_ref[...]
        him_ref[...] = h0im_ref[...]

    dims = dict(nb=nb, steps=steps)
    pending = []

    def trace_pending_tails():
        while pending:
            y, s = pending.pop(0)
            _ssm_gate_and_store(y, wglu_ref, bglu_ref, gs_ref, mix_ref, sub=s, **dims)

    for sub in range(n_sub):
        u, ub = _ssm_permute_in(u_ref, sub=sub, **dims)
        ys = []
        for j in range(N_SSM_BLOCKS):
            if j == SSM_TAIL_LAG_BLOCKS:
                trace_pending_tails()
            ys.append(_ssm_state_block(ub, j, ar_ref, ai_ref, wb_ref, wct_ref, hre_ref, him_ref, s_ref, sub=sub, **dims))
        trace_pending_tails()
        pending.append((jnp.concatenate(ys, axis=1) + d_ref[...] * u, sub))
    trace_pending_tails()


def _ssm_positions(ref, nb, steps, sub):
    per_sub = nb * steps // ref.shape[0]
    return slice(sub * per_sub, (sub + 1) * per_sub)


def _ssm_permute_in(u_ref, *, nb, steps, sub):
    rows = nb * steps
    tm_row = lax.broadcasted_iota(jnp.int32, (rows, rows), 0)
    bm_col = lax.broadcasted_iota(jnp.int32, (rows, rows), 1)
    to_time_major = (bm_col == (tm_row % nb) * steps + tm_row // nb).astype(BF16)
    u_bm = u_ref[:, _ssm_positions(u_ref, nb, steps, sub), :].reshape(rows, SSM_WIDTH)
    u_hi = u_bm.astype(BF16)
    u_lo = (u_bm - u_hi.astype(F32)).astype(BF16)
    u_hi_tm = jnp.dot(to_time_major, u_hi, preferred_element_type=F32)
    u = u_hi_tm + jnp.dot(to_time_major, u_lo, preferred_element_type=F32)
    return u, u_hi_tm.astype(BF16)


def _ssm_state_block(ub, j, ar_ref, ai_ref, wb_ref, wct_ref, hre_ref, him_ref, s_ref, *, nb, steps, sub):
    rows = nb * steps
    row0 = sub * rows
    blk_cols = 2 * BLOCK_STATES
    s_ref[row0:row0 + rows, j * blk_cols:(j + 1) * blk_cols] = jnp.dot(
        ub[:, j * MXU_DIM:(j + 1) * MXU_DIM], wb_ref[j], preferred_element_type=F32)
    for part in range(BLOCK_STATES // SCAN_LANES):
        rc = j * blk_cols + part * SCAN_LANES
        ic = rc + BLOCK_STATES
        sc = j * BLOCK_STATES + part * SCAN_LANES
        a_re = ar_ref[:, sc:sc + SCAN_LANES]
        a_im = ai_ref[:, sc:sc + SCAN_LANES]
        for bg in range(nb // SUBLANES):
            b0 = bg * SUBLANES
            h_re = hre_ref[b0:b0 + SUBLANES, sc:sc + SCAN_LANES]
            h_im = him_ref[b0:b0 + SUBLANES, sc:sc + SCAN_LANES]
            for t in range(steps):
                row = row0 + t * nb + b0
                n_re = a_re * h_re - a_im * h_im + s_ref[row:row + SUBLANES, rc:rc + SCAN_LANES]
                n_im = a_re * h_im + a_im * h_re + s_ref[row:row + SUBLANES, ic:ic + SCAN_LANES]
                s_ref[row:row + SUBLANES, rc:rc + SCAN_LANES] = n_re
                s_ref[row:row + SUBLANES, ic:ic + SCAN_LANES] = n_im
                h_re, h_im = n_re, n_im
            hre_ref[b0:b0 + SUBLANES, sc:sc + SCAN_LANES] = h_re
            him_ref[b0:b0 + SUBLANES, sc:sc + SCAN_LANES] = h_im
    hb = s_ref[row0:row0 + rows, j * blk_cols:(j + 1) * blk_cols].astype(BF16)
    return lax.dot_general(hb, wct_ref[j], (((1,), (1,)), ((), ())), preferred_element_type=F32)


def _ssm_gate_and_store(y, wglu_ref, bglu_ref, gs_ref, mix_ref, *, nb, steps, sub):
    rows = nb * steps
    bm_row = lax.broadcasted_iota(jnp.int32, (rows, rows), 0)
    tm_col = lax.broadcasted_iota(jnp.int32, (rows, rows), 1)
    to_batch_major = (bm_row == (tm_col % nb) * steps + tm_col // nb).astype(BF16)
    g = _gelu_tanh(y)
    gate = jnp.dot(g.astype(BF16), wglu_ref[...], preferred_element_type=F32) + bglu_ref[...]
    so = g * jax.nn.sigmoid(gate)
    r = lax.rsqrt(jnp.mean(so * so, axis=-1, keepdims=True) + EPS)
    mix_tm = (so * r * gs_ref[...]).astype(BF16)
    mix_bm = jnp.dot(to_batch_major, mix_tm, preferred_element_type=F32).astype(BF16)
    mix_ref[:, _ssm_positions(mix_ref, nb, steps, sub), :] = mix_bm.reshape(
        mix_ref.shape[0], rows // mix_ref.shape[0], SSM_WIDTH)


def _ssm_mixer(u3, h0_re, h0_im, ar8, ai8, wb, wct, d, wglu_b, bglu, gs, nb, steps, n_sub):
    nbv, seq, _ = u3.shape
    blk_rows = n_sub * nb * steps // nbv
    assert blk_rows % SUBLANES == 0 and seq % blk_rows == 0
    cols = 2 * N_STATE
    tr = n_sub * nb * steps
    return pl.pallas_call(
        functools.partial(_ssm_kernel, nb=nb, steps=steps, n_sub=n_sub),
        grid=(seq // blk_rows,),
        in_specs=[
            pl.BlockSpec((nbv, blk_rows, SSM_WIDTH), lambda i: (0, i, 0)),
            _const_spec((nb, N_STATE)),
            _const_spec((nb, N_STATE)),
            _const_spec((SUBLANES, N_STATE)),
            _const_spec((SUBLANES, N_STATE)),
            _const_spec((N_SSM_BLOCKS, MXU_DIM, 2 * BLOCK_STATES)),
            _const_spec((N_SSM_BLOCKS, MXU_DIM, 2 * BLOCK_STATES)),
            _const_spec((1, SSM_WIDTH)),
            _const_spec((SSM_WIDTH, SSM_WIDTH)),
            _const_spec((1, SSM_WIDTH)),
            _const_spec((1, SSM_WIDTH)),
        ],
        out_specs=[
            pl.BlockSpec((nbv, blk_rows, SSM_WIDTH), lambda i: (0, i, 0)),
            pl.BlockSpec((nb, N_STATE), lambda i: (0, 0)),
            pl.BlockSpec((nb, N_STATE), lambda i: (0, 0)),
        ],
        out_shape=(
            jax.ShapeDtypeStruct((nbv, seq, SSM_WIDTH), BF16),
            jax.ShapeDtypeStruct((nb, N_STATE), F32),
            jax.ShapeDtypeStruct((nb, N_STATE), F32),
        ),
        scratch_shapes=[pltpu.VMEM((tr, cols), F32)],
        compiler_params=_cparams(("arbitrary",)),
        name="ssm_mixer",
    )(u3, h0_re, h0_im, ar8, ai8, wb, wct, d, wglu_b, bglu, gs)


def _out_mlp_kernel(x_ref, a_ref, ga_ref, ms_ref, woa_ref, wos_ref, gm_ref, wup_ref, wdn_ref,
                    y_ref, hn_ref):
    @pl.when(pl.program_id(1) == 0)
    def _():
        a = a_ref[...].astype(F32)
        ra = lax.rsqrt(jnp.mean(a * a, axis=-1, keepdims=True) + EPS)
        ma = (a * ra * ga_ref[...]).astype(BF16)
        h = (x_ref[...]
             + jnp.dot(ma, woa_ref[...], preferred_element_type=F32)
             + jnp.dot(ms_ref[...], wos_ref[...], preferred_element_type=F32))
        y_ref[...] = h
        rh = lax.rsqrt(jnp.mean(h * h, axis=-1, keepdims=True) + EPS)
        hn_ref[...] = (h * rh * gm_ref[...]).astype(BF16)

    t = jnp.dot(hn_ref[...], wup_ref[...], preferred_element_type=F32)
    t = jnp.maximum(t, 0.0)
    t = (t * t).astype(BF16)
    y_ref[...] += jnp.dot(t, wdn_ref[...], preferred_element_type=F32)


def _out_mlp(x2d, attn, ga, mix_s, woa, wos, gm, wup, wdn, tm, tf):
    rows = x2d.shape[0]
    return pl.pallas_call(
        _out_mlp_kernel,
        grid=(rows // tm, D_FF // tf),
        in_specs=[
            pl.BlockSpec((tm, D_MODEL), lambda i, j: (i, 0)),
            pl.BlockSpec((tm, ATTN_WIDTH), lambda i, j: (i, 0)),
            _const_spec((1, ATTN_WIDTH)),
            pl.BlockSpec((tm, SSM_WIDTH), lambda i, j: (i, 0)),
            _const_spec((ATTN_WIDTH, D_MODEL)),
            _const_spec((SSM_WIDTH, D_MODEL)),
            _const_spec((1, D_MODEL)),
            pl.BlockSpec((D_MODEL, tf), lambda i, j: (0, j)),
            pl.BlockSpec((tf, D_MODEL), lambda i, j: (j, 0)),
        ],
        out_specs=pl.BlockSpec((tm, D_MODEL), lambda i, j: (i, 0)),
        out_shape=jax.ShapeDtypeStruct((rows, D_MODEL), F32),
        scratch_shapes=[pltpu.VMEM((tm, D_MODEL), BF16)],
        compiler_params=_cparams(("parallel", "arbitrary")),
        name="out_mlp",
    )(x2d, attn, ga, mix_s, woa, wos, gm, wup, wdn)


def _pair_heads(a, axis):
    shape = a.shape
    split = shape[:axis] + (N_KV_HEADS // 2, 2, Q_PER_KV, HEAD_DIM) + shape[axis + 1:]
    return jnp.swapaxes(a.reshape(split), axis + 1, axis + 2).reshape(shape)


def _layer(x, cache_k, cache_v, h0_re, h0_im, p, mlp_w):
    n, t = x.shape[:2]
    rows = n * t
    tm = 512
    x2d = x.reshape(rows, D_MODEL)
    if mlp_w is None:
        q, kv, u, *mlp_w = _in_proj(x2d, p['gn'], p['wq'], p['wr'], p['gq'], p['gk'], tm,
                                    to_bf16=(p['w_up_f32'], p['w_down_f32']))
    else:
        q, kv, u = _in_proj(x2d, p['gn'], p['wq'], p['wr'], p['gq'], p['gk'], tm)
    kv3 = kv.reshape(n, t, 2 * KV_WIDTH)

    if cache_k is None:
        attn = _prompt_attention(p['sinks'], q, kv, n, t)
        tail = kv3[:, t - WINDOW:]
        new_k = tail[..., :KV_WIDTH].reshape(n, WINDOW, N_KV_HEADS, HEAD_DIM)
        new_v = tail[..., KV_WIDTH:].reshape(n, WINDOW, N_KV_HEADS, HEAD_DIM)
        u3 = u.reshape(n, t, SSM_WIDTH)
        ssm_steps, ssm_subs = SSM_SUB_TILE_ROWS // n, SSM_SUB_TILES
    else:
        nblk = ATTN_WIDTH // LANES
        q3 = q.reshape(n, t, nblk, LANES).transpose(0, 2, 1, 3).reshape(n, nblk * t, LANES)
        kvn = jnp.pad(kv3, ((0, 0), (SUBLANES - t, 0), (0, 0)))
        ckt = cache_k.transpose(0, 2, 3, 1)
        cvt = cache_v.transpose(0, 2, 3, 1)
        a3, nkt, nvt = _sample_attention(p['sinks'], q3, ckt, cvt, kvn, t)
        attn = a3.reshape(n, nblk, t, LANES).transpose(0, 2, 1, 3).reshape(rows, ATTN_WIDTH)
        new_k = nkt.transpose(0, 3, 1, 2)
        new_v = nvt.transpose(0, 3, 1, 2)
        u3 = u.reshape(1, rows, SSM_WIDTH)
        ssm_steps, ssm_subs = t, 1

    mix3, h_re, h_im = _ssm_mixer(u3, h0_re.reshape(n, N_STATE), h0_im.reshape(n, N_STATE), p['ar8'], p['ai8'],
                                  p['wb'], p['wct'], p['d'], p['w_glu'], p['b_glu'], p['gs'], n, ssm_steps, ssm_subs)
    mix_s = mix3.reshape(rows, SSM_WIDTH)
    h_re = h_re.reshape(n, N_SSM_GROUPS, STATE_DIM)
    h_im = h_im.reshape(n, N_SSM_GROUPS, STATE_DIM)

    y = _out_mlp(x2d, attn, p['ga'], mix_s, p['woa'], p['wos'], p['gm'], mlp_w[0], mlp_w[1], tm, MLP_FF_TILE)
    return y.reshape(n, t, D_MODEL), new_k, new_v, h_re, h_im, mlp_w


def _prepare_params(l, attn_norm_g, w_in, q_norm_g, k_norm_g, attn_sinks,
                    ssm_A_re, ssm_A_im, ssm_log_dt, ssm_B_re, ssm_B_im, ssm_C_re, ssm_C_im, ssm_D,
                    w_glu, b_glu, attn_out_g, ssm_out_g, w_out, mlp_norm_g, w_mlp_up, w_mlp_down):
    w_in_l = w_in[l]
    heads_per_blk = MXU_DIM // HEAD_DIM

    a_re = ssm_A_re[l].reshape(1, N_STATE)
    a_im = ssm_A_im[l].reshape(1, N_STATE)
    ldt = jnp.broadcast_to(ssm_log_dt[l][:, None], (N_SSM_GROUPS, STATE_DIM)).reshape(1, N_STATE)
    b_re = ssm_B_re[l].transpose(2, 0, 1).reshape(SSM_GROUP, N_STATE)
    b_im = ssm_B_im[l].transpose(2, 0, 1).reshape(SSM_GROUP, N_STATE)
    c_re = ssm_C_re[l].transpose(1, 0, 2).reshape(SSM_GROUP, N_STATE)
    c_im = ssm_C_im[l].transpose(1, 0, 2).reshape(SSM_GROUP, N_STATE)
    ar8, ai8, wb, wct = _ssm_discretize(a_re, a_im, ldt, b_re, b_im, c_re, c_im)

    w_out_l = w_out[l]
    return dict(
        gn=attn_norm_g[l].reshape(1, D_MODEL),
        wq=_pair_heads(w_in_l[:, :ATTN_WIDTH], 1).astype(BF16),
        wr=w_in_l[:, ATTN_WIDTH:].astype(BF16),
        gq=jnp.tile(q_norm_g[l], heads_per_blk).reshape(1, MXU_DIM),
        gk=jnp.tile(k_norm_g[l], heads_per_blk).reshape(1, MXU_DIM),
        sinks=attn_sinks[l].astype(F32),
        ar8=ar8, ai8=ai8, wb=wb, wct=wct,
        d=ssm_D[l].reshape(1, SSM_WIDTH),
        w_glu=w_glu[l].astype(BF16),
        b_glu=b_glu[l].reshape(1, SSM_WIDTH),
        gs=ssm_out_g[l].reshape(1, SSM_WIDTH),
        ga=_pair_heads(attn_out_g[l], 0).reshape(1, ATTN_WIDTH),
        woa=_pair_heads(w_out_l[:ATTN_WIDTH], 0).astype(BF16),
        wos=w_out_l[ATTN_WIDTH:].astype(BF16),
        gm=mlp_norm_g[l].reshape(1, D_MODEL),
        w_up_f32=w_mlp_up[l],
        w_down_f32=w_mlp_down[l],
    )


def kernel(x_prompt, x_sample, cache_k, cache_v, state_ssm_re, state_ssm_im, attn_norm_g, w_in, q_norm_g, k_norm_g, attn_sinks, ssm_A_re, ssm_A_im, ssm_log_dt, ssm_B_re, ssm_B_im, ssm_C_re, ssm_C_im, ssm_D, w_glu, b_glu, attn_out_g, ssm_out_g, w_out, mlp_norm_g, w_mlp_up, w_mlp_down):
    depth = w_in.shape[0]
    xp, xs = x_prompt, x_sample
    zeros_state = jnp.zeros((x_prompt.shape[0], N_SSM_GROUPS, STATE_DIM), F32)
    outs = [[] for _ in range(8)]
    for l in range(depth):
        p = _prepare_params(l, attn_norm_g, w_in, q_norm_g, k_norm_g, attn_sinks,
                            ssm_A_re, ssm_A_im, ssm_log_dt, ssm_B_re, ssm_B_im, ssm_C_re, ssm_C_im, ssm_D,
                            w_glu, b_glu, attn_out_g, ssm_out_g, w_out, mlp_norm_g, w_mlp_up, w_mlp_down)
        xp, kp, vp, hrp, hip, mlp_w = _layer(xp, None, None, zeros_state, zeros_state, p, None)
        xs, ks, vs, hrs, his, _ = _layer(xs, cache_k[l], cache_v[l], state_ssm_re[l], state_ssm_im[l], p, mlp_w)
        for lst, val in zip(outs, (kp, vp, hrp, hip, ks, vs, hrs, his)):
            lst.append(val)
    return (xp, xs) + tuple(jnp.stack(o) for o in outs)
```

```python
import functools
import math

import jax
import jax.numpy as jnp
from jax import lax
from jax.experimental import pallas as pl
from jax.experimental.pallas import tpu as pltpu

D_MODEL = 2048
ATTN_WIDTH = 1024
SSM_WIDTH = 1024
HEAD_DIM = 64
N_HEADS = 16
N_KV_HEADS = 4
Q_PER_KV = 4
KV_WIDTH = 256
WINDOW = 128
SSM_GROUP = 16
N_SSM_GROUPS = 64
STATE_DIM = 64
N_STATE = N_SSM_GROUPS * STATE_DIM
D_FF = 8192
PROJ_WIDTH = ATTN_WIDTH + 2 * KV_WIDTH + SSM_WIDTH
EPS = 1e-6
NEG_INF = -1e30

LANES = 128
SUBLANES = 8
MXU_DIM = 256
VMEM_LIMIT = 56 * 1024 * 1024

GROUPS_PER_BLOCK = MXU_DIM // SSM_GROUP
N_SSM_BLOCKS = N_SSM_GROUPS // GROUPS_PER_BLOCK
BLOCK_STATES = GROUPS_PER_BLOCK * STATE_DIM
SCAN_LANES = 512
SSM_SUB_TILE_ROWS = 256
SSM_SUB_TILES = 2
SSM_TAIL_LAG_BLOCKS = 4
MLP_FF_TILE = 1024

F32 = jnp.float32
BF16 = jnp.bfloat16


def _cparams(sem):
    return pltpu.CompilerParams(dimension_semantics=sem, vmem_limit_bytes=VMEM_LIMIT)


def _const_spec(shape):
    nd = len(shape)
    return pl.BlockSpec(shape, lambda *_: (0,) * nd, pipeline_mode=pl.Buffered(1))


def _discretize_kernel(are_ref, aim_ref, ldt_ref, bre_ref, bim_ref, cre_ref, cim_ref,
                       abr_ref, abi_ref, wb_ref, wct_ref):
    a_re = are_ref[...]
    a_im = aim_ref[...]
    dt = jnp.exp(ldt_ref[...])
    mag = jnp.exp(a_re * dt)
    ab_re = mag * jnp.cos(a_im * dt)
    ab_im = mag * jnp.sin(a_im * dt)
    abr_ref[...] = jnp.broadcast_to(ab_re, abr_ref.shape)
    abi_ref[...] = jnp.broadcast_to(ab_im, abi_ref.shape)
    x = ab_re - 1.0
    y = ab_im
    den = a_re * a_re + a_im * a_im
    k_re = (x * a_re + y * a_im) / den
    k_im = (y * a_re - x * a_im) / den
    b_re = bre_ref[...]
    b_im = bim_ref[...]
    bb_re = k_re * b_re - k_im * b_im
    bb_im = k_re * b_im + k_im * b_re

    rows = GROUPS_PER_BLOCK * SSM_GROUP
    row_group = lax.broadcasted_iota(jnp.int32, (rows, BLOCK_STATES), 0) // SSM_GROUP
    col_group = lax.broadcasted_iota(jnp.int32, (rows, BLOCK_STATES), 1) // STATE_DIM
    same_group = row_group == col_group

    def block_diag(m, j):
        blk = m[:, j * BLOCK_STATES:(j + 1) * BLOCK_STATES]
        return jnp.where(same_group, jnp.concatenate([blk] * GROUPS_PER_BLOCK, axis=0), 0.0).astype(BF16)

    c_re = cre_ref[...]
    c_im_neg = -cim_ref[...]
    for j in range(N_SSM_BLOCKS):
        wb_ref[j, :, :BLOCK_STATES] = block_diag(bb_re, j)
        wb_ref[j, :, BLOCK_STATES:] = block_diag(bb_im, j)
        wct_ref[j, :, :BLOCK_STATES] = block_diag(c_re, j)
        wct_ref[j, :, BLOCK_STATES:] = block_diag(c_im_neg, j)


def _ssm_discretize(a_re, a_im, log_dt, b_re, b_im, c_re, c_im):
    rep = jax.ShapeDtypeStruct((SUBLANES, N_STATE), F32)
    blocks = jax.ShapeDtypeStruct((N_SSM_BLOCKS, GROUPS_PER_BLOCK * SSM_GROUP, 2 * BLOCK_STATES), BF16)
    return pl.pallas_call(
        _discretize_kernel,
        out_shape=(rep, rep, blocks, blocks),
        name="ssm_discretize",
    )(a_re, a_im, log_dt, b_re, b_im, c_re, c_im)


def _head_rmsnorm(zc, gain, ones_blk):
    sq = (zc * zc).astype(BF16)
    ss = jnp.dot(sq, ones_blk, preferred_element_type=F32)
    return zc * lax.rsqrt(ss * (1.0 / HEAD_DIM) + EPS) * gain


def _in_proj_kernel(x_ref, gn_ref, w_ref, gq_ref, gk_ref, *rest):
    n_cast = (len(rest) - 3) // 2
    cast_in = rest[:n_cast]
    q_ref, kv_ref, u_ref = rest[n_cast:n_cast + 3]
    cast_out = rest[n_cast + 3:]
    for src, dst in zip(cast_in, cast_out):
        dst[...] = src[...].astype(BF16)
    x = x_ref[...]
    r = lax.rsqrt(jnp.mean(x * x, axis=-1, keepdims=True) + EPS)
    xn = (x * r * gn_ref[...]).astype(BF16)
    zq = jnp.dot(xn, w_ref[:, :ATTN_WIDTH], preferred_element_type=F32)
    zr = jnp.dot(xn, w_ref[:, ATTN_WIDTH:], preferred_element_type=F32)
    ri = lax.broadcasted_iota(jnp.int32, (MXU_DIM, MXU_DIM), 0) // HEAD_DIM
    ci = lax.broadcasted_iota(jnp.int32, (MXU_DIM, MXU_DIM), 1) // HEAD_DIM
    ones_blk = (ri == ci).astype(BF16)
    gq = gq_ref[...]
    lane_blocks = []
    for c in range(ATTN_WIDTH // MXU_DIM):
        qn = _head_rmsnorm(zq[:, c * MXU_DIM:(c + 1) * MXU_DIM], gq, ones_blk) * (HEAD_DIM ** -0.5)
        lane_blocks += [qn[:, :LANES], qn[:, LANES:]]
    low = lax.broadcasted_iota(jnp.int32, (x.shape[0], LANES), 1) < HEAD_DIM
    for pair in range(N_KV_HEADS // 2):
        for r in range(Q_PER_KV):
            first = lane_blocks[((2 * pair) * Q_PER_KV + r) // 2]
            second = lane_blocks[((2 * pair + 1) * Q_PER_KV + r) // 2]
            if r % 2 == 0:
                blk = jnp.where(low, first, pltpu.roll(second, HEAD_DIM, axis=1))
            else:
                blk = jnp.where(low, pltpu.roll(first, HEAD_DIM, axis=1), second)
            dst = pair * Q_PER_KV + r
            q_ref[:, dst * LANES:(dst + 1) * LANES] = blk.astype(BF16)
    kv_ref[:, :KV_WIDTH] = _head_rmsnorm(zr[:, :KV_WIDTH], gk_ref[...], ones_blk)
    kv_ref[:, KV_WIDTH:] = zr[:, KV_WIDTH:2 * KV_WIDTH]
    u_ref[...] = zr[:, 2 * KV_WIDTH:]


def _in_proj(x2d, gn, w_in_b, gq, gk, tm, to_bf16=()):
    rows = x2d.shape[0]
    steps = rows // tm
    slab_specs = [pl.BlockSpec((w.shape[0] // steps, w.shape[1]), lambda i: (i, 0)) for w in to_bf16]
    assert all(w.shape[0] % (steps * 2 * SUBLANES) == 0 for w in to_bf16)
    return pl.pallas_call(
        _in_proj_kernel,
        grid=(steps,),
        in_specs=[
            pl.BlockSpec((tm, D_MODEL), lambda i: (i, 0)),
            _const_spec((1, D_MODEL)),
            _const_spec((D_MODEL, PROJ_WIDTH)),
            _const_spec((1, MXU_DIM)),
            _const_spec((1, MXU_DIM)),
        ] + slab_specs,
        out_specs=[
            pl.BlockSpec((tm, ATTN_WIDTH), lambda i: (i, 0)),
            pl.BlockSpec((tm, 2 * KV_WIDTH), lambda i: (i, 0)),
            pl.BlockSpec((tm, SSM_WIDTH), lambda i: (i, 0)),
        ] + slab_specs,
        out_shape=(
            jax.ShapeDtypeStruct((rows, ATTN_WIDTH), BF16),
            jax.ShapeDtypeStruct((rows, 2 * KV_WIDTH), F32),
            jax.ShapeDtypeStruct((rows, SSM_WIDTH), F32),
        ) + tuple(jax.ShapeDtypeStruct(w.shape, BF16) for w in to_bf16),
        compiler_params=_cparams(("arbitrary",)),
        name="in_proj",
    )(x2d, gn, w_in_b, gq, gk, *to_bf16)


def _alibi_slope(head):
    return 2.0 ** (-8.0 * (head + 1) / N_HEADS)


ATTN_BLOCKS_PER_STEP = 4


def _prompt_attn_kernel(sink_ref, q_ref, kvp_ref, kvc_ref, bias_ref, o_ref):
    tq = WINDOW
    lane = lax.broadcasted_iota(jnp.int32, (tq, LANES), 1)
    low = lane < HEAD_DIM
    zero = jnp.zeros((tq, LANES), BF16)

    def keys_or_values(blk, lanes):
        own = kvc_ref[blk * tq:(blk + 1) * tq, lanes]
        prev = kvp_ref[:, lanes] if blk == 0 else kvc_ref[(blk - 1) * tq:blk * tq, lanes]
        return jnp.concatenate([prev, own], axis=0).astype(BF16)

    for blk in range(q_ref.shape[0] // tq):
        rows = slice(blk * tq, (blk + 1) * tq)
        has_prev = (pl.program_id(1) > 0).astype(jnp.int32) if blk == 0 else 1
        for pair in range(N_KV_HEADS // 2):
            kc = keys_or_values(blk, slice(pair * LANES, (pair + 1) * LANES))
            vc = keys_or_values(blk, slice(KV_WIDTH + pair * LANES, KV_WIDTH + (pair + 1) * LANES))
            qb = [q_ref[rows, (pair * Q_PER_KV + r) * LANES:(pair * Q_PER_KV + r + 1) * LANES]
                  for r in range(Q_PER_KV)]
            qs = jnp.concatenate([jnp.where(low, b, zero) for b in qb] + [jnp.where(low, zero, b) for b in qb],
                                 axis=0)
            scores = lax.dot_general(qs, kc, (((1,), (1,)), ((), ())), preferred_element_type=F32)
            probs = []
            inv_den = []
            for hh in range(2 * Q_PER_KV):
                head = (2 * pair + hh // Q_PER_KV) * Q_PER_KV + hh % Q_PER_KV
                s = scores[hh * tq:(hh + 1) * tq] + bias_ref[has_prev, head]
                sink = sink_ref[head]
                m = jnp.maximum(jnp.max(s, axis=-1, keepdims=True), sink)
                p = jnp.exp(s - m)
                den = jnp.sum(p, axis=-1, keepdims=True) + jnp.exp(sink - m)
                probs.append(p.astype(BF16))
                inv_den.append(1.0 / den)
            pv = jnp.dot(jnp.concatenate(probs, axis=0), vc, preferred_element_type=F32)
            for r in range(Q_PER_KV):
                o_low = pv[r * tq:(r + 1) * tq] * inv_den[r]
                o_high = pv[(Q_PER_KV + r) * tq:(Q_PER_KV + r + 1) * tq] * inv_den[Q_PER_KV + r]
                o_ref[rows, (pair * Q_PER_KV + r) * LANES:(pair * Q_PER_KV + r + 1) * LANES] = (
                    jnp.where(low, o_low, o_high).astype(BF16))


def _prompt_attn_bias():
    w = WINDOW
    slope = jnp.exp2(-8.0 * (jnp.arange(N_HEADS, dtype=F32) + 1.0) / N_HEADS)
    i = jnp.arange(w)[:, None]
    j = jnp.arange(2 * w)[None, :]
    d = i + w - j
    ok = (d >= 0) & (d <= w)
    ok = jnp.stack([ok & (j >= w), ok])
    return jnp.where(ok[:, None], -slope[None, :, None, None] * d.astype(F32)[None, None], NEG_INF)


def _prompt_attention(sinks, q, kv, batch, seq):
    per = ATTN_BLOCKS_PER_STEP
    tq = per * WINDOW
    nb = seq // tq
    bias = _prompt_attn_bias()
    return pl.pallas_call(
        _prompt_attn_kernel,
        grid=(batch, nb),
        in_specs=[
            pl.BlockSpec(memory_space=pltpu.SMEM),
            pl.BlockSpec((tq, ATTN_WIDTH), lambda b, n: (b * nb + n, 0)),
            pl.BlockSpec((WINDOW, 2 * KV_WIDTH), lambda b, n: (b * nb * per + jnp.maximum(n * per - 1, 0), 0)),
            pl.BlockSpec((tq, 2 * KV_WIDTH), lambda b, n: (b * nb + n, 0)),
            _const_spec(bias.shape),
        ],
        out_specs=pl.BlockSpec((tq, ATTN_WIDTH), lambda b, n: (b * nb + n, 0)),
        out_shape=jax.ShapeDtypeStruct((batch * seq, ATTN_WIDTH), BF16),
        compiler_params=_cparams(("parallel", "parallel")),
        name="prompt_attention",
    )(sinks, q, kv, kv, bias)


SAMPLE_ATTN_BATCH = 16
SAMPLE_ATTN_GROUP = 2


def _sample_attn_kernel(sink_ref, q_ref, ckt_ref, cvt_ref, kvn_ref, o_ref, nkt_ref, nvt_ref, *, t_new):
    rows = Q_PER_KV * t_new
    tk = 2 * WINDOW
    i = lax.broadcasted_iota(jnp.int32, (rows, tk), 0) % t_new
    c = lax.broadcasted_iota(jnp.int32, (rows, tk), 1)
    is_new = c >= tk - t_new
    j = jnp.where(c < WINDOW, c, c - (WINDOW - t_new))
    d = i + WINDOW - j
    valid = (d >= 0) & (d <= WINDOW) & ((c < WINDOW) | is_new)
    delta = d.astype(F32)
    lane_q = lax.broadcasted_iota(jnp.int32, (rows, LANES), 1)
    low = lane_q < HEAD_DIM
    zero_q = jnp.zeros((rows, LANES), BF16)
    lane_w = lax.broadcasted_iota(jnp.int32, (WINDOW, LANES), 1)
    tail = lane_w >= WINDOW - t_new
    rid = lax.broadcasted_iota(jnp.int32, (rows, 1), 0) // t_new
    alibi, sinks = [], []
    for pair in range(N_KV_HEADS // 2):
        sl, sk = [], []
        for half in range(2):
            slope = jnp.zeros((rows, 1), F32)
            sink = jnp.zeros((rows, 1), F32)
            for r in range(Q_PER_KV):
                head = (2 * pair + half) * Q_PER_KV + r
                slope = jnp.where(rid == r, _alibi_slope(head), slope)
                sink = jnp.where(rid == r, sink_ref[head], sink)
            sl.append(slope)
            sk.append(sink)
        alibi.append(jnp.concatenate(sl, axis=0) * jnp.concatenate([delta, delta], axis=0))
        sinks.append(jnp.concatenate(sk, axis=0))
    valid2 = jnp.concatenate([valid, valid], axis=0)

    lead_zero = jnp.zeros((WINDOW - SUBLANES, LANES), F32)

    def new_rows_transposed(e, lanes):
        return jnp.concatenate([lead_zero, kvn_ref[e, :, lanes]], axis=0).T

    def shifted(old_t, new_t):
        return jnp.where(tail, new_t, pltpu.roll(old_t, WINDOW - t_new, axis=1))

    def keys_and_scores(e, pair):
        heads = slice(2 * pair, 2 * pair + 2)
        kt = ckt_ref[e, heads].reshape(2 * HEAD_DIM, WINDOW)
        knt = new_rows_transposed(e, slice(pair * LANES, (pair + 1) * LANES))
        nkt_ref[e, heads] = shifted(kt, knt).reshape(2, HEAD_DIM, WINDOW)
        qp = q_ref[e, pair * rows:(pair + 1) * rows, :]
        qs = jnp.concatenate([jnp.where(low, qp, zero_q), jnp.where(low, zero_q, qp)], axis=0)
        kt_all = jnp.concatenate([kt, knt], axis=1).astype(BF16)
        return jnp.dot(qs, kt_all, preferred_element_type=F32)

    def values(e, pair):
        heads = slice(2 * pair, 2 * pair + 2)
        vt = cvt_ref[e, heads].reshape(2 * HEAD_DIM, WINDOW)
        vnt = new_rows_transposed(e, slice(KV_WIDTH + pair * LANES, KV_WIDTH + (pair + 1) * LANES))
        nvt_ref[e, heads] = shifted(vt, vnt).reshape(2, HEAD_DIM, WINDOW)
        return jnp.concatenate([vt, vnt], axis=1).astype(BF16)

    def softmax(pair, scores):
        s = jnp.where(valid2, scores - alibi[pair], NEG_INF)
        m = jnp.maximum(jnp.max(s, axis=-1, keepdims=True), sinks[pair])
        p = jnp.exp(s - m)
        den = jnp.sum(p, axis=-1, keepdims=True) + jnp.exp(sinks[pair] - m)
        return p.astype(BF16), 1.0 / den

    def body(i, carry):
        group = [(i * SAMPLE_ATTN_GROUP + k, pair) for k in range(SAMPLE_ATTN_GROUP) for pair in range(N_KV_HEADS // 2)]
        scores = [keys_and_scores(e, pair) for e, pair in group]
        vt_all = [values(e, pair) for e, pair in group]
        probs = [softmax(pair, sc) for (e, pair), sc in zip(group, scores)]
        for (e, pair), (p, inv_den), vt in zip(group, probs, vt_all):
            pv = lax.dot_general(p, vt, (((1,), (1,)), ((), ())), preferred_element_type=F32)
            pv = pv * inv_den
            o_ref[e, pair * rows:(pair + 1) * rows, :] = jnp.where(low, pv[:rows], pv[rows:]).astype(BF16)
        return carry

    lax.fori_loop(0, q_ref.shape[0] // SAMPLE_ATTN_GROUP, body, 0)


def _sample_attention(sinks, q3, ckt, cvt, kvn, t_new):
    n = q3.shape[0]
    bn = SAMPLE_ATTN_BATCH
    qrows = q3.shape[1]
    cache_spec = pl.BlockSpec((bn, N_KV_HEADS, HEAD_DIM, WINDOW), lambda b: (b, 0, 0, 0))
    cache_shape = jax.ShapeDtypeStruct((n, N_KV_HEADS, HEAD_DIM, WINDOW), F32)
    return pl.pallas_call(
        functools.partial(_sample_attn_kernel, t_new=t_new),
        grid=(n // bn,),
        in_specs=[
            pl.BlockSpec(memory_space=pltpu.SMEM),
            pl.BlockSpec((bn, qrows, LANES), lambda b: (b, 0, 0)),
            cache_spec,
            cache_spec,
            pl.BlockSpec((bn, SUBLANES, 2 * KV_WIDTH), lambda b: (b, 0, 0)),
        ],
        out_specs=[pl.BlockSpec((bn, qrows, LANES), lambda b: (b, 0, 0)), cache_spec, cache_spec],
        out_shape=(jax.ShapeDtypeStruct((n, qrows, LANES), BF16), cache_shape, cache_shape),
        compiler_params=_cparams(("parallel",)),
        name="sample_attention",
    )(sinks, q3, ckt, cvt, kvn)


def _gelu_tanh(x):
    c = math.sqrt(2.0 / math.pi)
    return 0.5 * x * (1.0 + jnp.tanh(c * (x + 0.044715 * (x * x * x))))


def _ssm_kernel(u_ref, h0re_ref, h0im_ref, ar_ref, ai_ref, wb_ref, wct_ref, d_ref, wglu_ref, bglu_ref, gs_ref,
                mix_ref, hre_ref, him_ref, s_ref, *, nb, steps, n_sub):
    @pl.when(pl.program_id(0) == 0)
    def _():
        hre_ref[...] = h0re_ref[...]
        him_ref[...] = h0im_ref[...]

    dims = dict(nb=nb, steps=steps)
    pending = []

    def trace_pending_tails():
        while pending:
            y, s = pending.pop(0)
            _ssm_gate_and_store(y, wglu_ref, bglu_ref, gs_ref, mix_ref, sub=s, **dims)

    for sub in range(n_sub):
        u, ub = _ssm_permute_in(u_ref, sub=sub, **dims)
        ys = []
        for j in range(N_SSM_BLOCKS):
            if j == SSM_TAIL_LAG_BLOCKS:
                trace_pending_tails()
            ys.append(_ssm_state_block(ub, j, ar_ref, ai_ref, wb_ref, wct_ref, hre_ref, him_ref, s_ref, sub=sub, **dims))
        trace_pending_tails()
        pending.append((jnp.concatenate(ys, axis=1) + d_ref[...] * u, sub))
    trace_pending_tails()


def _ssm_positions(ref, nb, steps, sub):
    per_sub = nb * steps // ref.shape[0]
    return slice(sub * per_sub, (sub + 1) * per_sub)


def _ssm_permute_in(u_ref, *, nb, steps, sub):
    rows = nb * steps
    tm_row = lax.broadcasted_iota(jnp.int32, (rows, rows), 0)
    bm_col = lax.broadcasted_iota(jnp.int32, (rows, rows), 1)
    to_time_major = (bm_col == (tm_row % nb) * steps + tm_row // nb).astype(BF16)
    u_bm = u_ref[:, _ssm_positions(u_ref, nb, steps, sub), :].reshape(rows, SSM_WIDTH)
    u_hi = u_bm.astype(BF16)
    u_lo = (u_bm - u_hi.astype(F32)).astype(BF16)
    u_hi_tm = jnp.dot(to_time_major, u_hi, preferred_element_type=F32)
    u = u_hi_tm + jnp.dot(to_time_major, u_lo, preferred_element_type=F32)
    return u, u_hi_tm.astype(BF16)


def _ssm_state_block(ub, j, ar_ref, ai_ref, wb_ref, wct_ref, hre_ref, him_ref, s_ref, *, nb, steps, sub):
    rows = nb * steps
    row0 = sub * rows
    blk_cols = 2 * BLOCK_STATES
    s_ref[row0:row0 + rows, j * blk_cols:(j + 1) * blk_cols] = jnp.dot(
        ub[:, j * MXU_DIM:(j + 1) * MXU_DIM], wb_ref[j], preferred_element_type=F32)
    for part in range(BLOCK_STATES // SCAN_LANES):
        rc = j * blk_cols + part * SCAN_LANES
        ic = rc + BLOCK_STATES
        sc = j * BLOCK_STATES + part * SCAN_LANES
        a_re = ar_ref[:, sc:sc + SCAN_LANES]
        a_im = ai_ref[:, sc:sc + SCAN_LANES]
        for bg in range(nb // SUBLANES):
            b0 = bg * SUBLANES
            h_re = hre_ref[b0:b0 + SUBLANES, sc:sc + SCAN_LANES]
            h_im = him_ref[b0:b0 + SUBLANES, sc:sc + SCAN_LANES]
            for t in range(steps):
                row = row0 + t * nb + b0
                n_re = a_re * h_re - a_im * h_im + s_ref[row:row + SUBLANES, rc:rc + SCAN_LANES]
                n_im = a_re * h_im + a_im * h_re + s_ref[row:row + SUBLANES, ic:ic + SCAN_LANES]
                s_ref[row:row + SUBLANES, rc:rc + SCAN_LANES] = n_re
                s_ref[row:row + SUBLANES, ic:ic + SCAN_LANES] = n_im
                h_re, h_im = n_re, n_im
            hre_ref[b0:b0 + SUBLANES, sc:sc + SCAN_LANES] = h_re
            him_ref[b0:b0 + SUBLANES, sc:sc + SCAN_LANES] = h_im
    hb = s_ref[row0:row0 + rows, j * blk_cols:(j + 1) * blk_cols].astype(BF16)
    return lax.dot_general(hb, wct_ref[j], (((1,), (1,)), ((), ())), preferred_element_type=F32)


def _ssm_gate_and_store(y, wglu_ref, bglu_ref, gs_ref, mix_ref, *, nb, steps, sub):
    rows = nb * steps
    bm_row = lax.broadcasted_iota(jnp.int32, (rows, rows), 0)
    tm_col = lax.broadcasted_iota(jnp.int32, (rows, rows), 1)
    to_batch_major = (bm_row == (tm_col % nb) * steps + tm_col // nb).astype(BF16)
    g = _gelu_tanh(y)
    gate = jnp.dot(g.astype(BF16), wglu_ref[...], preferred_element_type=F32) + bglu_ref[...]
    so = g * jax.nn.sigmoid(gate)
    r = lax.rsqrt(jnp.mean(so * so, axis=-1, keepdims=True) + EPS)
    mix_tm = (so * r * gs_ref[...]).astype(BF16)
    mix_bm = jnp.dot(to_batch_major, mix_tm, preferred_element_type=F32).astype(BF16)
    mix_ref[:, _ssm_positions(mix_ref, nb, steps, sub), :] = mix_bm.reshape(
        mix_ref.shape[0], rows // mix_ref.shape[0], SSM_WIDTH)


def _ssm_mixer(u3, h0_re, h0_im, ar8, ai8, wb, wct, d, wglu_b, bglu, gs, nb, steps, n_sub):
    nbv, seq, _ = u3.shape
    blk_rows = n_sub * nb * steps // nbv
    assert blk_rows % SUBLANES == 0 and seq % blk_rows == 0
    cols = 2 * N_STATE
    tr = n_sub * nb * steps
    return pl.pallas_call(
        functools.partial(_ssm_kernel, nb=nb, steps=steps, n_sub=n_sub),
        grid=(seq // blk_rows,),
        in_specs=[
            pl.BlockSpec((nbv, blk_rows, SSM_WIDTH), lambda i: (0, i, 0)),
            _const_spec((nb, N_STATE)),
            _const_spec((nb, N_STATE)),
            _const_spec((SUBLANES, N_STATE)),
            _const_spec((SUBLANES, N_STATE)),
            _const_spec((N_SSM_BLOCKS, MXU_DIM, 2 * BLOCK_STATES)),
            _const_spec((N_SSM_BLOCKS, MXU_DIM, 2 * BLOCK_STATES)),
            _const_spec((1, SSM_WIDTH)),
            _const_spec((SSM_WIDTH, SSM_WIDTH)),
            _const_spec((1, SSM_WIDTH)),
            _const_spec((1, SSM_WIDTH)),
        ],
        out_specs=[
            pl.BlockSpec((nbv, blk_rows, SSM_WIDTH), lambda i: (0, i, 0)),
            pl.BlockSpec((nb, N_STATE), lambda i: (0, 0)),
            pl.BlockSpec((nb, N_STATE), lambda i: (0, 0)),
        ],
        out_shape=(
            jax.ShapeDtypeStruct((nbv, seq, SSM_WIDTH), BF16),
            jax.ShapeDtypeStruct((nb, N_STATE), F32),
            jax.ShapeDtypeStruct((nb, N_STATE), F32),
        ),
        scratch_shapes=[pltpu.VMEM((tr, cols), F32)],
        compiler_params=_cparams(("arbitrary",)),
        name="ssm_mixer",
    )(u3, h0_re, h0_im, ar8, ai8, wb, wct, d, wglu_b, bglu, gs)


def _out_mlp_kernel(x_ref, a_ref, ga_ref, ms_ref, woa_ref, wos_ref, gm_ref, wup_ref, wdn_ref,
                    y_ref, hn_ref):
    @pl.when(pl.program_id(1) == 0)
    def _():
        a = a_ref[...].astype(F32)
        ra = lax.rsqrt(jnp.mean(a * a, axis=-1, keepdims=True) + EPS)
        ma = (a * ra * ga_ref[...]).astype(BF16)
        h = (x_ref[...]
             + jnp.dot(ma, woa_ref[...], preferred_element_type=F32)
             + jnp.dot(ms_ref[...], wos_ref[...], preferred_element_type=F32))
        y_ref[...] = h
        rh = lax.rsqrt(jnp.mean(h * h, axis=-1, keepdims=True) + EPS)
        hn_ref[...] = (h * rh * gm_ref[...]).astype(BF16)

    t = jnp.dot(hn_ref[...], wup_ref[...], preferred_element_type=F32)
    t = jnp.maximum(t, 0.0)
    t = (t * t).astype(BF16)
    y_ref[...] += jnp.dot(t, wdn_ref[...], preferred_element_type=F32)


def _out_mlp(x2d, attn, ga, mix_s, woa, wos, gm, wup, wdn, tm, tf):
    rows = x2d.shape[0]
    return pl.pallas_call(
        _out_mlp_kernel,
        grid=(rows // tm, D_FF // tf),
        in_specs=[
            pl.BlockSpec((tm, D_MODEL), lambda i, j: (i, 0)),
            pl.BlockSpec((tm, ATTN_WIDTH), lambda i, j: (i, 0)),
            _const_spec((1, ATTN_WIDTH)),
            pl.BlockSpec((tm, SSM_WIDTH), lambda i, j: (i, 0)),
            _const_spec((ATTN_WIDTH, D_MODEL)),
            _const_spec((SSM_WIDTH, D_MODEL)),
            _const_spec((1, D_MODEL)),
            pl.BlockSpec((D_MODEL, tf), lambda i, j: (0, j)),
            pl.BlockSpec((tf, D_MODEL), lambda i, j: (j, 0)),
        ],
        out_specs=pl.BlockSpec((tm, D_MODEL), lambda i, j: (i, 0)),
        out_shape=jax.ShapeDtypeStruct((rows, D_MODEL), F32),
        scratch_shapes=[pltpu.VMEM((tm, D_MODEL), BF16)],
        compiler_params=_cparams(("parallel", "arbitrary")),
        name="out_mlp",
    )(x2d, attn, ga, mix_s, woa, wos, gm, wup, wdn)


def _pair_heads(a, axis):
    shape = a.shape
    split = shape[:axis] + (N_KV_HEADS // 2, 2, Q_PER_KV, HEAD_DIM) + shape[axis + 1:]
    return jnp.swapaxes(a.reshape(split), axis + 1, axis + 2).reshape(shape)


def _layer(x, cache_k, cache_v, h0_re, h0_im, p, mlp_w):
    n, t = x.shape[:2]
    rows = n * t
    tm = 512
    x2d = x.reshape(rows, D_MODEL)
    if mlp_w is None:
        q, kv, u, *mlp_w = _in_proj(x2d, p['gn'], p['w_in'], p['gq'], p['gk'], tm,
                                    to_bf16=(p['w_up_f32'], p['w_down_f32']))
    else:
        q, kv, u = _in_proj(x2d, p['gn'], p['w_in'], p['gq'], p['gk'], tm)
    kv3 = kv.reshape(n, t, 2 * KV_WIDTH)

    if cache_k is None:
        attn = _prompt_attention(p['sinks'], q, kv, n, t)
        tail = kv3[:, t - WINDOW:]
        new_k = tail[..., :KV_WIDTH].reshape(n, WINDOW, N_KV_HEADS, HEAD_DIM)
        new_v = tail[..., KV_WIDTH:].reshape(n, WINDOW, N_KV_HEADS, HEAD_DIM)
        u3 = u.reshape(n, t, SSM_WIDTH)
        ssm_steps, ssm_subs = SSM_SUB_TILE_ROWS // n, SSM_SUB_TILES
    else:
        nblk = ATTN_WIDTH // LANES
        q3 = q.reshape(n, t, nblk, LANES).transpose(0, 2, 1, 3).reshape(n, nblk * t, LANES)
        kvn = jnp.pad(kv3, ((0, 0), (SUBLANES - t, 0), (0, 0)))
        ckt = cache_k.transpose(0, 2, 3, 1)
        cvt = cache_v.transpose(0, 2, 3, 1)
        a3, nkt, nvt = _sample_attention(p['sinks'], q3, ckt, cvt, kvn, t)
        attn = a3.reshape(n, nblk, t, LANES).transpose(0, 2, 1, 3).reshape(rows, ATTN_WIDTH)
        new_k = nkt.transpose(0, 3, 1, 2)
        new_v = nvt.transpose(0, 3, 1, 2)
        u3 = u.reshape(1, rows, SSM_WIDTH)
        ssm_steps, ssm_subs = t, 1

    mix3, h_re, h_im = _ssm_mixer(u3, h0_re.reshape(n, N_STATE), h0_im.reshape(n, N_STATE), p['ar8'], p['ai8'],
                                  p['wb'], p['wct'], p['d'], p['w_glu'], p['b_glu'], p['gs'], n, ssm_steps, ssm_subs)
    mix_s = mix3.reshape(rows, SSM_WIDTH)
    h_re = h_re.reshape(n, N_SSM_GROUPS, STATE_DIM)
    h_im = h_im.reshape(n, N_SSM_GROUPS, STATE_DIM)

    y = _out_mlp(x2d, attn, p['ga'], mix_s, p['woa'], p['wos'], p['gm'], mlp_w[0], mlp_w[1], tm, MLP_FF_TILE)
    return y.reshape(n, t, D_MODEL), new_k, new_v, h_re, h_im, mlp_w


def _prepare_params(l, attn_norm_g, w_in, q_norm_g, k_norm_g, attn_sinks,
                    ssm_A_re, ssm_A_im, ssm_log_dt, ssm_B_re, ssm_B_im, ssm_C_re, ssm_C_im, ssm_D,
                    w_glu, b_glu, attn_out_g, ssm_out_g, w_out, mlp_norm_g, w_mlp_up, w_mlp_down):
    w_in_l = w_in[l]
    heads_per_blk = MXU_DIM // HEAD_DIM

    a_re = ssm_A_re[l].reshape(1, N_STATE)
    a_im = ssm_A_im[l].reshape(1, N_STATE)
    ldt = jnp.broadcast_to(ssm_log_dt[l][:, None], (N_SSM_GROUPS, STATE_DIM)).reshape(1, N_STATE)
    b_re = ssm_B_re[l].transpose(2, 0, 1).reshape(SSM_GROUP, N_STATE)
    b_im = ssm_B_im[l].transpose(2, 0, 1).reshape(SSM_GROUP, N_STATE)
    c_re = ssm_C_re[l].transpose(1, 0, 2).reshape(SSM_GROUP, N_STATE)
    c_im = ssm_C_im[l].transpose(1, 0, 2).reshape(SSM_GROUP, N_STATE)
    ar8, ai8, wb, wct = _ssm_discretize(a_re, a_im, ldt, b_re, b_im, c_re, c_im)

    w_out_l = w_out[l]
    return dict(
        gn=attn_norm_g[l].reshape(1, D_MODEL),
        w_in=w_in_l.astype(BF16),
        gq=jnp.tile(q_norm_g[l], heads_per_blk).reshape(1, MXU_DIM),
        gk=jnp.tile(k_norm_g[l], heads_per_blk).reshape(1, MXU_DIM),
        sinks=attn_sinks[l].astype(F32),
        ar8=ar8, ai8=ai8, wb=wb, wct=wct,
        d=ssm_D[l].reshape(1, SSM_WIDTH),
        w_glu=w_glu[l].astype(BF16),
        b_glu=b_glu[l].reshape(1, SSM_WIDTH),
        gs=ssm_out_g[l].reshape(1, SSM_WIDTH),
        ga=_pair_heads(attn_out_g[l], 0).reshape(1, ATTN_WIDTH),
        woa=_pair_heads(w_out_l[:ATTN_WIDTH], 0).astype(BF16),
        wos=w_out_l[ATTN_WIDTH:].astype(BF16),
        gm=mlp_norm_g[l].reshape(1, D_MODEL),
        w_up_f32=w_mlp_up[l],
        w_down_f32=w_mlp_down[l],
    )


def kernel(x_prompt, x_sample, cache_k, cache_v, state_ssm_re, state_ssm_im, attn_norm_g, w_in, q_norm_g, k_norm_g, attn_sinks, ssm_A_re, ssm_A_im, ssm_log_dt, ssm_B_re, ssm_B_im, ssm_C_re, ssm_C_im, ssm_D, w_glu, b_glu, attn_out_g, ssm_out_g, w_out, mlp_norm_g, w_mlp_up, w_mlp_down):
    depth = w_in.shape[0]
    xp, xs = x_prompt, x_sample
    zeros_state = jnp.zeros((x_prompt.shape[0], N_SSM_GROUPS, STATE_DIM), F32)
    outs = [[] for _ in range(8)]
    for l in range(depth):
        p = _prepare_params(l, attn_norm_g, w_in, q_norm_g, k_norm_g, attn_sinks,
                            ssm_A_re, ssm_A_im, ssm_log_dt, ssm_B_re, ssm_B_im, ssm_C_re, ssm_C_im, ssm_D,
                            w_glu, b_glu, attn_out_g, ssm_out_g, w_out, mlp_norm_g, w_mlp_up, w_mlp_down)
        xp, kp, vp, hrp, hip, mlp_w = _layer(xp, None, None, zeros_state, zeros_state, p, None)
        xs, ks, vs, hrs, his, _ = _layer(xs, cache_k[l], cache_v[l], state_ssm_re[l], state_ssm_im[l], p, mlp_w)
        for lst, val in zip(outs, (kp, vp, hrp, hip, ks, vs, hrs, his)):
            lst.append(val)
    return (xp, xs) + tuple(jnp.stack(o) for o in outs)
```

```python
import functools
import math

import jax
import jax.numpy as jnp
from jax import lax
from jax.experimental import pallas as pl
from jax.experimental.pallas import tpu as pltpu

D_MODEL = 2048
ATTN_WIDTH = 1024
SSM_WIDTH = 1024
HEAD_DIM = 64
N_HEADS = 16
N_KV_HEADS = 4
Q_PER_KV = 4
KV_WIDTH = 256
WINDOW = 128
SSM_GROUP = 16
N_SSM_GROUPS = 64
STATE_DIM = 64
N_STATE = N_SSM_GROUPS * STATE_DIM
D_FF = 8192
PROJ_WIDTH = ATTN_WIDTH + 2 * KV_WIDTH + SSM_WIDTH
EPS = 1e-6
NEG_INF = -1e30

LANES = 128
SUBLANES = 8
MXU_DIM = 256
VMEM_LIMIT = 56 * 1024 * 1024

GROUPS_PER_BLOCK = MXU_DIM // SSM_GROUP
N_SSM_BLOCKS = N_SSM_GROUPS // GROUPS_PER_BLOCK
BLOCK_STATES = GROUPS_PER_BLOCK * STATE_DIM
SCAN_LANES = 512
SSM_SUB_TILE_ROWS = 256
SSM_SUB_TILES = 2
SSM_TAIL_LAG_BLOCKS = 4
MLP_FF_TILE = 1024
IN_PROJ_ROW_CHUNKS = 2

F32 = jnp.float32
BF16 = jnp.bfloat16


def _cparams(sem):
    return pltpu.CompilerParams(dimension_semantics=sem, vmem_limit_bytes=VMEM_LIMIT)


def _const_spec(shape):
    nd = len(shape)
    return pl.BlockSpec(shape, lambda *_: (0,) * nd, pipeline_mode=pl.Buffered(1))


def _discretize_kernel(are_ref, aim_ref, ldt_ref, bre_ref, bim_ref, cre_ref, cim_ref,
                       abr_ref, abi_ref, wb_ref, wct_ref):
    a_re = are_ref[...]
    a_im = aim_ref[...]
    dt = jnp.exp(ldt_ref[...])
    mag = jnp.exp(a_re * dt)
    ab_re = mag * jnp.cos(a_im * dt)
    ab_im = mag * jnp.sin(a_im * dt)
    abr_ref[...] = jnp.broadcast_to(ab_re, abr_ref.shape)
    abi_ref[...] = jnp.broadcast_to(ab_im, abi_ref.shape)
    x = ab_re - 1.0
    y = ab_im
    den = a_re * a_re + a_im * a_im
    k_re = (x * a_re + y * a_im) / den
    k_im = (y * a_re - x * a_im) / den
    b_re = bre_ref[...]
    b_im = bim_ref[...]
    bb_re = k_re * b_re - k_im * b_im
    bb_im = k_re * b_im + k_im * b_re

    rows = GROUPS_PER_BLOCK * SSM_GROUP
    row_group = lax.broadcasted_iota(jnp.int32, (rows, BLOCK_STATES), 0) // SSM_GROUP
    col_group = lax.broadcasted_iota(jnp.int32, (rows, BLOCK_STATES), 1) // STATE_DIM
    same_group = row_group == col_group

    def block_diag(m, j):
        blk = m[:, j * BLOCK_STATES:(j + 1) * BLOCK_STATES]
        return jnp.where(same_group, jnp.concatenate([blk] * GROUPS_PER_BLOCK, axis=0), 0.0).astype(BF16)

    c_re = cre_ref[...]
    c_im_neg = -cim_ref[...]
    for j in range(N_SSM_BLOCKS):
        wb_ref[j, :, :BLOCK_STATES] = block_diag(bb_re, j)
        wb_ref[j, :, BLOCK_STATES:] = block_diag(bb_im, j)
        wct_ref[j, :, :BLOCK_STATES] = block_diag(c_re, j)
        wct_ref[j, :, BLOCK_STATES:] = block_diag(c_im_neg, j)


def _ssm_discretize(a_re, a_im, log_dt, b_re, b_im, c_re, c_im):
    rep = jax.ShapeDtypeStruct((SUBLANES, N_STATE), F32)
    blocks = jax.ShapeDtypeStruct((N_SSM_BLOCKS, GROUPS_PER_BLOCK * SSM_GROUP, 2 * BLOCK_STATES), BF16)
    return pl.pallas_call(
        _discretize_kernel,
        out_shape=(rep, rep, blocks, blocks),
        name="ssm_discretize",
    )(a_re, a_im, log_dt, b_re, b_im, c_re, c_im)


def _head_rmsnorm(zc, gain, ones_blk):
    sq = (zc * zc).astype(BF16)
    ss = jnp.dot(sq, ones_blk, preferred_element_type=F32)
    return zc * lax.rsqrt(ss * (1.0 / HEAD_DIM) + EPS) * gain


def _in_proj_kernel(x_ref, gn_ref, w_ref, gq_ref, gk_ref, *rest):
    n_cast = (len(rest) - 3) // 2
    cast_in = rest[:n_cast]
    q_ref, kv_ref, u_ref = rest[n_cast:n_cast + 3]
    cast_out = rest[n_cast + 3:]
    for src, dst in zip(cast_in, cast_out):
        dst[...] = src[...].astype(BF16)
    ri = lax.broadcasted_iota(jnp.int32, (MXU_DIM, MXU_DIM), 0) // HEAD_DIM
    ci = lax.broadcasted_iota(jnp.int32, (MXU_DIM, MXU_DIM), 1) // HEAD_DIM
    ones_blk = (ri == ci).astype(BF16)
    gq = gq_ref[...]
    chunk = x_ref.shape[0] // IN_PROJ_ROW_CHUNKS
    low = lax.broadcasted_iota(jnp.int32, (chunk, LANES), 1) < HEAD_DIM
    for ck in range(IN_PROJ_ROW_CHUNKS):
        rs = slice(ck * chunk, (ck + 1) * chunk)
        x = x_ref[rs, :]
        r = lax.rsqrt(jnp.mean(x * x, axis=-1, keepdims=True) + EPS)
        xn = (x * r * gn_ref[...]).astype(BF16)
        zq = jnp.dot(xn, w_ref[:, :ATTN_WIDTH], preferred_element_type=F32)
        zr = jnp.dot(xn, w_ref[:, ATTN_WIDTH:], preferred_element_type=F32)
        lane_blocks = []
        for c in range(ATTN_WIDTH // MXU_DIM):
            qn = _head_rmsnorm(zq[:, c * MXU_DIM:(c + 1) * MXU_DIM], gq, ones_blk) * (HEAD_DIM ** -0.5)
            lane_blocks += [qn[:, :LANES], qn[:, LANES:]]
        for pair in range(N_KV_HEADS // 2):
            for r in range(Q_PER_KV):
                first = lane_blocks[((2 * pair) * Q_PER_KV + r) // 2]
                second = lane_blocks[((2 * pair + 1) * Q_PER_KV + r) // 2]
                if r % 2 == 0:
                    blk = jnp.where(low, first, pltpu.roll(second, HEAD_DIM, axis=1))
                else:
                    blk = jnp.where(low, pltpu.roll(first, HEAD_DIM, axis=1), second)
                dst = pair * Q_PER_KV + r
                q_ref[rs, dst * LANES:(dst + 1) * LANES] = blk.astype(BF16)
        kv_ref[rs, :KV_WIDTH] = _head_rmsnorm(zr[:, :KV_WIDTH], gk_ref[...], ones_blk)
        kv_ref[rs, KV_WIDTH:] = zr[:, KV_WIDTH:2 * KV_WIDTH]
        u_ref[rs, :] = zr[:, 2 * KV_WIDTH:]


def _paired_head_slab(i):
    g, r = i // Q_PER_KV, i % Q_PER_KV
    paired = (g // 2) * (2 * Q_PER_KV) + r * 2 + g % 2
    return jnp.where(i < N_HEADS, paired, i)


def _in_proj(x2d, gn, w_in_b, gq, gk, tm, to_bf16=()):
    rows = x2d.shape[0]
    steps = rows // tm
    mats = [w for w, _ in to_bf16]
    assert all(w.shape[0] % (steps * 2 * SUBLANES) == 0 for w in mats)
    slab_in = [pl.BlockSpec((w.shape[0] // steps, w.shape[1]), lambda i: (i, 0)) for w in mats]
    slab_out = [pl.BlockSpec((w.shape[0] // steps, w.shape[1]),
                             (lambda i: (i, 0)) if place is None else (lambda i, place=place: (place(i), 0)))
                for w, place in to_bf16]
    return pl.pallas_call(
        _in_proj_kernel,
        grid=(steps,),
        in_specs=[
            pl.BlockSpec((tm, D_MODEL), lambda i: (i, 0)),
            _const_spec((1, D_MODEL)),
            _const_spec((D_MODEL, PROJ_WIDTH)),
            _const_spec((1, MXU_DIM)),
            _const_spec((1, MXU_DIM)),
        ] + slab_in,
        out_specs=[
            pl.BlockSpec((tm, ATTN_WIDTH), lambda i: (i, 0)),
            pl.BlockSpec((tm, 2 * KV_WIDTH), lambda i: (i, 0)),
            pl.BlockSpec((tm, SSM_WIDTH), lambda i: (i, 0)),
        ] + slab_out,
        out_shape=(
            jax.ShapeDtypeStruct((rows, ATTN_WIDTH), BF16),
            jax.ShapeDtypeStruct((rows, 2 * KV_WIDTH), F32),
            jax.ShapeDtypeStruct((rows, SSM_WIDTH), F32),
        ) + tuple(jax.ShapeDtypeStruct(w.shape, BF16) for w in mats),
        compiler_params=_cparams(("arbitrary",)),
        name="in_proj",
    )(x2d, gn, w_in_b, gq, gk, *mats)


def _alibi_slope(head):
    return 2.0 ** (-8.0 * (head + 1) / N_HEADS)


ATTN_BLOCKS_PER_STEP = 4


def _prompt_attn_kernel(sink_ref, q_ref, kvp_ref, kvc_ref, bias_ref, o_ref):
    tq = WINDOW
    lane = lax.broadcasted_iota(jnp.int32, (tq, LANES), 1)
    low = lane < HEAD_DIM
    zero = jnp.zeros((tq, LANES), BF16)

    def keys_or_values(blk, lanes):
        own = kvc_ref[blk * tq:(blk + 1) * tq, lanes]
        prev = kvp_ref[:, lanes] if blk == 0 else kvc_ref[(blk - 1) * tq:blk * tq, lanes]
        return jnp.concatenate([prev, own], axis=0).astype(BF16)

    for blk in range(q_ref.shape[0] // tq):
        rows = slice(blk * tq, (blk + 1) * tq)
        has_prev = (pl.program_id(1) > 0).astype(jnp.int32) if blk == 0 else 1
        for pair in range(N_KV_HEADS // 2):
            kc = keys_or_values(blk, slice(pair * LANES, (pair + 1) * LANES))
            vc = keys_or_values(blk, slice(KV_WIDTH + pair * LANES, KV_WIDTH + (pair + 1) * LANES))
            qb = [q_ref[rows, (pair * Q_PER_KV + r) * LANES:(pair * Q_PER_KV + r + 1) * LANES]
                  for r in range(Q_PER_KV)]
            qs = jnp.concatenate([jnp.where(low, b, zero) for b in qb] + [jnp.where(low, zero, b) for b in qb],
                                 axis=0)
            scores = lax.dot_general(qs, kc, (((1,), (1,)), ((), ())), preferred_element_type=F32)
            probs = []
            inv_den = []
            for hh in range(2 * Q_PER_KV):
                head = (2 * pair + hh // Q_PER_KV) * Q_PER_KV + hh % Q_PER_KV
                s = scores[hh * tq:(hh + 1) * tq] + bias_ref[has_prev, head]
                sink = sink_ref[head]
                m = jnp.maximum(jnp.max(s, axis=-1, keepdims=True), sink)
                p = jnp.exp(s - m)
                den = jnp.sum(p, axis=-1, keepdims=True) + jnp.exp(sink - m)
                probs.append(p.astype(BF16))
                inv_den.append(1.0 / den)
            pv = jnp.dot(jnp.concatenate(probs, axis=0), vc, preferred_element_type=F32)
            for r in range(Q_PER_KV):
                o_low = pv[r * tq:(r + 1) * tq] * inv_den[r]
                o_high = pv[(Q_PER_KV + r) * tq:(Q_PER_KV + r + 1) * tq] * inv_den[Q_PER_KV + r]
                o_ref[rows, (pair * Q_PER_KV + r) * LANES:(pair * Q_PER_KV + r + 1) * LANES] = (
                    jnp.where(low, o_low, o_high).astype(BF16))


def _prompt_attn_bias():
    w = WINDOW
    slope = jnp.exp2(-8.0 * (jnp.arange(N_HEADS, dtype=F32) + 1.0) / N_HEADS)
    i = jnp.arange(w)[:, None]
    j = jnp.arange(2 * w)[None, :]
    d = i + w - j
    ok = (d >= 0) & (d <= w)
    ok = jnp.stack([ok & (j >= w), ok])
    return jnp.where(ok[:, None], -slope[None, :, None, None] * d.astype(F32)[None, None], NEG_INF)


def _prompt_attention(sinks, q, kv, batch, seq):
    per = ATTN_BLOCKS_PER_STEP
    tq = per * WINDOW
    nb = seq // tq
    bias = _prompt_attn_bias()
    return pl.pallas_call(
        _prompt_attn_kernel,
        grid=(batch, nb),
        in_specs=[
            pl.BlockSpec(memory_space=pltpu.SMEM),
            pl.BlockSpec((tq, ATTN_WIDTH), lambda b, n: (b * nb + n, 0)),
            pl.BlockSpec((WINDOW, 2 * KV_WIDTH), lambda b, n: (b * nb * per + jnp.maximum(n * per - 1, 0), 0)),
            pl.BlockSpec((tq, 2 * KV_WIDTH), lambda b, n: (b * nb + n, 0)),
            _const_spec(bias.shape),
        ],
        out_specs=pl.BlockSpec((tq, ATTN_WIDTH), lambda b, n: (b * nb + n, 0)),
        out_shape=jax.ShapeDtypeStruct((batch * seq, ATTN_WIDTH), BF16),
        compiler_params=_cparams(("parallel", "parallel")),
        name="prompt_attention",
    )(sinks, q, kv, kv, bias)


SAMPLE_ATTN_BATCH = 16
SAMPLE_ATTN_GROUP = 2


def _sample_attn_kernel(sink_ref, q_ref, ckt_ref, cvt_ref, kvn_ref, o_ref, nkt_ref, nvt_ref, *, t_new):
    rows = Q_PER_KV * t_new
    tk = 2 * WINDOW
    i = lax.broadcasted_iota(jnp.int32, (rows, tk), 0) % t_new
    c = lax.broadcasted_iota(jnp.int32, (rows, tk), 1)
    is_new = c >= tk - t_new
    j = jnp.where(c < WINDOW, c, c - (WINDOW - t_new))
    d = i + WINDOW - j
    valid = (d >= 0) & (d <= WINDOW) & ((c < WINDOW) | is_new)
    delta = d.astype(F32)
    lane_q = lax.broadcasted_iota(jnp.int32, (rows, LANES), 1)
    low = lane_q < HEAD_DIM
    zero_q = jnp.zeros((rows, LANES), BF16)
    lane_w = lax.broadcasted_iota(jnp.int32, (WINDOW, LANES), 1)
    tail = lane_w >= WINDOW - t_new
    rid = lax.broadcasted_iota(jnp.int32, (rows, 1), 0) // t_new
    alibi, sinks = [], []
    for pair in range(N_KV_HEADS // 2):
        sl, sk = [], []
        for half in range(2):
            slope = jnp.zeros((rows, 1), F32)
            sink = jnp.zeros((rows, 1), F32)
            for r in range(Q_PER_KV):
                head = (2 * pair + half) * Q_PER_KV + r
                slope = jnp.where(rid == r, _alibi_slope(head), slope)
                sink = jnp.where(rid == r, sink_ref[head], sink)
            sl.append(slope)
            sk.append(sink)
        alibi.append(jnp.concatenate(sl, axis=0) * jnp.concatenate([delta, delta], axis=0))
        sinks.append(jnp.concatenate(sk, axis=0))
    valid2 = jnp.concatenate([valid, valid], axis=0)

    lead_zero = jnp.zeros((WINDOW - SUBLANES, LANES), F32)

    def new_rows_transposed(e, lanes):
        return jnp.concatenate([lead_zero, kvn_ref[e, :, lanes]], axis=0).T

    def shifted(old_t, new_t):
        return jnp.where(tail, new_t, pltpu.roll(old_t, WINDOW - t_new, axis=1))

    def keys_and_scores(e, pair):
        heads = slice(2 * pair, 2 * pair + 2)
        kt = ckt_ref[e, heads].reshape(2 * HEAD_DIM, WINDOW)
        knt = new_rows_transposed(e, slice(pair * LANES, (pair + 1) * LANES))
        nkt_ref[e, heads] = shifted(kt, knt).reshape(2, HEAD_DIM, WINDOW)
        qp = q_ref[e, pair * rows:(pair + 1) * rows, :]
        qs = jnp.concatenate([jnp.where(low, qp, zero_q), jnp.where(low, zero_q, qp)], axis=0)
        kt_all = jnp.concatenate([kt, knt], axis=1).astype(BF16)
        return jnp.dot(qs, kt_all, preferred_element_type=F32)

    def values(e, pair):
        heads = slice(2 * pair, 2 * pair + 2)
        vt = cvt_ref[e, heads].reshape(2 * HEAD_DIM, WINDOW)
        vnt = new_rows_transposed(e, slice(KV_WIDTH + pair * LANES, KV_WIDTH + (pair + 1) * LANES))
        nvt_ref[e, heads] = shifted(vt, vnt).reshape(2, HEAD_DIM, WINDOW)
        return jnp.concatenate([vt, vnt], axis=1).astype(BF16)

    def softmax(pair, scores):
        s = jnp.where(valid2, scores - alibi[pair], NEG_INF)
        m = jnp.maximum(jnp.max(s, axis=-1, keepdims=True), sinks[pair])
        p = jnp.exp(s - m)
        den = jnp.sum(p, axis=-1, keepdims=True) + jnp.exp(sinks[pair] - m)
        return p.astype(BF16), 1.0 / den

    def body(i, carry):
        group = [(i * SAMPLE_ATTN_GROUP + k, pair) for k in range(SAMPLE_ATTN_GROUP) for pair in range(N_KV_HEADS // 2)]
        scores = [keys_and_scores(e, pair) for e, pair in group]
        vt_all = [values(e, pair) for e, pair in group]
        probs = [softmax(pair, sc) for (e, pair), sc in zip(group, scores)]
        for (e, pair), (p, inv_den), vt in zip(group, probs, vt_all):
            pv = lax.dot_general(p, vt, (((1,), (1,)), ((), ())), preferred_element_type=F32)
            pv = pv * inv_den
            o_ref[e, pair * rows:(pair + 1) * rows, :] = jnp.where(low, pv[:rows], pv[rows:]).astype(BF16)
        return carry

    lax.fori_loop(0, q_ref.shape[0] // SAMPLE_ATTN_GROUP, body, 0)


def _sample_attention(sinks, q3, ckt, cvt, kvn, t_new):
    n = q3.shape[0]
    bn = SAMPLE_ATTN_BATCH
    qrows = q3.shape[1]
    cache_spec = pl.BlockSpec((bn, N_KV_HEADS, HEAD_DIM, WINDOW), lambda b: (b, 0, 0, 0))
    cache_shape = jax.ShapeDtypeStruct((n, N_KV_HEADS, HEAD_DIM, WINDOW), F32)
    return pl.pallas_call(
        functools.partial(_sample_attn_kernel, t_new=t_new),
        grid=(n // bn,),
        in_specs=[
            pl.BlockSpec(memory_space=pltpu.SMEM),
            pl.BlockSpec((bn, qrows, LANES), lambda b: (b, 0, 0)),
            cache_spec,
            cache_spec,
            pl.BlockSpec((bn, SUBLANES, 2 * KV_WIDTH), lambda b: (b, 0, 0)),
        ],
        out_specs=[pl.BlockSpec((bn, qrows, LANES), lambda b: (b, 0, 0)), cache_spec, cache_spec],
        out_shape=(jax.ShapeDtypeStruct((n, qrows, LANES), BF16), cache_shape, cache_shape),
        compiler_params=_cparams(("parallel",)),
        name="sample_attention",
    )(sinks, q3, ckt, cvt, kvn)


def _gelu_tanh(x):
    c = math.sqrt(2.0 / math.pi)
    return 0.5 * x * (1.0 + jnp.tanh(c * (x + 0.044715 * (x * x * x))))


def _ssm_kernel(u_ref, h0re_ref, h0im_ref, ar_ref, ai_ref, wb_ref, wct_ref, d_ref, wglu_ref, bglu_ref, gs_ref,
                mix_ref, hre_ref, him_ref, s_ref, *, nb, steps, n_sub):
    @pl.when(pl.program_id(0) == 0)
    def _():
        hre_ref[...] = h0re_ref[...]
        him_ref[...] = h0im_ref[...]

    dims = dict(nb=nb, steps=steps)
    pending = []

    def trace_pending_tails():
        while pending:
            y, s = pending.pop(0)
            _ssm_gate_and_store(y, wglu_ref, bglu_ref, gs_ref, mix_ref, sub=s, **dims)

    for sub in range(n_sub):
        u, ub = _ssm_permute_in(u_ref, sub=sub, **dims)
        ys = []
        for j in range(N_SSM_BLOCKS):
            if j == SSM_TAIL_LAG_BLOCKS:
                trace_pending_tails()
            ys.append(_ssm_state_block(ub, j, ar_ref, ai_ref, wb_ref, wct_ref, hre_ref, him_ref, s_ref, sub=sub, **dims))
        trace_pending_tails()
        pending.append((jnp.concatenate(ys, axis=1) + d_ref[...] * u, sub))
    trace_pending_tails()


def _ssm_positions(ref, nb, steps, sub):
    per_sub = nb * steps // ref.shape[0]
    return slice(sub * per_sub, (sub + 1) * per_sub)


def _ssm_permute_in(u_ref, *, nb, steps, sub):
    rows = nb * steps
    tm_row = lax.broadcasted_iota(jnp.int32, (rows, rows), 0)
    bm_col = lax.broadcasted_iota(jnp.int32, (rows, rows), 1)
    to_time_major = (bm_col == (tm_row % nb) * steps + tm_row // nb).astype(BF16)
    u_bm = u_ref[:, _ssm_positions(u_ref, nb, steps, sub), :].reshape(rows, SSM_WIDTH)
    u_hi = u_bm.astype(BF16)
    u_lo = (u_bm - u_hi.astype(F32)).astype(BF16)
    u_hi_tm = jnp.dot(to_time_major, u_hi, preferred_element_type=F32)
    u = u_hi_tm + jnp.dot(to_time_major, u_lo, preferred_element_type=F32)
    return u, u_hi_tm.astype(BF16)


def _ssm_state_block(ub, j, ar_ref, ai_ref, wb_ref, wct_ref, hre_ref, him_ref, s_ref, *, nb, steps, sub):
    rows = nb * steps
    row0 = (sub % (s_ref.shape[0] // rows)) * rows
    blk_cols = 2 * BLOCK_STATES
    s_ref[row0:row0 + rows, j * blk_cols:(j + 1) * blk_cols] = jnp.dot(
        ub[:, j * MXU_DIM:(j + 1) * MXU_DIM], wb_ref[j], preferred_element_type=F32)
    for part in range(BLOCK_STATES // SCAN_LANES):
        rc = j * blk_cols + part * SCAN_LANES
        ic = rc + BLOCK_STATES
        sc = j * BLOCK_STATES + part * SCAN_LANES
        a_re = ar_ref[:, sc:sc + SCAN_LANES]
        a_im = ai_ref[:, sc:sc + SCAN_LANES]
        for bg in range(nb // SUBLANES):
            b0 = bg * SUBLANES
            h_re = hre_ref[b0:b0 + SUBLANES, sc:sc + SCAN_LANES]
            h_im = him_ref[b0:b0 + SUBLANES, sc:sc + SCAN_LANES]
            for t in range(steps):
                row = row0 + t * nb + b0
                n_re = a_re * h_re - a_im * h_im + s_ref[row:row + SUBLANES, rc:rc + SCAN_LANES]
                n_im = a_re * h_im + a_im * h_re + s_ref[row:row + SUBLANES, ic:ic + SCAN_LANES]
                s_ref[row:row + SUBLANES, rc:rc + SCAN_LANES] = n_re
                s_ref[row:row + SUBLANES, ic:ic + SCAN_LANES] = n_im
                h_re, h_im = n_re, n_im
            hre_ref[b0:b0 + SUBLANES, sc:sc + SCAN_LANES] = h_re
            him_ref[b0:b0 + SUBLANES, sc:sc + SCAN_LANES] = h_im
    hb = s_ref[row0:row0 + rows, j * blk_cols:(j + 1) * blk_cols].astype(BF16)
    return lax.dot_general(hb, wct_ref[j], (((1,), (1,)), ((), ())), preferred_element_type=F32)


def _ssm_gate_and_store(y, wglu_ref, bglu_ref, gs_ref, mix_ref, *, nb, steps, sub):
    rows = nb * steps
    bm_row = lax.broadcasted_iota(jnp.int32, (rows, rows), 0)
    tm_col = lax.broadcasted_iota(jnp.int32, (rows, rows), 1)
    to_batch_major = (bm_row == (tm_col % nb) * steps + tm_col // nb).astype(BF16)
    g = _gelu_tanh(y)
    gate = jnp.dot(g.astype(BF16), wglu_ref[...], preferred_element_type=F32) + bglu_ref[...]
    so = g * jax.nn.sigmoid(gate)
    r = lax.rsqrt(jnp.mean(so * so, axis=-1, keepdims=True) + EPS)
    mix_tm = (so * r * gs_ref[...]).astype(BF16)
    mix_bm = jnp.dot(to_batch_major, mix_tm, preferred_element_type=F32).astype(BF16)
    mix_ref[:, _ssm_positions(mix_ref, nb, steps, sub), :] = mix_bm.reshape(
        mix_ref.shape[0], rows // mix_ref.shape[0], SSM_WIDTH)


def _ssm_mixer(u3, h0_re, h0_im, ar8, ai8, wb, wct, d, wglu_b, bglu, gs, nb, steps, n_sub):
    nbv, seq, _ = u3.shape
    blk_rows = n_sub * nb * steps // nbv
    assert blk_rows % SUBLANES == 0 and seq % blk_rows == 0
    cols = 2 * N_STATE
    tr = min(n_sub, 2) * nb * steps
    return pl.pallas_call(
        functools.partial(_ssm_kernel, nb=nb, steps=steps, n_sub=n_sub),
        grid=(seq // blk_rows,),
        in_specs=[
            pl.BlockSpec((nbv, blk_rows, SSM_WIDTH), lambda i: (0, i, 0)),
            _const_spec((nb, N_STATE)),
            _const_spec((nb, N_STATE)),
            _const_spec((SUBLANES, N_STATE)),
            _const_spec((SUBLANES, N_STATE)),
            _const_spec((N_SSM_BLOCKS, MXU_DIM, 2 * BLOCK_STATES)),
            _const_spec((N_SSM_BLOCKS, MXU_DIM, 2 * BLOCK_STATES)),
            _const_spec((1, SSM_WIDTH)),
            _const_spec((SSM_WIDTH, SSM_WIDTH)),
            _const_spec((1, SSM_WIDTH)),
            _const_spec((1, SSM_WIDTH)),
        ],
        out_specs=[
            pl.BlockSpec((nbv, blk_rows, SSM_WIDTH), lambda i: (0, i, 0)),
            pl.BlockSpec((nb, N_STATE), lambda i: (0, 0)),
            pl.BlockSpec((nb, N_STATE), lambda i: (0, 0)),
        ],
        out_shape=(
            jax.ShapeDtypeStruct((nbv, seq, SSM_WIDTH), BF16),
            jax.ShapeDtypeStruct((nb, N_STATE), F32),
            jax.ShapeDtypeStruct((nb, N_STATE), F32),
        ),
        scratch_shapes=[pltpu.VMEM((tr, cols), F32)],
        compiler_params=_cparams(("arbitrary",)),
        name="ssm_mixer",
    )(u3, h0_re, h0_im, ar8, ai8, wb, wct, d, wglu_b, bglu, gs)


def _out_mlp_kernel(x_ref, a_ref, ga_ref, ms_ref, wo_ref, gm_ref, wup_ref, wdn_ref, y_ref, hn_ref):
    @pl.when(pl.program_id(1) == 0)
    def _():
        half_rows = x_ref.shape[0] // 2
        for half in range(2):
            rs = slice(half * half_rows, (half + 1) * half_rows)
            h = x_ref[rs, :] + jnp.dot(ms_ref[rs, :], wo_ref[ATTN_WIDTH:, :], preferred_element_type=F32)
            a = a_ref[rs, :].astype(F32)
            ra = lax.rsqrt(jnp.mean(a * a, axis=-1, keepdims=True) + EPS)
            ma = (a * ra * ga_ref[...]).astype(BF16)
            h = h + jnp.dot(ma, wo_ref[:ATTN_WIDTH, :], preferred_element_type=F32)
            y_ref[rs, :] = h
            rh = lax.rsqrt(jnp.mean(h * h, axis=-1, keepdims=True) + EPS)
            hn_ref[rs, :] = (h * rh * gm_ref[...]).astype(BF16)

    t = jnp.dot(hn_ref[...], wup_ref[...], preferred_element_type=F32)
    t = jnp.maximum(t, 0.0)
    t = (t * t).astype(BF16)
    y_ref[...] += jnp.dot(t, wdn_ref[...], preferred_element_type=F32)


def _out_mlp(x2d, attn, ga, mix_s, wo, gm, wup, wdn, tm, tf):
    rows = x2d.shape[0]
    return pl.pallas_call(
        _out_mlp_kernel,
        grid=(rows // tm, D_FF // tf),
        in_specs=[
            pl.BlockSpec((tm, D_MODEL), lambda i, j: (i, 0)),
            pl.BlockSpec((tm, ATTN_WIDTH), lambda i, j: (i, 0)),
            _const_spec((1, ATTN_WIDTH)),
            pl.BlockSpec((tm, SSM_WIDTH), lambda i, j: (i, 0)),
            _const_spec((D_MODEL, D_MODEL)),
            _const_spec((1, D_MODEL)),
            pl.BlockSpec((D_MODEL, tf), lambda i, j: (0, j)),
            pl.BlockSpec((tf, D_MODEL), lambda i, j: (j, 0)),
        ],
        out_specs=pl.BlockSpec((tm, D_MODEL), lambda i, j: (i, 0)),
        out_shape=jax.ShapeDtypeStruct((rows, D_MODEL), F32),
        scratch_shapes=[pltpu.VMEM((tm, D_MODEL), BF16)],
        compiler_params=_cparams(("parallel", "arbitrary")),
        name="out_mlp",
    )(x2d, attn, ga, mix_s, wo, gm, wup, wdn)


def _pair_heads(a, axis):
    shape = a.shape
    split = shape[:axis] + (N_KV_HEADS // 2, 2, Q_PER_KV, HEAD_DIM) + shape[axis + 1:]
    return jnp.swapaxes(a.reshape(split), axis + 1, axis + 2).reshape(shape)


def _layer(x, cache_k, cache_v, h0_re, h0_im, p, side):
    n, t = x.shape[:2]
    rows = n * t
    tm = 512
    x2d = x.reshape(rows, D_MODEL)
    if side is None:
        assert D_MODEL // (rows // tm) == HEAD_DIM
        q, kv, u, *side = _in_proj(x2d, p['gn'], p['w_in'], p['gq'], p['gk'], tm,
                                   to_bf16=((p['w_up_f32'], None), (p['w_down_f32'], None),
                                            (p['w_out_f32'], _paired_head_slab), (p['w_glu_f32'], None)))
    else:
        q, kv, u = _in_proj(x2d, p['gn'], p['w_in'], p['gq'], p['gk'], tm)
    w_up, w_down, w_out_b, w_glu_b = side
    kv3 = kv.reshape(n, t, 2 * KV_WIDTH)

    if cache_k is None:
        attn = _prompt_attention(p['sinks'], q, kv, n, t)
        tail = kv3[:, t - WINDOW:]
        new_k = tail[..., :KV_WIDTH].reshape(n, WINDOW, N_KV_HEADS, HEAD_DIM)
        new_v = tail[..., KV_WIDTH:].reshape(n, WINDOW, N_KV_HEADS, HEAD_DIM)
        u3 = u.reshape(n, t, SSM_WIDTH)
        ssm_steps, ssm_subs = SSM_SUB_TILE_ROWS // n, SSM_SUB_TILES
    else:
        nblk = ATTN_WIDTH // LANES
        q3 = q.reshape(n, t, nblk, LANES).transpose(0, 2, 1, 3).reshape(n, nblk * t, LANES)
        kvn = jnp.pad(kv3, ((0, 0), (SUBLANES - t, 0), (0, 0)))
        ckt = cache_k.transpose(0, 2, 3, 1)
        cvt = cache_v.transpose(0, 2, 3, 1)
        a3, nkt, nvt = _sample_attention(p['sinks'], q3, ckt, cvt, kvn, t)
        attn = a3.reshape(n, nblk, t, LANES).transpose(0, 2, 1, 3).reshape(rows, ATTN_WIDTH)
        new_k = nkt.transpose(0, 3, 1, 2)
        new_v = nvt.transpose(0, 3, 1, 2)
        u3 = u.reshape(1, rows, SSM_WIDTH)
        ssm_steps, ssm_subs = t, 1

    mix3, h_re, h_im = _ssm_mixer(u3, h0_re.reshape(n, N_STATE), h0_im.reshape(n, N_STATE), p['ar8'], p['ai8'],
                                  p['wb'], p['wct'], p['d'], w_glu_b, p['b_glu'], p['gs'], n, ssm_steps, ssm_subs)
    mix_s = mix3.reshape(rows, SSM_WIDTH)
    h_re = h_re.reshape(n, N_SSM_GROUPS, STATE_DIM)
    h_im = h_im.reshape(n, N_SSM_GROUPS, STATE_DIM)

    y = _out_mlp(x2d, attn, p['ga'], mix_s, w_out_b, p['gm'], w_up, w_down, tm, MLP_FF_TILE)
    return y.reshape(n, t, D_MODEL), new_k, new_v, h_re, h_im, side


def _prepare_params(l, attn_norm_g, w_in, q_norm_g, k_norm_g, attn_sinks,
                    ssm_A_re, ssm_A_im, ssm_log_dt, ssm_B_re, ssm_B_im, ssm_C_re, ssm_C_im, ssm_D,
                    w_glu, b_glu, attn_out_g, ssm_out_g, w_out, mlp_norm_g, w_mlp_up, w_mlp_down):
    w_in_l = w_in[l]
    heads_per_blk = MXU_DIM // HEAD_DIM

    a_re = ssm_A_re[l].reshape(1, N_STATE)
    a_im = ssm_A_im[l].reshape(1, N_STATE)
    ldt = jnp.broadcast_to(ssm_log_dt[l][:, None], (N_SSM_GROUPS, STATE_DIM)).reshape(1, N_STATE)
    b_re = ssm_B_re[l].transpose(2, 0, 1).reshape(SSM_GROUP, N_STATE)
    b_im = ssm_B_im[l].transpose(2, 0, 1).reshape(SSM_GROUP, N_STATE)
    c_re = ssm_C_re[l].transpose(1, 0, 2).reshape(SSM_GROUP, N_STATE)
    c_im = ssm_C_im[l].transpose(1, 0, 2).reshape(SSM_GROUP, N_STATE)
    ar8, ai8, wb, wct = _ssm_discretize(a_re, a_im, ldt, b_re, b_im, c_re, c_im)

    return dict(
        gn=attn_norm_g[l].reshape(1, D_MODEL),
        w_in=w_in_l.astype(BF16),
        gq=jnp.tile(q_norm_g[l], heads_per_blk).reshape(1, MXU_DIM),
        gk=jnp.tile(k_norm_g[l], heads_per_blk).reshape(1, MXU_DIM),
        sinks=attn_sinks[l].astype(F32),
        ar8=ar8, ai8=ai8, wb=wb, wct=wct,
        d=ssm_D[l].reshape(1, SSM_WIDTH),
        w_glu_f32=w_glu[l],
        b_glu=b_glu[l].reshape(1, SSM_WIDTH),
        gs=ssm_out_g[l].reshape(1, SSM_WIDTH),
        ga=_pair_heads(attn_out_g[l], 0).reshape(1, ATTN_WIDTH),
        w_out_f32=w_out[l],
        gm=mlp_norm_g[l].reshape(1, D_MODEL),
        w_up_f32=w_mlp_up[l],
        w_down_f32=w_mlp_down[l],
    )


def kernel(x_prompt, x_sample, cache_k, cache_v, state_ssm_re, state_ssm_im, attn_norm_g, w_in, q_norm_g, k_norm_g, attn_sinks, ssm_A_re, ssm_A_im, ssm_log_dt, ssm_B_re, ssm_B_im, ssm_C_re, ssm_C_im, ssm_D, w_glu, b_glu, attn_out_g, ssm_out_g, w_out, mlp_norm_g, w_mlp_up, w_mlp_down):
    depth = w_in.shape[0]
    xp, xs = x_prompt, x_sample
    zeros_state = jnp.zeros((x_prompt.shape[0], N_SSM_GROUPS, STATE_DIM), F32)
    outs = [[] for _ in range(8)]
    for l in range(depth):
        p = _prepare_params(l, attn_norm_g, w_in, q_norm_g, k_norm_g, attn_sinks,
                            ssm_A_re, ssm_A_im, ssm_log_dt, ssm_B_re, ssm_B_im, ssm_C_re, ssm_C_im, ssm_D,
                            w_glu, b_glu, attn_out_g, ssm_out_g, w_out, mlp_norm_g, w_mlp_up, w_mlp_down)
        xp, kp, vp, hrp, hip, side = _layer(xp, None, None, zeros_state, zeros_state, p, None)
        xs, ks, vs, hrs, his, _ = _layer(xs, cache_k[l], cache_v[l], state_ssm_re[l], state_ssm_im[l], p, side)
        for lst, val in zip(outs, (kp, vp, hrp, hip, ks, vs, hrs, his)):
            lst.append(val)
    return (xp, xs) + tuple(jnp.stack(o) for o in outs)
```

```python
import functools
import math

import jax
import jax.numpy as jnp
from jax import lax
from jax.experimental import pallas as pl
from jax.experimental.pallas import tpu as pltpu

D_MODEL = 2048
ATTN_WIDTH = 1024
SSM_WIDTH = 1024
HEAD_DIM = 64
N_HEADS = 16
N_KV_HEADS = 4
Q_PER_KV = 4
KV_WIDTH = 256
WINDOW = 128
SSM_GROUP = 16
N_SSM_GROUPS = 64
STATE_DIM = 64
N_STATE = N_SSM_GROUPS * STATE_DIM
D_FF = 8192
PROJ_WIDTH = ATTN_WIDTH + 2 * KV_WIDTH + SSM_WIDTH
EPS = 1e-6
NEG_INF = -1e30

LANES = 128
SUBLANES = 8
MXU_DIM = 256
VMEM_LIMIT = 56 * 1024 * 1024

GROUPS_PER_BLOCK = MXU_DIM // SSM_GROUP
N_SSM_BLOCKS = N_SSM_GROUPS // GROUPS_PER_BLOCK
BLOCK_STATES = GROUPS_PER_BLOCK * STATE_DIM
SCAN_LANES = 512
SSM_SUB_TILE_ROWS = 256
SSM_SUB_TILES = 2
SSM_TAIL_LAG_BLOCKS = 4
MLP_FF_TILE = 1024
IN_PROJ_ROW_CHUNKS = 2

F32 = jnp.float32
BF16 = jnp.bfloat16


def _cparams(sem):
    return pltpu.CompilerParams(dimension_semantics=sem, vmem_limit_bytes=VMEM_LIMIT)


def _const_spec(shape):
    nd = len(shape)
    return pl.BlockSpec(shape, lambda *_: (0,) * nd, pipeline_mode=pl.Buffered(1))


def _discretize_kernel(are_ref, aim_ref, ldt_ref, bre_ref, bim_ref, cre_ref, cim_ref,
                       abr_ref, abi_ref, wb_ref, wct_ref):
    a_re = are_ref[...]
    a_im = aim_ref[...]
    dt = jnp.exp(ldt_ref[...])
    mag = jnp.exp(a_re * dt)
    ab_re = mag * jnp.cos(a_im * dt)
    ab_im = mag * jnp.sin(a_im * dt)
    abr_ref[...] = jnp.broadcast_to(ab_re, abr_ref.shape)
    abi_ref[...] = jnp.broadcast_to(ab_im, abi_ref.shape)
    x = ab_re - 1.0
    y = ab_im
    den = a_re * a_re + a_im * a_im
    k_re = (x * a_re + y * a_im) / den
    k_im = (y * a_re - x * a_im) / den
    b_re = bre_ref[...]
    b_im = bim_ref[...]
    bb_re = k_re * b_re - k_im * b_im
    bb_im = k_re * b_im + k_im * b_re

    rows = GROUPS_PER_BLOCK * SSM_GROUP
    row_group = lax.broadcasted_iota(jnp.int32, (rows, BLOCK_STATES), 0) // SSM_GROUP
    col_group = lax.broadcasted_iota(jnp.int32, (rows, BLOCK_STATES), 1) // STATE_DIM
    same_group = row_group == col_group

    def block_diag(m, j):
        blk = m[:, j * BLOCK_STATES:(j + 1) * BLOCK_STATES]
        return jnp.where(same_group, jnp.concatenate([blk] * GROUPS_PER_BLOCK, axis=0), 0.0).astype(BF16)

    c_re = cre_ref[...]
    c_im_neg = -cim_ref[...]
    for j in range(N_SSM_BLOCKS):
        wb_ref[j, :, :BLOCK_STATES] = block_diag(bb_re, j)
        wb_ref[j, :, BLOCK_STATES:] = block_diag(bb_im, j)
        wct_ref[j, :, :BLOCK_STATES] = block_diag(c_re, j)
        wct_ref[j, :, BLOCK_STATES:] = block_diag(c_im_neg, j)


def _ssm_discretize(a_re, a_im, log_dt, b_re, b_im, c_re, c_im):
    rep = jax.ShapeDtypeStruct((SUBLANES, N_STATE), F32)
    blocks = jax.ShapeDtypeStruct((N_SSM_BLOCKS, GROUPS_PER_BLOCK * SSM_GROUP, 2 * BLOCK_STATES), BF16)
    return pl.pallas_call(
        _discretize_kernel,
        out_shape=(rep, rep, blocks, blocks),
        name="ssm_discretize",
    )(a_re, a_im, log_dt, b_re, b_im, c_re, c_im)


def _head_rmsnorm(zc, gain, ones_blk):
    sq = (zc * zc).astype(BF16)
    ss = jnp.dot(sq, ones_blk, preferred_element_type=F32)
    return zc * lax.rsqrt(ss * (1.0 / HEAD_DIM) + EPS) * gain


def _in_proj_kernel(x_ref, gn_ref, w_ref, gq_ref, gk_ref, *rest):
    n_cast = (len(rest) - 3) // 2
    cast_in = rest[:n_cast]
    q_ref, kv_ref, u_ref = rest[n_cast:n_cast + 3]
    cast_out = rest[n_cast + 3:]
    for src, dst in zip(cast_in, cast_out):
        dst[...] = src[...].astype(BF16)
    ri = lax.broadcasted_iota(jnp.int32, (MXU_DIM, MXU_DIM), 0) // HEAD_DIM
    ci = lax.broadcasted_iota(jnp.int32, (MXU_DIM, MXU_DIM), 1) // HEAD_DIM
    ones_blk = (ri == ci).astype(BF16)
    gq = gq_ref[...]
    chunk = x_ref.shape[0] // IN_PROJ_ROW_CHUNKS
    low = lax.broadcasted_iota(jnp.int32, (chunk, LANES), 1) < HEAD_DIM
    for ck in range(IN_PROJ_ROW_CHUNKS):
        rs = slice(ck * chunk, (ck + 1) * chunk)
        x = x_ref[rs, :]
        r = lax.rsqrt(jnp.mean(x * x, axis=-1, keepdims=True) + EPS)
        xn = (x * r * gn_ref[...]).astype(BF16)
        zq = jnp.dot(xn, w_ref[:, :ATTN_WIDTH], preferred_element_type=F32)
        zr = jnp.dot(xn, w_ref[:, ATTN_WIDTH:], preferred_element_type=F32)
        lane_blocks = []
        for c in range(ATTN_WIDTH // MXU_DIM):
            qn = _head_rmsnorm(zq[:, c * MXU_DIM:(c + 1) * MXU_DIM], gq, ones_blk) * (HEAD_DIM ** -0.5)
            lane_blocks += [qn[:, :LANES], qn[:, LANES:]]
        for pair in range(N_KV_HEADS // 2):
            for r in range(Q_PER_KV):
                first = lane_blocks[((2 * pair) * Q_PER_KV + r) // 2]
                second = lane_blocks[((2 * pair + 1) * Q_PER_KV + r) // 2]
                if r % 2 == 0:
                    blk = jnp.where(low, first, pltpu.roll(second, HEAD_DIM, axis=1))
                else:
                    blk = jnp.where(low, pltpu.roll(first, HEAD_DIM, axis=1), second)
                dst = pair * Q_PER_KV + r
                q_ref[rs, dst * LANES:(dst + 1) * LANES] = blk.astype(BF16)
        kv_ref[rs, :KV_WIDTH] = _head_rmsnorm(zr[:, :KV_WIDTH], gk_ref[...], ones_blk)
        kv_ref[rs, KV_WIDTH:] = zr[:, KV_WIDTH:2 * KV_WIDTH]
        u_ref[rs, :] = zr[:, 2 * KV_WIDTH:]


def _paired_head_slab(i):
    g, r = i // Q_PER_KV, i % Q_PER_KV
    paired = (g // 2) * (2 * Q_PER_KV) + r * 2 + g % 2
    return jnp.where(i < N_HEADS, paired, i)


def _in_proj(x2d, gn, w_in_b, gq, gk, tm, to_bf16=()):
    rows = x2d.shape[0]
    steps = rows // tm
    mats = [w for w, _ in to_bf16]
    assert all(w.shape[0] % (steps * 2 * SUBLANES) == 0 for w in mats)
    slab_in = [pl.BlockSpec((w.shape[0] // steps, w.shape[1]), lambda i: (i, 0)) for w in mats]
    slab_out = [pl.BlockSpec((w.shape[0] // steps, w.shape[1]),
                             (lambda i: (i, 0)) if place is None else (lambda i, place=place: (place(i), 0)))
                for w, place in to_bf16]
    return pl.pallas_call(
        _in_proj_kernel,
        grid=(steps,),
        in_specs=[
            pl.BlockSpec((tm, D_MODEL), lambda i: (i, 0)),
            _const_spec((1, D_MODEL)),
            _const_spec((D_MODEL, PROJ_WIDTH)),
            _const_spec((1, MXU_DIM)),
            _const_spec((1, MXU_DIM)),
        ] + slab_in,
        out_specs=[
            pl.BlockSpec((tm, ATTN_WIDTH), lambda i: (i, 0)),
            pl.BlockSpec((tm, 2 * KV_WIDTH), lambda i: (i, 0)),
            pl.BlockSpec((tm, SSM_WIDTH), lambda i: (i, 0)),
        ] + slab_out,
        out_shape=(
            jax.ShapeDtypeStruct((rows, ATTN_WIDTH), BF16),
            jax.ShapeDtypeStruct((rows, 2 * KV_WIDTH), F32),
            jax.ShapeDtypeStruct((rows, SSM_WIDTH), F32),
        ) + tuple(jax.ShapeDtypeStruct(w.shape, BF16) for w in mats),
        compiler_params=_cparams(("arbitrary",)),
        name="in_proj",
    )(x2d, gn, w_in_b, gq, gk, *mats)


def _alibi_slope(head):
    return 2.0 ** (-8.0 * (head + 1) / N_HEADS)


ATTN_BLOCKS_PER_STEP = 4


def _prompt_attn_kernel(sink_ref, q_ref, kvp_ref, kvc_ref, bias_ref, o_ref):
    tq = WINDOW
    lane = lax.broadcasted_iota(jnp.int32, (tq, LANES), 1)
    low = lane < HEAD_DIM
    zero = jnp.zeros((tq, LANES), BF16)

    def keys_or_values(blk, lanes):
        own = kvc_ref[blk * tq:(blk + 1) * tq, lanes]
        prev = kvp_ref[:, lanes] if blk == 0 else kvc_ref[(blk - 1) * tq:blk * tq, lanes]
        return jnp.concatenate([prev, own], axis=0).astype(BF16)

    for blk in range(q_ref.shape[0] // tq):
        rows = slice(blk * tq, (blk + 1) * tq)
        has_prev = (pl.program_id(1) > 0).astype(jnp.int32) if blk == 0 else 1
        for pair in range(N_KV_HEADS // 2):
            kc = keys_or_values(blk, slice(pair * LANES, (pair + 1) * LANES))
            vc = keys_or_values(blk, slice(KV_WIDTH + pair * LANES, KV_WIDTH + (pair + 1) * LANES))
            qb = [q_ref[rows, (pair * Q_PER_KV + r) * LANES:(pair * Q_PER_KV + r + 1) * LANES]
                  for r in range(Q_PER_KV)]
            qs = jnp.concatenate([jnp.where(low, b, zero) for b in qb] + [jnp.where(low, zero, b) for b in qb],
                                 axis=0)
            scores = lax.dot_general(qs, kc, (((1,), (1,)), ((), ())), preferred_element_type=F32)
            probs = []
            inv_den = []
            for hh in range(2 * Q_PER_KV):
                head = (2 * pair + hh // Q_PER_KV) * Q_PER_KV + hh % Q_PER_KV
                s = scores[hh * tq:(hh + 1) * tq] + bias_ref[has_prev, head]
                sink = sink_ref[head]
                m = jnp.maximum(jnp.max(s, axis=-1, keepdims=True), sink)
                p = jnp.exp(s - m)
                den = jnp.sum(p, axis=-1, keepdims=True) + jnp.exp(sink - m)
                probs.append(p.astype(BF16))
                inv_den.append(1.0 / den)
            pv = jnp.dot(jnp.concatenate(probs, axis=0), vc, preferred_element_type=F32)
            for r in range(Q_PER_KV):
                o_low = pv[r * tq:(r + 1) * tq] * inv_den[r]
                o_high = pv[(Q_PER_KV + r) * tq:(Q_PER_KV + r + 1) * tq] * inv_den[Q_PER_KV + r]
                o_ref[rows, (pair * Q_PER_KV + r) * LANES:(pair * Q_PER_KV + r + 1) * LANES] = (
                    jnp.where(low, o_low, o_high).astype(BF16))


def _prompt_attn_bias():
    w = WINDOW
    slope = jnp.exp2(-8.0 * (jnp.arange(N_HEADS, dtype=F32) + 1.0) / N_HEADS)
    i = jnp.arange(w)[:, None]
    j = jnp.arange(2 * w)[None, :]
    d = i + w - j
    ok = (d >= 0) & (d <= w)
    ok = jnp.stack([ok & (j >= w), ok])
    return jnp.where(ok[:, None], -slope[None, :, None, None] * d.astype(F32)[None, None], NEG_INF)


def _prompt_attention(sinks, q, kv, batch, seq):
    per = ATTN_BLOCKS_PER_STEP
    tq = per * WINDOW
    nb = seq // tq
    bias = _prompt_attn_bias()
    return pl.pallas_call(
        _prompt_attn_kernel,
        grid=(batch, nb),
        in_specs=[
            pl.BlockSpec(memory_space=pltpu.SMEM),
            pl.BlockSpec((tq, ATTN_WIDTH), lambda b, n: (b * nb + n, 0)),
            pl.BlockSpec((WINDOW, 2 * KV_WIDTH), lambda b, n: (b * nb * per + jnp.maximum(n * per - 1, 0), 0)),
            pl.BlockSpec((tq, 2 * KV_WIDTH), lambda b, n: (b * nb + n, 0)),
            _const_spec(bias.shape),
        ],
        out_specs=pl.BlockSpec((tq, ATTN_WIDTH), lambda b, n: (b * nb + n, 0)),
        out_shape=jax.ShapeDtypeStruct((batch * seq, ATTN_WIDTH), BF16),
        compiler_params=_cparams(("parallel", "parallel")),
        name="prompt_attention",
    )(sinks, q, kv, kv, bias)


SAMPLE_ATTN_BATCH = 16
SAMPLE_ATTN_GROUP = 2


def _sample_attn_kernel(sink_ref, q_ref, ckt_ref, cvt_ref, kvn_ref, o_ref, nkt_ref, nvt_ref, *, t_new):
    rows = Q_PER_KV * t_new
    tk = 2 * WINDOW
    i = lax.broadcasted_iota(jnp.int32, (rows, tk), 0) % t_new
    c = lax.broadcasted_iota(jnp.int32, (rows, tk), 1)
    is_new = c >= tk - t_new
    j = jnp.where(c < WINDOW, c, c - (WINDOW - t_new))
    d = i + WINDOW - j
    valid = (d >= 0) & (d <= WINDOW) & ((c < WINDOW) | is_new)
    delta = d.astype(F32)
    lane_q = lax.broadcasted_iota(jnp.int32, (rows, LANES), 1)
    low = lane_q < HEAD_DIM
    zero_q = jnp.zeros((rows, LANES), BF16)
    lane_w = lax.broadcasted_iota(jnp.int32, (WINDOW, LANES), 1)
    tail = lane_w >= WINDOW - t_new
    rid = lax.broadcasted_iota(jnp.int32, (rows, 1), 0) // t_new
    alibi, sinks = [], []
    for pair in range(N_KV_HEADS // 2):
        sl, sk = [], []
        for half in range(2):
            slope = jnp.zeros((rows, 1), F32)
            sink = jnp.zeros((rows, 1), F32)
            for r in range(Q_PER_KV):
                head = (2 * pair + half) * Q_PER_KV + r
                slope = jnp.where(rid == r, _alibi_slope(head), slope)
                sink = jnp.where(rid == r, sink_ref[head], sink)
            sl.append(slope)
            sk.append(sink)
        alibi.append(jnp.concatenate(sl, axis=0) * jnp.concatenate([delta, delta], axis=0))
        sinks.append(jnp.concatenate(sk, axis=0))
    valid2 = jnp.concatenate([valid, valid], axis=0)

    lead_zero = jnp.zeros((WINDOW - SUBLANES, LANES), F32)

    def new_rows_transposed(e, lanes):
        return jnp.concatenate([lead_zero, kvn_ref[e, :, lanes]], axis=0).T

    def shifted(old_t, new_t):
        return jnp.where(tail, new_t, pltpu.roll(old_t, WINDOW - t_new, axis=1))

    def keys_and_scores(e, pair):
        heads = slice(2 * pair, 2 * pair + 2)
        kt = ckt_ref[e, heads].reshape(2 * HEAD_DIM, WINDOW)
        knt = new_rows_transposed(e, slice(pair * LANES, (pair + 1) * LANES))
        nkt_ref[e, heads] = shifted(kt, knt).reshape(2, HEAD_DIM, WINDOW)
        qp = q_ref[e, pair * rows:(pair + 1) * rows, :]
        qs = jnp.concatenate([jnp.where(low, qp, zero_q), jnp.where(low, zero_q, qp)], axis=0)
        kt_all = jnp.concatenate([kt, knt], axis=1).astype(BF16)
        return jnp.dot(qs, kt_all, preferred_element_type=F32)

    def values(e, pair):
        heads = slice(2 * pair, 2 * pair + 2)
        vt = cvt_ref[e, heads].reshape(2 * HEAD_DIM, WINDOW)
        vnt = new_rows_transposed(e, slice(KV_WIDTH + pair * LANES, KV_WIDTH + (pair + 1) * LANES))
        nvt_ref[e, heads] = shifted(vt, vnt).reshape(2, HEAD_DIM, WINDOW)
        return jnp.concatenate([vt, vnt], axis=1).astype(BF16)

    def softmax(pair, scores):
        s = jnp.where(valid2, scores - alibi[pair], NEG_INF)
        m = jnp.maximum(jnp.max(s, axis=-1, keepdims=True), sinks[pair])
        p = jnp.exp(s - m)
        den = jnp.sum(p, axis=-1, keepdims=True) + jnp.exp(sinks[pair] - m)
        return p.astype(BF16), 1.0 / den

    def body(i, carry):
        group = [(i * SAMPLE_ATTN_GROUP + k, pair) for k in range(SAMPLE_ATTN_GROUP) for pair in range(N_KV_HEADS // 2)]
        scores = [keys_and_scores(e, pair) for e, pair in group]
        vt_all = [values(e, pair) for e, pair in group]
        probs = [softmax(pair, sc) for (e, pair), sc in zip(group, scores)]
        for (e, pair), (p, inv_den), vt in zip(group, probs, vt_all):
            pv = lax.dot_general(p, vt, (((1,), (1,)), ((), ())), preferred_element_type=F32)
            pv = pv * inv_den
            o_ref[e, pair * rows:(pair + 1) * rows, :] = jnp.where(low, pv[:rows], pv[rows:]).astype(BF16)
        return carry

    lax.fori_loop(0, q_ref.shape[0] // SAMPLE_ATTN_GROUP, body, 0)


def _sample_attention(sinks, q3, ckt, cvt, kvn, t_new):
    n = q3.shape[0]
    bn = SAMPLE_ATTN_BATCH
    qrows = q3.shape[1]
    cache_spec = pl.BlockSpec((bn, N_KV_HEADS, HEAD_DIM, WINDOW), lambda b: (b, 0, 0, 0))
    cache_shape = jax.ShapeDtypeStruct((n, N_KV_HEADS, HEAD_DIM, WINDOW), F32)
    return pl.pallas_call(
        functools.partial(_sample_attn_kernel, t_new=t_new),
        grid=(n // bn,),
        in_specs=[
            pl.BlockSpec(memory_space=pltpu.SMEM),
            pl.BlockSpec((bn, qrows, LANES), lambda b: (b, 0, 0)),
            cache_spec,
            cache_spec,
            pl.BlockSpec((bn, SUBLANES, 2 * KV_WIDTH), lambda b: (b, 0, 0)),
        ],
        out_specs=[pl.BlockSpec((bn, qrows, LANES), lambda b: (b, 0, 0)), cache_spec, cache_spec],
        out_shape=(jax.ShapeDtypeStruct((n, qrows, LANES), BF16), cache_shape, cache_shape),
        compiler_params=_cparams(("parallel",)),
        name="sample_attention",
    )(sinks, q3, ckt, cvt, kvn)


def _gelu_tanh(x):
    c = math.sqrt(2.0 / math.pi)
    return 0.5 * x * (1.0 + jnp.tanh(c * (x + 0.044715 * (x * x * x))))


def _ssm_kernel(u_ref, h0re_ref, h0im_ref, ar_ref, ai_ref, wb_ref, wct_ref, d_ref, wglu_ref, bglu_ref, gs_ref,
                mix_ref, hre_ref, him_ref, s_ref, *, nb, steps, n_sub):
    @pl.when(pl.program_id(0) == 0)
    def _():
        hre_ref[...] = h0re_ref[...]
        him_ref[...] = h0im_ref[...]

    dims = dict(nb=nb, steps=steps)
    pending = []

    def trace_pending_tails():
        while pending:
            y, s = pending.pop(0)
            _ssm_gate_and_store(y, wglu_ref, bglu_ref, gs_ref, mix_ref, sub=s, **dims)

    for sub in range(n_sub):
        u, ub = _ssm_permute_in(u_ref, sub=sub, **dims)
        ys = []
        for j in range(N_SSM_BLOCKS):
            if j == SSM_TAIL_LAG_BLOCKS:
                trace_pending_tails()
            ys.append(_ssm_state_block(ub, j, ar_ref, ai_ref, wb_ref, wct_ref, hre_ref, him_ref, s_ref, sub=sub, **dims))
        trace_pending_tails()
        pending.append((jnp.concatenate(ys, axis=1) + d_ref[...] * u, sub))
    trace_pending_tails()


def _ssm_positions(ref, nb, steps, sub):
    per_sub = nb * steps // ref.shape[0]
    return slice(sub * per_sub, (sub + 1) * per_sub)


def _ssm_permute_in(u_ref, *, nb, steps, sub):
    rows = nb * steps
    tm_row = lax.broadcasted_iota(jnp.int32, (rows, rows), 0)
    bm_col = lax.broadcasted_iota(jnp.int32, (rows, rows), 1)
    to_time_major = (bm_col == (tm_row % nb) * steps + tm_row // nb).astype(BF16)
    u_bm = u_ref[:, _ssm_positions(u_ref, nb, steps, sub), :].reshape(rows, SSM_WIDTH)
    u_hi = u_bm.astype(BF16)
    u_lo = (u_bm - u_hi.astype(F32)).astype(BF16)
    u_hi_tm = jnp.dot(to_time_major, u_hi, preferred_element_type=F32)
    u = u_hi_tm + jnp.dot(to_time_major, u_lo, preferred_element_type=F32)
    return u, u_hi_tm.astype(BF16)


def _ssm_state_block(ub, j, ar_ref, ai_ref, wb_ref, wct_ref, hre_ref, him_ref, s_ref, *, nb, steps, sub):
    rows = nb * steps
    row0 = (sub % (s_ref.shape[0] // rows)) * rows
    blk_cols = 2 * BLOCK_STATES
    s_ref[row0:row0 + rows, j * blk_cols:(j + 1) * blk_cols] = jnp.dot(
        ub[:, j * MXU_DIM:(j + 1) * MXU_DIM], wb_ref[j], preferred_element_type=F32)
    for part in range(BLOCK_STATES // SCAN_LANES):
        rc = j * blk_cols + part * SCAN_LANES
        ic = rc + BLOCK_STATES
        sc = j * BLOCK_STATES + part * SCAN_LANES
        a_re = ar_ref[:, sc:sc + SCAN_LANES]
        a_im = ai_ref[:, sc:sc + SCAN_LANES]
        for bg in range(nb // SUBLANES):
            b0 = bg * SUBLANES
            h_re = hre_ref[b0:b0 + SUBLANES, sc:sc + SCAN_LANES]
            h_im = him_ref[b0:b0 + SUBLANES, sc:sc + SCAN_LANES]
            for t in range(steps):
                row = row0 + t * nb + b0
                n_re = a_re * h_re - a_im * h_im + s_ref[row:row + SUBLANES, rc:rc + SCAN_LANES]
                n_im = a_re * h_im + a_im * h_re + s_ref[row:row + SUBLANES, ic:ic + SCAN_LANES]
                s_ref[row:row + SUBLANES, rc:rc + SCAN_LANES] = n_re
                s_ref[row:row + SUBLANES, ic:ic + SCAN_LANES] = n_im
                h_re, h_im = n_re, n_im
            hre_ref[b0:b0 + SUBLANES, sc:sc + SCAN_LANES] = h_re
            him_ref[b0:b0 + SUBLANES, sc:sc + SCAN_LANES] = h_im
    hb = s_ref[row0:row0 + rows, j * blk_cols:(j + 1) * blk_cols].astype(BF16)
    return lax.dot_general(hb, wct_ref[j], (((1,), (1,)), ((), ())), preferred_element_type=F32)


def _ssm_gate_and_store(y, wglu_ref, bglu_ref, gs_ref, mix_ref, *, nb, steps, sub):
    rows = nb * steps
    bm_row = lax.broadcasted_iota(jnp.int32, (rows, rows), 0)
    tm_col = lax.broadcasted_iota(jnp.int32, (rows, rows), 1)
    to_batch_major = (bm_row == (tm_col % nb) * steps + tm_col // nb).astype(BF16)
    g = _gelu_tanh(y)
    gate = jnp.dot(g.astype(BF16), wglu_ref[...], preferred_element_type=F32) + bglu_ref[...]
    so = g * jax.nn.sigmoid(gate)
    r = lax.rsqrt(jnp.mean(so * so, axis=-1, keepdims=True) + EPS)
    mix_tm = (so * r * gs_ref[...]).astype(BF16)
    mix_bm = jnp.dot(to_batch_major, mix_tm, preferred_element_type=F32).astype(BF16)
    mix_ref[:, _ssm_positions(mix_ref, nb, steps, sub), :] = mix_bm.reshape(
        mix_ref.shape[0], rows // mix_ref.shape[0], SSM_WIDTH)


def _ssm_mixer(u3, h0_re, h0_im, ar8, ai8, wb, wct, d, wglu_b, bglu, gs, nb, steps, n_sub):
    nbv, seq, _ = u3.shape
    blk_rows = n_sub * nb * steps // nbv
    assert blk_rows % SUBLANES == 0 and seq % blk_rows == 0
    cols = 2 * N_STATE
    tr = min(n_sub, 2) * nb * steps
    return pl.pallas_call(
        functools.partial(_ssm_kernel, nb=nb, steps=steps, n_sub=n_sub),
        grid=(seq // blk_rows,),
        in_specs=[
            pl.BlockSpec((nbv, blk_rows, SSM_WIDTH), lambda i: (0, i, 0)),
            _const_spec((nb, N_STATE)),
            _const_spec((nb, N_STATE)),
            _const_spec((SUBLANES, N_STATE)),
            _const_spec((SUBLANES, N_STATE)),
            _const_spec((N_SSM_BLOCKS, MXU_DIM, 2 * BLOCK_STATES)),
            _const_spec((N_SSM_BLOCKS, MXU_DIM, 2 * BLOCK_STATES)),
            _const_spec((1, SSM_WIDTH)),
            _const_spec((SSM_WIDTH, SSM_WIDTH)),
            _const_spec((1, SSM_WIDTH)),
            _const_spec((1, SSM_WIDTH)),
        ],
        out_specs=[
            pl.BlockSpec((nbv, blk_rows, SSM_WIDTH), lambda i: (0, i, 0)),
            pl.BlockSpec((nb, N_STATE), lambda i: (0, 0)),
            pl.BlockSpec((nb, N_STATE), lambda i: (0, 0)),
        ],
        out_shape=(
            jax.ShapeDtypeStruct((nbv, seq, SSM_WIDTH), BF16),
            jax.ShapeDtypeStruct((nb, N_STATE), F32),
            jax.ShapeDtypeStruct((nb, N_STATE), F32),
        ),
        scratch_shapes=[pltpu.VMEM((tr, cols), F32)],
        compiler_params=_cparams(("arbitrary",)),
        name="ssm_mixer",
    )(u3, h0_re, h0_im, ar8, ai8, wb, wct, d, wglu_b, bglu, gs)


def _out_mlp_kernel(x_ref, a_ref, ga_ref, ms_ref, wo_ref, gm_ref, wup_ref, wdn_ref, y_ref, hn_ref):
    @pl.when(pl.program_id(1) == 0)
    def _():
        half_rows = x_ref.shape[0] // 2
        for half in range(2):
            rs = slice(half * half_rows, (half + 1) * half_rows)
            h = x_ref[rs, :] + jnp.dot(ms_ref[rs, :], wo_ref[ATTN_WIDTH:, :], preferred_element_type=F32)
            a = a_ref[rs, :].astype(F32)
            ra = lax.rsqrt(jnp.mean(a * a, axis=-1, keepdims=True) + EPS)
            ma = (a * ra * ga_ref[...]).astype(BF16)
            h = h + jnp.dot(ma, wo_ref[:ATTN_WIDTH, :], preferred_element_type=F32)
            y_ref[rs, :] = h
            rh = lax.rsqrt(jnp.mean(h * h, axis=-1, keepdims=True) + EPS)
            hn_ref[rs, :] = (h * rh * gm_ref[...]).astype(BF16)

    t = jnp.dot(hn_ref[...], wup_ref[...], preferred_element_type=F32)
    t = jnp.maximum(t, 0.0)
    t = (t * t).astype(BF16)
    y_ref[...] += jnp.dot(t, wdn_ref[...], preferred_element_type=F32)


def _out_mlp(x2d, attn, ga, mix_s, wo, gm, wup, wdn, tm, tf):
    rows = x2d.shape[0]
    n_ff = D_FF // tf

    def ff_tile(i, j):
        return jnp.where(i % 2 == 0, j, n_ff - 1 - j)

    return pl.pallas_call(
        _out_mlp_kernel,
        grid=(rows // tm, n_ff),
        in_specs=[
            pl.BlockSpec((tm, D_MODEL), lambda i, j: (i, 0)),
            pl.BlockSpec((tm, ATTN_WIDTH), lambda i, j: (i, 0)),
            _const_spec((1, ATTN_WIDTH)),
            pl.BlockSpec((tm, SSM_WIDTH), lambda i, j: (i, 0)),
            _const_spec((D_MODEL, D_MODEL)),
            _const_spec((1, D_MODEL)),
            pl.BlockSpec((D_MODEL, tf), lambda i, j: (0, ff_tile(i, j))),
            pl.BlockSpec((tf, D_MODEL), lambda i, j: (ff_tile(i, j), 0)),
        ],
        out_specs=pl.BlockSpec((tm, D_MODEL), lambda i, j: (i, 0)),
        out_shape=jax.ShapeDtypeStruct((rows, D_MODEL), F32),
        scratch_shapes=[pltpu.VMEM((tm, D_MODEL), BF16)],
        compiler_params=_cparams(("parallel", "arbitrary")),
        name="out_mlp",
    )(x2d, attn, ga, mix_s, wo, gm, wup, wdn)


def _pair_heads(a, axis):
    shape = a.shape
    split = shape[:axis] + (N_KV_HEADS // 2, 2, Q_PER_KV, HEAD_DIM) + shape[axis + 1:]
    return jnp.swapaxes(a.reshape(split), axis + 1, axis + 2).reshape(shape)


def _layer(x, cache_k, cache_v, h0_re, h0_im, p, side):
    n, t = x.shape[:2]
    rows = n * t
    tm = 512
    x2d = x.reshape(rows, D_MODEL)
    if side is None:
        assert D_MODEL // (rows // tm) == HEAD_DIM
        q, kv, u, *side = _in_proj(x2d, p['gn'], p['w_in'], p['gq'], p['gk'], tm,
                                   to_bf16=((p['w_up_f32'], None), (p['w_down_f32'], None),
                                            (p['w_out_f32'], _paired_head_slab), (p['w_glu_f32'], None)))
    else:
        q, kv, u = _in_proj(x2d, p['gn'], p['w_in'], p['gq'], p['gk'], tm)
    w_up, w_down, w_out_b, w_glu_b = side
    kv3 = kv.reshape(n, t, 2 * KV_WIDTH)

    if cache_k is None:
        attn = _prompt_attention(p['sinks'], q, kv, n, t)
        tail = kv3[:, t - WINDOW:]
        new_k = tail[..., :KV_WIDTH].reshape(n, WINDOW, N_KV_HEADS, HEAD_DIM)
        new_v = tail[..., KV_WIDTH:].reshape(n, WINDOW, N_KV_HEADS, HEAD_DIM)
        u3 = u.reshape(n, t, SSM_WIDTH)
        ssm_steps, ssm_subs = SSM_SUB_TILE_ROWS // n, SSM_SUB_TILES
    else:
        nblk = ATTN_WIDTH // LANES
        q3 = q.reshape(n, t, nblk, LANES).transpose(0, 2, 1, 3).reshape(n, nblk * t, LANES)
        kvn = jnp.pad(kv3, ((0, 0), (SUBLANES - t, 0), (0, 0)))
        ckt = cache_k.transpose(0, 2, 3, 1)
        cvt = cache_v.transpose(0, 2, 3, 1)
        a3, nkt, nvt = _sample_attention(p['sinks'], q3, ckt, cvt, kvn, t)
        attn = a3.reshape(n, nblk, t, LANES).transpose(0, 2, 1, 3).reshape(rows, ATTN_WIDTH)
        new_k = nkt.transpose(0, 3, 1, 2)
        new_v = nvt.transpose(0, 3, 1, 2)
        u3 = u.reshape(1, rows, SSM_WIDTH)
        ssm_steps, ssm_subs = t, 1

    mix3, h_re, h_im = _ssm_mixer(u3, h0_re.reshape(n, N_STATE), h0_im.reshape(n, N_STATE), p['ar8'], p['ai8'],
                                  p['wb'], p['wct'], p['d'], w_glu_b, p['b_glu'], p['gs'], n, ssm_steps, ssm_subs)
    mix_s = mix3.reshape(rows, SSM_WIDTH)
    h_re = h_re.reshape(n, N_SSM_GROUPS, STATE_DIM)
    h_im = h_im.reshape(n, N_SSM_GROUPS, STATE_DIM)

    y = _out_mlp(x2d, attn, p['ga'], mix_s, w_out_b, p['gm'], w_up, w_down, tm, MLP_FF_TILE)
    return y.reshape(n, t, D_MODEL), new_k, new_v, h_re, h_im, side


def _prepare_params(l, attn_norm_g, w_in, q_norm_g, k_norm_g, attn_sinks,
                    ssm_A_re, ssm_A_im, ssm_log_dt, ssm_B_re, ssm_B_im, ssm_C_re, ssm_C_im, ssm_D,
                    w_glu, b_glu, attn_out_g, ssm_out_g, w_out, mlp_norm_g, w_mlp_up, w_mlp_down):
    w_in_l = w_in[l]
    heads_per_blk = MXU_DIM // HEAD_DIM

    a_re = ssm_A_re[l].reshape(1, N_STATE)
    a_im = ssm_A_im[l].reshape(1, N_STATE)
    ldt = jnp.broadcast_to(ssm_log_dt[l][:, None], (N_SSM_GROUPS, STATE_DIM)).reshape(1, N_STATE)
    b_re = ssm_B_re[l].transpose(2, 0, 1).reshape(SSM_GROUP, N_STATE)
    b_im = ssm_B_im[l].transpose(2, 0, 1).reshape(SSM_GROUP, N_STATE)
    c_re = ssm_C_re[l].transpose(1, 0, 2).reshape(SSM_GROUP, N_STATE)
    c_im = ssm_C_im[l].transpose(1, 0, 2).reshape(SSM_GROUP, N_STATE)
    ar8, ai8, wb, wct = _ssm_discretize(a_re, a_im, ldt, b_re, b_im, c_re, c_im)

    return dict(
        gn=attn_norm_g[l].reshape(1, D_MODEL),
        w_in=w_in_l.astype(BF16),
        gq=jnp.tile(q_norm_g[l], heads_per_blk).reshape(1, MXU_DIM),
        gk=jnp.tile(k_norm_g[l], heads_per_blk).reshape(1, MXU_DIM),
        sinks=attn_sinks[l].astype(F32),
        ar8=ar8, ai8=ai8, wb=wb, wct=wct,
        d=ssm_D[l].reshape(1, SSM_WIDTH),
        w_glu_f32=w_glu[l],
        b_glu=b_glu[l].reshape(1, SSM_WIDTH),
        gs=ssm_out_g[l].reshape(1, SSM_WIDTH),
        ga=_pair_heads(attn_out_g[l], 0).reshape(1, ATTN_WIDTH),
        w_out_f32=w_out[l],
        gm=mlp_norm_g[l].reshape(1, D_MODEL),
        w_up_f32=w_mlp_up[l],
        w_down_f32=w_mlp_down[l],
    )


def kernel(x_prompt, x_sample, cache_k, cache_v, state_ssm_re, state_ssm_im, attn_norm_g, w_in, q_norm_g, k_norm_g, attn_sinks, ssm_A_re, ssm_A_im, ssm_log_dt, ssm_B_re, ssm_B_im, ssm_C_re, ssm_C_im, ssm_D, w_glu, b_glu, attn_out_g, ssm_out_g, w_out, mlp_norm_g, w_mlp_up, w_mlp_down):
    depth = w_in.shape[0]
    xp, xs = x_prompt, x_sample
    zeros_state = jnp.zeros((x_prompt.shape[0], N_SSM_GROUPS, STATE_DIM), F32)
    outs = [[] for _ in range(8)]
    for l in range(depth):
        p = _prepare_params(l, attn_norm_g, w_in, q_norm_g, k_norm_g, attn_sinks,
                            ssm_A_re, ssm_A_im, ssm_log_dt, ssm_B_re, ssm_B_im, ssm_C_re, ssm_C_im, ssm_D,
                            w_glu, b_glu, attn_out_g, ssm_out_g, w_out, mlp_norm_g, w_mlp_up, w_mlp_down)
        xp, kp, vp, hrp, hip, side = _layer(xp, None, None, zeros_state, zeros_state, p, None)
        xs, ks, vs, hrs, his, _ = _layer(xs, cache_k[l], cache_v[l], state_ssm_re[l], state_ssm_im[l], p, side)
        for lst, val in zip(outs, (kp, vp, hrp, hip, ks, vs, hrs, his)):
            lst.append(val)
    return (xp, xs) + tuple(jnp.stack(o) for o in outs)
```

```python
import functools
import math

import jax
import jax.numpy as jnp
from jax import lax
from jax.experimental import pallas as pl
from jax.experimental.pallas import tpu as pltpu

D_MODEL = 2048
ATTN_WIDTH = 1024
SSM_WIDTH = 1024
HEAD_DIM = 64
N_HEADS = 16
N_KV_HEADS = 4
Q_PER_KV = 4
KV_WIDTH = 256
WINDOW = 128
SSM_GROUP = 16
N_SSM_GROUPS = 64
STATE_DIM = 64
N_STATE = N_SSM_GROUPS * STATE_DIM
D_FF = 8192
PROJ_WIDTH = ATTN_WIDTH + 2 * KV_WIDTH + SSM_WIDTH
EPS = 1e-6
NEG_INF = -1e30

LANES = 128
SUBLANES = 8
MXU_DIM = 256
VMEM_LIMIT = 56 * 1024 * 1024

GROUPS_PER_BLOCK = MXU_DIM // SSM_GROUP
N_SSM_BLOCKS = N_SSM_GROUPS // GROUPS_PER_BLOCK
BLOCK_STATES = GROUPS_PER_BLOCK * STATE_DIM
SCAN_LANES = 512
SSM_SUB_TILE_ROWS = 256
SSM_SUB_TILES = 2
SSM_TAIL_LAG_BLOCKS = 4
MLP_ROW_TILE = 512
MLP_FF_TILE = 1024
IN_PROJ_TILE_ROWS = 1024
IN_PROJ_CHUNK_ROWS = 256

F32 = jnp.float32
BF16 = jnp.bfloat16


def _cparams(sem):
    return pltpu.CompilerParams(dimension_semantics=sem, vmem_limit_bytes=VMEM_LIMIT)


def _const_spec(shape):
    nd = len(shape)
    return pl.BlockSpec(shape, lambda *_: (0,) * nd, pipeline_mode=pl.Buffered(1))


def _discretize_kernel(are_ref, aim_ref, ldt_ref, bre_ref, bim_ref, cre_ref, cim_ref,
                       abr_ref, abi_ref, wb_ref, wct_ref):
    a_re = are_ref[...]
    a_im = aim_ref[...]
    dt = jnp.exp(ldt_ref[...])
    mag = jnp.exp(a_re * dt)
    ab_re = mag * jnp.cos(a_im * dt)
    ab_im = mag * jnp.sin(a_im * dt)
    abr_ref[...] = jnp.broadcast_to(ab_re, abr_ref.shape)
    abi_ref[...] = jnp.broadcast_to(ab_im, abi_ref.shape)
    x = ab_re - 1.0
    y = ab_im
    den = a_re * a_re + a_im * a_im
    k_re = (x * a_re + y * a_im) / den
    k_im = (y * a_re - x * a_im) / den
    b_re = bre_ref[...]
    b_im = bim_ref[...]
    bb_re = k_re * b_re - k_im * b_im
    bb_im = k_re * b_im + k_im * b_re

    rows = GROUPS_PER_BLOCK * SSM_GROUP
    row_group = lax.broadcasted_iota(jnp.int32, (rows, BLOCK_STATES), 0) // SSM_GROUP
    col_group = lax.broadcasted_iota(jnp.int32, (rows, BLOCK_STATES), 1) // STATE_DIM
    same_group = row_group == col_group

    def block_diag(m, j):
        blk = m[:, j * BLOCK_STATES:(j + 1) * BLOCK_STATES]
        return jnp.where(same_group, jnp.concatenate([blk] * GROUPS_PER_BLOCK, axis=0), 0.0).astype(BF16)

    c_re = cre_ref[...]
    c_im_neg = -cim_ref[...]
    for j in range(N_SSM_BLOCKS):
        wb_ref[j, :, :BLOCK_STATES] = block_diag(bb_re, j)
        wb_ref[j, :, BLOCK_STATES:] = block_diag(bb_im, j)
        wct_ref[j, :, :BLOCK_STATES] = block_diag(c_re, j)
        wct_ref[j, :, BLOCK_STATES:] = block_diag(c_im_neg, j)


def _ssm_discretize(a_re, a_im, log_dt, b_re, b_im, c_re, c_im):
    rep = jax.ShapeDtypeStruct((SUBLANES, N_STATE), F32)
    blocks = jax.ShapeDtypeStruct((N_SSM_BLOCKS, GROUPS_PER_BLOCK * SSM_GROUP, 2 * BLOCK_STATES), BF16)
    return pl.pallas_call(
        _discretize_kernel,
        out_shape=(rep, rep, blocks, blocks),
        name="ssm_discretize",
    )(a_re, a_im, log_dt, b_re, b_im, c_re, c_im)


def _head_rmsnorm(zc, gain, ones_blk):
    sq = (zc * zc).astype(BF16)
    ss = jnp.dot(sq, ones_blk, preferred_element_type=F32)
    return zc * lax.rsqrt(ss * (1.0 / HEAD_DIM) + EPS) * gain


def _in_proj_kernel(x_ref, gn_ref, w_ref, gq_ref, gk_ref, q_ref, kv_ref, u_ref):
    ri = lax.broadcasted_iota(jnp.int32, (MXU_DIM, MXU_DIM), 0) // HEAD_DIM
    ci = lax.broadcasted_iota(jnp.int32, (MXU_DIM, MXU_DIM), 1) // HEAD_DIM
    ones_blk = (ri == ci).astype(BF16)
    gq = gq_ref[...]
    chunk = IN_PROJ_CHUNK_ROWS
    low = lax.broadcasted_iota(jnp.int32, (chunk, LANES), 1) < HEAD_DIM
    for ck in range(x_ref.shape[0] // chunk):
        rs = slice(ck * chunk, (ck + 1) * chunk)
        x = x_ref[rs, :]
        r = lax.rsqrt(jnp.mean(x * x, axis=-1, keepdims=True) + EPS)
        xn = (x * r * gn_ref[...]).astype(BF16)
        zq = jnp.dot(xn, w_ref[:, :ATTN_WIDTH], preferred_element_type=F32)
        zr = jnp.dot(xn, w_ref[:, ATTN_WIDTH:], preferred_element_type=F32)
        lane_blocks = []
        for c in range(ATTN_WIDTH // MXU_DIM):
            qn = _head_rmsnorm(zq[:, c * MXU_DIM:(c + 1) * MXU_DIM], gq, ones_blk) * (HEAD_DIM ** -0.5)
            lane_blocks += [qn[:, :LANES], qn[:, LANES:]]
        for pair in range(N_KV_HEADS // 2):
            for r in range(Q_PER_KV):
                first = lane_blocks[((2 * pair) * Q_PER_KV + r) // 2]
                second = lane_blocks[((2 * pair + 1) * Q_PER_KV + r) // 2]
                if r % 2 == 0:
                    blk = jnp.where(low, first, pltpu.roll(second, HEAD_DIM, axis=1))
                else:
                    blk = jnp.where(low, pltpu.roll(first, HEAD_DIM, axis=1), second)
                dst = pair * Q_PER_KV + r
                q_ref[rs, dst * LANES:(dst + 1) * LANES] = blk.astype(BF16)
        kv_ref[rs, :KV_WIDTH] = _head_rmsnorm(zr[:, :KV_WIDTH], gk_ref[...], ones_blk)
        kv_ref[rs, KV_WIDTH:] = zr[:, KV_WIDTH:2 * KV_WIDTH]
        u_ref[rs, :] = zr[:, 2 * KV_WIDTH:]


def _paired_head_slab(i):
    g, r = i // Q_PER_KV, i % Q_PER_KV
    paired = (g // 2) * (2 * Q_PER_KV) + r * 2 + g % 2
    return jnp.where(i < N_HEADS, paired, i)


def _in_proj(x2d, gn, w_in_b, gq, gk):
    rows = x2d.shape[0]
    tm = min(rows, IN_PROJ_TILE_ROWS)
    assert rows % tm == 0 and tm % IN_PROJ_CHUNK_ROWS == 0
    return pl.pallas_call(
        _in_proj_kernel,
        grid=(rows // tm,),
        in_specs=[
            pl.BlockSpec((tm, D_MODEL), lambda i: (i, 0)),
            _const_spec((1, D_MODEL)),
            _const_spec((D_MODEL, PROJ_WIDTH)),
            _const_spec((1, MXU_DIM)),
            _const_spec((1, MXU_DIM)),
        ],
        out_specs=[
            pl.BlockSpec((tm, ATTN_WIDTH), lambda i: (i, 0)),
            pl.BlockSpec((tm, 2 * KV_WIDTH), lambda i: (i, 0)),
            pl.BlockSpec((tm, SSM_WIDTH), lambda i: (i, 0)),
        ],
        out_shape=(
            jax.ShapeDtypeStruct((rows, ATTN_WIDTH), BF16),
            jax.ShapeDtypeStruct((rows, 2 * KV_WIDTH), F32),
            jax.ShapeDtypeStruct((rows, SSM_WIDTH), F32),
        ),
        compiler_params=_cparams(("parallel",)),
        name="in_proj",
    )(x2d, gn, w_in_b, gq, gk)


def _alibi_slope(head):
    return 2.0 ** (-8.0 * (head + 1) / N_HEADS)


ATTN_BLOCKS_PER_STEP = 4


def _prompt_attn_kernel(sink_ref, q_ref, kvp_ref, kvc_ref, bias_ref, *rest):
    n_cast = (len(rest) - 1) // 2
    o_ref = rest[n_cast]
    for src, dst in zip(rest[:n_cast], rest[n_cast + 1:]):
        dst[...] = src[...].astype(BF16)
    tq = WINDOW
    lane = lax.broadcasted_iota(jnp.int32, (tq, LANES), 1)
    low = lane < HEAD_DIM
    zero = jnp.zeros((tq, LANES), BF16)

    def keys_or_values(blk, lanes):
        own = kvc_ref[blk * tq:(blk + 1) * tq, lanes]
        prev = kvp_ref[:, lanes] if blk == 0 else kvc_ref[(blk - 1) * tq:blk * tq, lanes]
        return jnp.concatenate([prev, own], axis=0).astype(BF16)

    for blk in range(q_ref.shape[0] // tq):
        rows = slice(blk * tq, (blk + 1) * tq)
        has_prev = (pl.program_id(1) > 0).astype(jnp.int32) if blk == 0 else 1
        for pair in range(N_KV_HEADS // 2):
            kc = keys_or_values(blk, slice(pair * LANES, (pair + 1) * LANES))
            vc = keys_or_values(blk, slice(KV_WIDTH + pair * LANES, KV_WIDTH + (pair + 1) * LANES))
            qb = [q_ref[rows, (pair * Q_PER_KV + r) * LANES:(pair * Q_PER_KV + r + 1) * LANES]
                  for r in range(Q_PER_KV)]
            qs = jnp.concatenate([jnp.where(low, b, zero) for b in qb] + [jnp.where(low, zero, b) for b in qb],
                                 axis=0)
            scores = lax.dot_general(qs, kc, (((1,), (1,)), ((), ())), preferred_element_type=F32)
            probs = []
            inv_den = []
            for hh in range(2 * Q_PER_KV):
                head = (2 * pair + hh // Q_PER_KV) * Q_PER_KV + hh % Q_PER_KV
                s = scores[hh * tq:(hh + 1) * tq] + bias_ref[has_prev, head]
                sink = sink_ref[head]
                m = jnp.maximum(jnp.max(s, axis=-1, keepdims=True), sink)
                p = jnp.exp(s - m)
                den = jnp.sum(p, axis=-1, keepdims=True) + jnp.exp(sink - m)
                probs.append(p.astype(BF16))
                inv_den.append(1.0 / den)
            pv = jnp.dot(jnp.concatenate(probs, axis=0), vc, preferred_element_type=F32)
            for r in range(Q_PER_KV):
                o_low = pv[r * tq:(r + 1) * tq] * inv_den[r]
                o_high = pv[(Q_PER_KV + r) * tq:(Q_PER_KV + r + 1) * tq] * inv_den[Q_PER_KV + r]
                o_ref[rows, (pair * Q_PER_KV + r) * LANES:(pair * Q_PER_KV + r + 1) * LANES] = (
                    jnp.where(low, o_low, o_high).astype(BF16))


def _prompt_attn_bias():
    w = WINDOW
    slope = jnp.exp2(-8.0 * (jnp.arange(N_HEADS, dtype=F32) + 1.0) / N_HEADS)
    i = jnp.arange(w)[:, None]
    j = jnp.arange(2 * w)[None, :]
    d = i + w - j
    ok = (d >= 0) & (d <= w)
    ok = jnp.stack([ok & (j >= w), ok])
    return jnp.where(ok[:, None], -slope[None, :, None, None] * d.astype(F32)[None, None], NEG_INF)


def _prompt_attention(sinks, q, kv, batch, seq, to_bf16=()):
    per = ATTN_BLOCKS_PER_STEP
    tq = per * WINDOW
    nb = seq // tq
    steps = batch * nb
    bias = _prompt_attn_bias()
    mats = [w for w, _ in to_bf16]
    assert all(w.shape[0] % (steps * 2 * SUBLANES) == 0 for w in mats)
    slab_in = [pl.BlockSpec((w.shape[0] // steps, w.shape[1]), lambda b, n: (b * nb + n, 0)) for w in mats]
    slab_out = [pl.BlockSpec((w.shape[0] // steps, w.shape[1]),
                             (lambda b, n: (b * nb + n, 0)) if place is None
                             else (lambda b, n, place=place: (place(b * nb + n), 0)))
                for w, place in to_bf16]
    return pl.pallas_call(
        _prompt_attn_kernel,
        grid=(batch, nb),
        in_specs=[
            pl.BlockSpec(memory_space=pltpu.SMEM),
            pl.BlockSpec((tq, ATTN_WIDTH), lambda b, n: (b * nb + n, 0)),
            pl.BlockSpec((WINDOW, 2 * KV_WIDTH), lambda b, n: (b * nb * per + jnp.maximum(n * per - 1, 0), 0)),
            pl.BlockSpec((tq, 2 * KV_WIDTH), lambda b, n: (b * nb + n, 0)),
            _const_spec(bias.shape),
        ] + slab_in,
        out_specs=[pl.BlockSpec((tq, ATTN_WIDTH), lambda b, n: (b * nb + n, 0))] + slab_out,
        out_shape=(jax.ShapeDtypeStruct((batch * seq, ATTN_WIDTH), BF16),)
        + tuple(jax.ShapeDtypeStruct(w.shape, BF16) for w in mats),
        compiler_params=_cparams(("arbitrary", "arbitrary")),
        name="prompt_attention",
    )(sinks, q, kv, kv, bias, *mats)


SAMPLE_ATTN_BATCH = 16
SAMPLE_ATTN_GROUP = 2


def _sample_attn_kernel(sink_ref, q_ref, ckt_ref, cvt_ref, kvn_ref, o_ref, nkt_ref, nvt_ref, *, t_new):
    rows = Q_PER_KV * t_new
    tk = 2 * WINDOW
    i = lax.broadcasted_iota(jnp.int32, (rows, tk), 0) % t_new
    c = lax.broadcasted_iota(jnp.int32, (rows, tk), 1)
    is_new = c >= tk - t_new
    j = jnp.where(c < WINDOW, c, c - (WINDOW - t_new))
    d = i + WINDOW - j
    valid = (d >= 0) & (d <= WINDOW) & ((c < WINDOW) | is_new)
    delta = d.astype(F32)
    lane_q = lax.broadcasted_iota(jnp.int32, (rows, LANES), 1)
    low = lane_q < HEAD_DIM
    zero_q = jnp.zeros((rows, LANES), BF16)
    lane_w = lax.broadcasted_iota(jnp.int32, (WINDOW, LANES), 1)
    tail = lane_w >= WINDOW - t_new
    rid = lax.broadcasted_iota(jnp.int32, (rows, 1), 0) // t_new
    alibi, sinks = [], []
    for pair in range(N_KV_HEADS // 2):
        sl, sk = [], []
        for half in range(2):
            slope = jnp.zeros((rows, 1), F32)
            sink = jnp.zeros((rows, 1), F32)
            for r in range(Q_PER_KV):
                head = (2 * pair + half) * Q_PER_KV + r
                slope = jnp.where(rid == r, _alibi_slope(head), slope)
                sink = jnp.where(rid == r, sink_ref[head], sink)
            sl.append(slope)
            sk.append(sink)
        alibi.append(jnp.concatenate(sl, axis=0) * jnp.concatenate([delta, delta], axis=0))
        sinks.append(jnp.concatenate(sk, axis=0))
    valid2 = jnp.concatenate([valid, valid], axis=0)

    lead_zero = jnp.zeros((WINDOW - SUBLANES, LANES), F32)

    def new_rows_transposed(e, lanes):
        return jnp.concatenate([lead_zero, kvn_ref[e, :, lanes]], axis=0).T

    def shifted(old_t, new_t):
        return jnp.where(tail, new_t, pltpu.roll(old_t, WINDOW - t_new, axis=1))

    def keys_and_scores(e, pair):
        heads = slice(2 * pair, 2 * pair + 2)
        kt = ckt_ref[e, heads].reshape(2 * HEAD_DIM, WINDOW)
        knt = new_rows_transposed(e, slice(pair * LANES, (pair + 1) * LANES))
        nkt_ref[e, heads] = shifted(kt, knt).reshape(2, HEAD_DIM, WINDOW)
        qp = q_ref[e, pair * rows:(pair + 1) * rows, :]
        qs = jnp.concatenate([jnp.where(low, qp, zero_q), jnp.where(low, zero_q, qp)], axis=0)
        kt_all = jnp.concatenate([kt, knt], axis=1).astype(BF16)
        return jnp.dot(qs, kt_all, preferred_element_type=F32)

    def values(e, pair):
        heads = slice(2 * pair, 2 * pair + 2)
        vt = cvt_ref[e, heads].reshape(2 * HEAD_DIM, WINDOW)
        vnt = new_rows_transposed(e, slice(KV_WIDTH + pair * LANES, KV_WIDTH + (pair + 1) * LANES))
        nvt_ref[e, heads] = shifted(vt, vnt).reshape(2, HEAD_DIM, WINDOW)
        return jnp.concatenate([vt, vnt], axis=1).astype(BF16)

    def softmax(pair, scores):
        s = jnp.where(valid2, scores - alibi[pair], NEG_INF)
        m = jnp.maximum(jnp.max(s, axis=-1, keepdims=True), sinks[pair])
        p = jnp.exp(s - m)
        den = jnp.sum(p, axis=-1, keepdims=True) + jnp.exp(sinks[pair] - m)
        return p.astype(BF16), 1.0 / den

    def body(i, carry):
        group = [(i * SAMPLE_ATTN_GROUP + k, pair) for k in range(SAMPLE_ATTN_GROUP) for pair in range(N_KV_HEADS // 2)]
        scores = [keys_and_scores(e, pair) for e, pair in group]
        vt_all = [values(e, pair) for e, pair in group]
        probs = [softmax(pair, sc) for (e, pair), sc in zip(group, scores)]
        for (e, pair), (p, inv_den), vt in zip(group, probs, vt_all):
            pv = lax.dot_general(p, vt, (((1,), (1,)), ((), ())), preferred_element_type=F32)
            pv = pv * inv_den
            o_ref[e, pair * rows:(pair + 1) * rows, :] = jnp.where(low, pv[:rows], pv[rows:]).astype(BF16)
        return carry

    lax.fori_loop(0, q_ref.shape[0] // SAMPLE_ATTN_GROUP, body, 0)


def _sample_attention(sinks, q3, ckt, cvt, kvn, t_new):
    n = q3.shape[0]
    bn = SAMPLE_ATTN_BATCH
    qrows = q3.shape[1]
    cache_spec = pl.BlockSpec((bn, N_KV_HEADS, HEAD_DIM, WINDOW), lambda b: (b, 0, 0, 0))
    cache_shape = jax.ShapeDtypeStruct((n, N_KV_HEADS, HEAD_DIM, WINDOW), F32)
    return pl.pallas_call(
        functools.partial(_sample_attn_kernel, t_new=t_new),
        grid=(n // bn,),
        in_specs=[
            pl.BlockSpec(memory_space=pltpu.SMEM),
            pl.BlockSpec((bn, qrows, LANES), lambda b: (b, 0, 0)),
            cache_spec,
            cache_spec,
            pl.BlockSpec((bn, SUBLANES, 2 * KV_WIDTH), lambda b: (b, 0, 0)),
        ],
        out_specs=[pl.BlockSpec((bn, qrows, LANES), lambda b: (b, 0, 0)), cache_spec, cache_spec],
        out_shape=(jax.ShapeDtypeStruct((n, qrows, LANES), BF16), cache_shape, cache_shape),
        compiler_params=_cparams(("parallel",)),
        name="sample_attention",
    )(sinks, q3, ckt, cvt, kvn)


def _gelu_tanh(x):
    c = math.sqrt(2.0 / math.pi)
    return 0.5 * x * (1.0 + jnp.tanh(c * (x + 0.044715 * (x * x * x))))


def _ssm_kernel(u_ref, h0re_ref, h0im_ref, ar_ref, ai_ref, wb_ref, wct_ref, d_ref, wglu_ref, bglu_ref, gs_ref,
                *rest, nb, steps, n_sub):
    n_cast = (len(rest) - 4) // 2
    mix_ref, hre_ref, him_ref = rest[n_cast:n_cast + 3]
    s_ref = rest[-1]
    for src, dst in zip(rest[:n_cast], rest[n_cast + 3:-1]):
        dst[...] = src[...].astype(BF16)

    @pl.when(pl.program_id(0) == 0)
    def _():
        hre_ref[...] = h0re_ref[...]
        him_ref[...] = h0im_ref[...]

    dims = dict(nb=nb, steps=steps)
    pending = []

    def trace_pending_tails():
        while pending:
            y, s = pending.pop(0)
            _ssm_gate_and_store(y, wglu_ref, bglu_ref, gs_ref, mix_ref, sub=s, **dims)

    for sub in range(n_sub):
        u, ub = _ssm_permute_in(u_ref, sub=sub, **dims)
        ys = []
        for j in range(N_SSM_BLOCKS):
            if j == SSM_TAIL_LAG_BLOCKS:
                trace_pending_tails()
            ys.append(_ssm_state_block(ub, j, ar_ref, ai_ref, wb_ref, wct_ref, hre_ref, him_ref, s_ref, sub=sub, **dims))
        trace_pending_tails()
        pending.append((jnp.concatenate(ys, axis=1) + d_ref[...] * u, sub))
    trace_pending_tails()


def _ssm_positions(ref, nb, steps, sub):
    per_sub = nb * steps // ref.shape[0]
    return slice(sub * per_sub, (sub + 1) * per_sub)


def _ssm_permute_in(u_ref, *, nb, steps, sub):
    rows = nb * steps
    tm_row = lax.broadcasted_iota(jnp.int32, (rows, rows), 0)
    bm_col = lax.broadcasted_iota(jnp.int32, (rows, rows), 1)
    to_time_major = (bm_col == (tm_row % nb) * steps + tm_row // nb).astype(BF16)
    u_bm = u_ref[:, _ssm_positions(u_ref, nb, steps, sub), :].reshape(rows, SSM_WIDTH)
    u_hi = u_bm.astype(BF16)
    u_lo = (u_bm - u_hi.astype(F32)).astype(BF16)
    u_hi_tm = jnp.dot(to_time_major, u_hi, preferred_element_type=F32)
    u = u_hi_tm + jnp.dot(to_time_major, u_lo, preferred_element_type=F32)
    return u, u_hi_tm.astype(BF16)


def _ssm_state_block(ub, j, ar_ref, ai_ref, wb_ref, wct_ref, hre_ref, him_ref, s_ref, *, nb, steps, sub):
    rows = nb * steps
    row0 = (sub % (s_ref.shape[0] // rows)) * rows
    blk_cols = 2 * BLOCK_STATES
    s_ref[row0:row0 + rows, j * blk_cols:(j + 1) * blk_cols] = jnp.dot(
        ub[:, j * MXU_DIM:(j + 1) * MXU_DIM], wb_ref[j], preferred_element_type=F32)
    for part in range(BLOCK_STATES // SCAN_LANES):
        rc = j * blk_cols + part * SCAN_LANES
        ic = rc + BLOCK_STATES
        sc = j * BLOCK_STATES + part * SCAN_LANES
        a_re = ar_ref[:, sc:sc + SCAN_LANES]
        a_im = ai_ref[:, sc:sc + SCAN_LANES]
        for bg in range(nb // SUBLANES):
            b0 = bg * SUBLANES
            h_re = hre_ref[b0:b0 + SUBLANES, sc:sc + SCAN_LANES]
            h_im = him_ref[b0:b0 + SUBLANES, sc:sc + SCAN_LANES]
            for t in range(steps):
                row = row0 + t * nb + b0
                n_re = a_re * h_re - a_im * h_im + s_ref[row:row + SUBLANES, rc:rc + SCAN_LANES]
                n_im = a_re * h_im + a_im * h_re + s_ref[row:row + SUBLANES, ic:ic + SCAN_LANES]
                s_ref[row:row + SUBLANES, rc:rc + SCAN_LANES] = n_re
                s_ref[row:row + SUBLANES, ic:ic + SCAN_LANES] = n_im
                h_re, h_im = n_re, n_im
            hre_ref[b0:b0 + SUBLANES, sc:sc + SCAN_LANES] = h_re
            him_ref[b0:b0 + SUBLANES, sc:sc + SCAN_LANES] = h_im
    hb = s_ref[row0:row0 + rows, j * blk_cols:(j + 1) * blk_cols].astype(BF16)
    return lax.dot_general(hb, wct_ref[j], (((1,), (1,)), ((), ())), preferred_element_type=F32)


def _ssm_gate_and_store(y, wglu_ref, bglu_ref, gs_ref, mix_ref, *, nb, steps, sub):
    rows = nb * steps
    bm_row = lax.broadcasted_iota(jnp.int32, (rows, rows), 0)
    tm_col = lax.broadcasted_iota(jnp.int32, (rows, rows), 1)
    to_batch_major = (bm_row == (tm_col % nb) * steps + tm_col // nb).astype(BF16)
    g = _gelu_tanh(y)
    gate = jnp.dot(g.astype(BF16), wglu_ref[...], preferred_element_type=F32) + bglu_ref[...]
    so = g * jax.nn.sigmoid(gate)
    r = lax.rsqrt(jnp.mean(so * so, axis=-1, keepdims=True) + EPS)
    mix_tm = (so * r * gs_ref[...]).astype(BF16)
    mix_bm = jnp.dot(to_batch_major, mix_tm, preferred_element_type=F32).astype(BF16)
    mix_ref[:, _ssm_positions(mix_ref, nb, steps, sub), :] = mix_bm.reshape(
        mix_ref.shape[0], rows // mix_ref.shape[0], SSM_WIDTH)


def _weight_slab_specs(to_bf16, steps):
    mats = [w for w, _ in to_bf16]
    assert all(w.shape[0] % (steps * 2 * SUBLANES) == 0 for w in mats)
    slab_in = [pl.BlockSpec((w.shape[0] // steps, w.shape[1]), lambda i: (i, 0)) for w in mats]
    slab_out = [pl.BlockSpec((w.shape[0] // steps, w.shape[1]),
                             (lambda i: (i, 0)) if place is None else (lambda i, place=place: (place(i), 0)))
                for w, place in to_bf16]
    return mats, slab_in, slab_out, tuple(jax.ShapeDtypeStruct(w.shape, BF16) for w in mats)


def _ssm_mixer(u3, h0_re, h0_im, ar8, ai8, wb, wct, d, wglu_b, bglu, gs, nb, steps, n_sub, to_bf16=()):
    nbv, seq, _ = u3.shape
    blk_rows = n_sub * nb * steps // nbv
    assert blk_rows % SUBLANES == 0 and seq % blk_rows == 0
    cols = 2 * N_STATE
    tr = min(n_sub, 2) * nb * steps
    mats, slab_in, slab_out, slab_shapes = _weight_slab_specs(to_bf16, seq // blk_rows)
    return pl.pallas_call(
        functools.partial(_ssm_kernel, nb=nb, steps=steps, n_sub=n_sub),
        grid=(seq // blk_rows,),
        in_specs=[
            pl.BlockSpec((nbv, blk_rows, SSM_WIDTH), lambda i: (0, i, 0)),
            _const_spec((nb, N_STATE)),
            _const_spec((nb, N_STATE)),
            _const_spec((SUBLANES, N_STATE)),
            _const_spec((SUBLANES, N_STATE)),
            _const_spec((N_SSM_BLOCKS, MXU_DIM, 2 * BLOCK_STATES)),
            _const_spec((N_SSM_BLOCKS, MXU_DIM, 2 * BLOCK_STATES)),
            _const_spec((1, SSM_WIDTH)),
            _const_spec((SSM_WIDTH, SSM_WIDTH)),
            _const_spec((1, SSM_WIDTH)),
            _const_spec((1, SSM_WIDTH)),
        ] + slab_in,
        out_specs=[
            pl.BlockSpec((nbv, blk_rows, SSM_WIDTH), lambda i: (0, i, 0)),
            pl.BlockSpec((nb, N_STATE), lambda i: (0, 0)),
            pl.BlockSpec((nb, N_STATE), lambda i: (0, 0)),
        ] + slab_out,
        out_shape=(
            jax.ShapeDtypeStruct((nbv, seq, SSM_WIDTH), BF16),
            jax.ShapeDtypeStruct((nb, N_STATE), F32),
            jax.ShapeDtypeStruct((nb, N_STATE), F32),
        ) + slab_shapes,
        scratch_shapes=[pltpu.VMEM((tr, cols), F32)],
        compiler_params=_cparams(("arbitrary",)),
        name="ssm_mixer",
    )(u3, h0_re, h0_im, ar8, ai8, wb, wct, d, wglu_b, bglu, gs, *mats)


def _out_mlp_kernel(x_ref, a_ref, ga_ref, ms_ref, wo_ref, gm_ref, wup_ref, wdn_ref, y_ref, hn_ref):
    @pl.when(pl.program_id(1) == 0)
    def _():
        half_rows = x_ref.shape[0] // 2
        for half in range(2):
            rs = slice(half * half_rows, (half + 1) * half_rows)
            h = x_ref[rs, :] + jnp.dot(ms_ref[rs, :], wo_ref[ATTN_WIDTH:, :], preferred_element_type=F32)
            a = a_ref[rs, :].astype(F32)
            ra = lax.rsqrt(jnp.mean(a * a, axis=-1, keepdims=True) + EPS)
            ma = (a * ra * ga_ref[...]).astype(BF16)
            h = h + jnp.dot(ma, wo_ref[:ATTN_WIDTH, :], preferred_element_type=F32)
            y_ref[rs, :] = h
            rh = lax.rsqrt(jnp.mean(h * h, axis=-1, keepdims=True) + EPS)
            hn_ref[rs, :] = (h * rh * gm_ref[...]).astype(BF16)

    t = jnp.dot(hn_ref[...], wup_ref[...], preferred_element_type=F32)
    t = jnp.maximum(t, 0.0)
    t = (t * t).astype(BF16)
    y_ref[...] += jnp.dot(t, wdn_ref[...], preferred_element_type=F32)


def _out_mlp(x2d, attn, ga, mix_s, wo, gm, wup, wdn, tm, tf):
    rows = x2d.shape[0]
    n_ff = D_FF // tf

    def ff_tile(i, j):
        return jnp.where(i % 2 == 0, j, n_ff - 1 - j)

    return pl.pallas_call(
        _out_mlp_kernel,
        grid=(rows // tm, n_ff),
        in_specs=[
            pl.BlockSpec((tm, D_MODEL), lambda i, j: (i, 0)),
            pl.BlockSpec((tm, ATTN_WIDTH), lambda i, j: (i, 0)),
            _const_spec((1, ATTN_WIDTH)),
            pl.BlockSpec((tm, SSM_WIDTH), lambda i, j: (i, 0)),
            _const_spec((D_MODEL, D_MODEL)),
            _const_spec((1, D_MODEL)),
            pl.BlockSpec((D_MODEL, tf), lambda i, j: (0, ff_tile(i, j))),
            pl.BlockSpec((tf, D_MODEL), lambda i, j: (ff_tile(i, j), 0)),
        ],
        out_specs=pl.BlockSpec((tm, D_MODEL), lambda i, j: (i, 0)),
        out_shape=jax.ShapeDtypeStruct((rows, D_MODEL), F32),
        scratch_shapes=[pltpu.VMEM((tm, D_MODEL), BF16)],
        compiler_params=_cparams(("parallel", "arbitrary")),
        name="out_mlp",
    )(x2d, attn, ga, mix_s, wo, gm, wup, wdn)


def _pair_heads(a, axis):
    shape = a.shape
    split = shape[:axis] + (N_KV_HEADS // 2, 2, Q_PER_KV, HEAD_DIM) + shape[axis + 1:]
    return jnp.swapaxes(a.reshape(split), axis + 1, axis + 2).reshape(shape)


def _layer(x, cache_k, cache_v, h0_re, h0_im, p, side):
    n, t = x.shape[:2]
    rows = n * t
    x2d = x.reshape(rows, D_MODEL)
    q, kv, u = _in_proj(x2d, p['gn'], p['w_in'], p['gq'], p['gk'])
    kv3 = kv.reshape(n, t, 2 * KV_WIDTH)

    if cache_k is None:
        assert side is None
        attn, w_glu_b = _prompt_attention(p['sinks'], q, kv, n, t, to_bf16=((p['w_glu_f32'], None),))
        tail = kv3[:, t - WINDOW:]
        new_k = tail[..., :KV_WIDTH].reshape(n, WINDOW, N_KV_HEADS, HEAD_DIM)
        new_v = tail[..., KV_WIDTH:].reshape(n, WINDOW, N_KV_HEADS, HEAD_DIM)
        u3 = u.reshape(n, t, SSM_WIDTH)
        ssm_steps, ssm_subs = SSM_SUB_TILE_ROWS // n, SSM_SUB_TILES
        assert D_MODEL // (t // (ssm_steps * ssm_subs)) == HEAD_DIM
        ssm_casts = ((p['w_up_f32'], None), (p['w_down_f32'], None), (p['w_out_f32'], _paired_head_slab))
    else:
        w_up, w_down, w_out_b, w_glu_b = side
        ssm_casts = ()
        nblk = ATTN_WIDTH // LANES
        q3 = q.reshape(n, t, nblk, LANES).transpose(0, 2, 1, 3).reshape(n, nblk * t, LANES)
        kvn = jnp.pad(kv3, ((0, 0), (SUBLANES - t, 0), (0, 0)))
        ckt = cache_k.transpose(0, 2, 3, 1)
        cvt = cache_v.transpose(0, 2, 3, 1)
        a3, nkt, nvt = _sample_attention(p['sinks'], q3, ckt, cvt, kvn, t)
        attn = a3.reshape(n, nblk, t, LANES).transpose(0, 2, 1, 3).reshape(rows, ATTN_WIDTH)
        new_k = nkt.transpose(0, 3, 1, 2)
        new_v = nvt.transpose(0, 3, 1, 2)
        u3 = u.reshape(1, rows, SSM_WIDTH)
        ssm_steps, ssm_subs = t, 1

    mix3, h_re, h_im, *cast = _ssm_mixer(u3, h0_re.reshape(n, N_STATE), h0_im.reshape(n, N_STATE), p['ar8'],
                                         p['ai8'], p['wb'], p['wct'], p['d'], w_glu_b, p['b_glu'], p['gs'],
                                         n, ssm_steps, ssm_subs, to_bf16=ssm_casts)
    if cast:
        w_up, w_down, w_out_b = cast
        side = (w_up, w_down, w_out_b, w_glu_b)
    mix_s = mix3.reshape(rows, SSM_WIDTH)
    h_re = h_re.reshape(n, N_SSM_GROUPS, STATE_DIM)
    h_im = h_im.reshape(n, N_SSM_GROUPS, STATE_DIM)

    y = _out_mlp(x2d, attn, p['ga'], mix_s, w_out_b, p['gm'], w_up, w_down, MLP_ROW_TILE, MLP_FF_TILE)
    return y.reshape(n, t, D_MODEL), new_k, new_v, h_re, h_im, side


def _prepare_params(l, attn_norm_g, w_in, q_norm_g, k_norm_g, attn_sinks,
                    ssm_A_re, ssm_A_im, ssm_log_dt, ssm_B_re, ssm_B_im, ssm_C_re, ssm_C_im, ssm_D,
                    w_glu, b_glu, attn_out_g, ssm_out_g, w_out, mlp_norm_g, w_mlp_up, w_mlp_down):
    w_in_l = w_in[l]
    heads_per_blk = MXU_DIM // HEAD_DIM

    a_re = ssm_A_re[l].reshape(1, N_STATE)
    a_im = ssm_A_im[l].reshape(1, N_STATE)
    ldt = jnp.broadcast_to(ssm_log_dt[l][:, None], (N_SSM_GROUPS, STATE_DIM)).reshape(1, N_STATE)
    b_re = ssm_B_re[l].transpose(2, 0, 1).reshape(SSM_GROUP, N_STATE)
    b_im = ssm_B_im[l].transpose(2, 0, 1).reshape(SSM_GROUP, N_STATE)
    c_re = ssm_C_re[l].transpose(1, 0, 2).reshape(SSM_GROUP, N_STATE)
    c_im = ssm_C_im[l].transpose(1, 0, 2).reshape(SSM_GROUP, N_STATE)
    ar8, ai8, wb, wct = _ssm_discretize(a_re, a_im, ldt, b_re, b_im, c_re, c_im)

    return dict(
        gn=attn_norm_g[l].reshape(1, D_MODEL),
        w_in=w_in_l.astype(BF16),
        gq=jnp.tile(q_norm_g[l], heads_per_blk).reshape(1, MXU_DIM),
        gk=jnp.tile(k_norm_g[l], heads_per_blk).reshape(1, MXU_DIM),
        sinks=attn_sinks[l].astype(F32),
        ar8=ar8, ai8=ai8, wb=wb, wct=wct,
        d=ssm_D[l].reshape(1, SSM_WIDTH),
        w_glu_f32=w_glu[l],
        b_glu=b_glu[l].reshape(1, SSM_WIDTH),
        gs=ssm_out_g[l].reshape(1, SSM_WIDTH),
        ga=_pair_heads(attn_out_g[l], 0).reshape(1, ATTN_WIDTH),
        w_out_f32=w_out[l],
        gm=mlp_norm_g[l].reshape(1, D_MODEL),
        w_up_f32=w_mlp_up[l],
        w_down_f32=w_mlp_down[l],
    )


def kernel(x_prompt, x_sample, cache_k, cache_v, state_ssm_re, state_ssm_im, attn_norm_g, w_in, q_norm_g, k_norm_g, attn_sinks, ssm_A_re, ssm_A_im, ssm_log_dt, ssm_B_re, ssm_B_im, ssm_C_re, ssm_C_im, ssm_D, w_glu, b_glu, attn_out_g, ssm_out_g, w_out, mlp_norm_g, w_mlp_up, w_mlp_down):
    depth = w_in.shape[0]
    xp, xs = x_prompt, x_sample
    zeros_state = jnp.zeros((x_prompt.shape[0], N_SSM_GROUPS, STATE_DIM), F32)
    outs = [[] for _ in range(8)]
    for l in range(depth):
        p = _prepare_params(l, attn_norm_g, w_in, q_norm_g, k_norm_g, attn_sinks,
                            ssm_A_re, ssm_A_im, ssm_log_dt, ssm_B_re, ssm_B_im, ssm_C_re, ssm_C_im, ssm_D,
                            w_glu, b_glu, attn_out_g, ssm_out_g, w_out, mlp_norm_g, w_mlp_up, w_mlp_down)
        xp, kp, vp, hrp, hip, side = _layer(xp, None, None, zeros_state, zeros_state, p, None)
        xs, ks, vs, hrs, his, _ = _layer(xs, cache_k[l], cache_v[l], state_ssm_re[l], state_ssm_im[l], p, side)
        for lst, val in zip(outs, (kp, vp, hrp, hip, ks, vs, hrs, his)):
            lst.append(val)
    return (xp, xs) + tuple(jnp.stack(o) for o in outs)
```

```python
import functools
import math

import jax
import jax.numpy as jnp
from jax import lax
from jax.experimental import pallas as pl
from jax.experimental.pallas import tpu as pltpu

D_MODEL = 2048
ATTN_WIDTH = 1024
SSM_WIDTH = 1024
HEAD_DIM = 64
N_HEADS = 16
N_KV_HEADS = 4
Q_PER_KV = 4
KV_WIDTH = 256
WINDOW = 128
SSM_GROUP = 16
N_SSM_GROUPS = 64
STATE_DIM = 64
N_STATE = N_SSM_GROUPS * STATE_DIM
D_FF = 8192
PROJ_WIDTH = ATTN_WIDTH + 2 * KV_WIDTH + SSM_WIDTH
EPS = 1e-6
NEG_INF = -1e30

LANES = 128
SUBLANES = 8
MXU_DIM = 256
VMEM_LIMIT = 56 * 1024 * 1024

GROUPS_PER_BLOCK = MXU_DIM // SSM_GROUP
N_SSM_BLOCKS = N_SSM_GROUPS // GROUPS_PER_BLOCK
BLOCK_STATES = GROUPS_PER_BLOCK * STATE_DIM
SCAN_LANES = 512
SSM_SUB_TILE_ROWS = 256
SSM_SUB_TILES = 2
SSM_TAIL_LAG_BLOCKS = 4
ROW_TILE = 512
MLP_FF_TILE = 1024
IN_PROJ_ROW_CHUNKS = 2

F32 = jnp.float32
BF16 = jnp.bfloat16


def _cparams(sem):
    return pltpu.CompilerParams(dimension_semantics=sem, vmem_limit_bytes=VMEM_LIMIT)


def _const_spec(shape):
    nd = len(shape)
    return pl.BlockSpec(shape, lambda *_: (0,) * nd, pipeline_mode=pl.Buffered(1))


def _discretize_kernel(are_ref, aim_ref, ldt_ref, bre_ref, bim_ref, cre_ref, cim_ref,
                       abr_ref, abi_ref, wb_ref, wct_ref):
    a_re = are_ref[...]
    a_im = aim_ref[...]
    dt = jnp.exp(ldt_ref[...])
    mag = jnp.exp(a_re * dt)
    ab_re = mag * jnp.cos(a_im * dt)
    ab_im = mag * jnp.sin(a_im * dt)
    abr_ref[...] = jnp.broadcast_to(ab_re, abr_ref.shape)
    abi_ref[...] = jnp.broadcast_to(ab_im, abi_ref.shape)
    x = ab_re - 1.0
    y = ab_im
    den = a_re * a_re + a_im * a_im
    k_re = (x * a_re + y * a_im) / den
    k_im = (y * a_re - x * a_im) / den
    b_re = bre_ref[...]
    b_im = bim_ref[...]
    bb_re = k_re * b_re - k_im * b_im
    bb_im = k_re * b_im + k_im * b_re

    rows = GROUPS_PER_BLOCK * SSM_GROUP
    row_group = lax.broadcasted_iota(jnp.int32, (rows, BLOCK_STATES), 0) // SSM_GROUP
    col_group = lax.broadcasted_iota(jnp.int32, (rows, BLOCK_STATES), 1) // STATE_DIM
    same_group = row_group == col_group

    def block_diag(m, j):
        blk = m[:, j * BLOCK_STATES:(j + 1) * BLOCK_STATES]
        return jnp.where(same_group, jnp.concatenate([blk] * GROUPS_PER_BLOCK, axis=0), 0.0).astype(BF16)

    c_re = cre_ref[...]
    c_im_neg = -cim_ref[...]
    for j in range(N_SSM_BLOCKS):
        wb_ref[j, :, :BLOCK_STATES] = block_diag(bb_re, j)
        wb_ref[j, :, BLOCK_STATES:] = block_diag(bb_im, j)
        wct_ref[j, :, :BLOCK_STATES] = block_diag(c_re, j)
        wct_ref[j, :, BLOCK_STATES:] = block_diag(c_im_neg, j)


def _ssm_discretize(a_re, a_im, log_dt, b_re, b_im, c_re, c_im):
    rep = jax.ShapeDtypeStruct((SUBLANES, N_STATE), F32)
    blocks = jax.ShapeDtypeStruct((N_SSM_BLOCKS, GROUPS_PER_BLOCK * SSM_GROUP, 2 * BLOCK_STATES), BF16)
    return pl.pallas_call(
        _discretize_kernel,
        out_shape=(rep, rep, blocks, blocks),
        name="ssm_discretize",
    )(a_re, a_im, log_dt, b_re, b_im, c_re, c_im)


def _head_rmsnorm(zc, gain, ones_blk):
    sq = (zc * zc).astype(BF16)
    ss = jnp.dot(sq, ones_blk, preferred_element_type=F32)
    return zc * lax.rsqrt(ss * (1.0 / HEAD_DIM) + EPS) * gain


def _in_proj_kernel(x_ref, gn_ref, w_ref, gq_ref, gk_ref, *rest):
    n_cast = (len(rest) - 3) // 2
    cast_in = rest[:n_cast]
    q_ref, kv_ref, u_ref = rest[n_cast:n_cast + 3]
    cast_out = rest[n_cast + 3:]
    for src, dst in zip(cast_in, cast_out):
        dst[...] = src[...].astype(BF16)
    ri = lax.broadcasted_iota(jnp.int32, (MXU_DIM, MXU_DIM), 0) // HEAD_DIM
    ci = lax.broadcasted_iota(jnp.int32, (MXU_DIM, MXU_DIM), 1) // HEAD_DIM
    ones_blk = (ri == ci).astype(BF16)
    gq = gq_ref[...]
    chunk = x_ref.shape[0] // IN_PROJ_ROW_CHUNKS
    low = lax.broadcasted_iota(jnp.int32, (chunk, LANES), 1) < HEAD_DIM
    for ck in range(IN_PROJ_ROW_CHUNKS):
        rs = slice(ck * chunk, (ck + 1) * chunk)
        x = x_ref[rs, :]
        r = lax.rsqrt(jnp.mean(x * x, axis=-1, keepdims=True) + EPS)
        xn = (x * r * gn_ref[...]).astype(BF16)
        zq = jnp.dot(xn, w_ref[:, :ATTN_WIDTH], preferred_element_type=F32)
        zr = jnp.dot(xn, w_ref[:, ATTN_WIDTH:], preferred_element_type=F32)
        lane_blocks = []
        for c in range(ATTN_WIDTH // MXU_DIM):
            qn = _head_rmsnorm(zq[:, c * MXU_DIM:(c + 1) * MXU_DIM], gq, ones_blk) * (HEAD_DIM ** -0.5)
            lane_blocks += [qn[:, :LANES], qn[:, LANES:]]
        for pair in range(N_KV_HEADS // 2):
            for r in range(Q_PER_KV):
                first = lane_blocks[((2 * pair) * Q_PER_KV + r) // 2]
                second = lane_blocks[((2 * pair + 1) * Q_PER_KV + r) // 2]
                if r % 2 == 0:
                    blk = jnp.where(low, first, pltpu.roll(second, HEAD_DIM, axis=1))
                else:
                    blk = jnp.where(low, pltpu.roll(first, HEAD_DIM, axis=1), second)
                dst = pair * Q_PER_KV + r
                q_ref[rs, dst * LANES:(dst + 1) * LANES] = blk.astype(BF16)
        kv_ref[rs, :KV_WIDTH] = _head_rmsnorm(zr[:, :KV_WIDTH], gk_ref[...], ones_blk)
        kv_ref[rs, KV_WIDTH:] = zr[:, KV_WIDTH:2 * KV_WIDTH]
        u_ref[rs, :] = zr[:, 2 * KV_WIDTH:]


def _paired_head_slab(i):
    g, r = i // Q_PER_KV, i % Q_PER_KV
    paired = (g // 2) * (2 * Q_PER_KV) + r * 2 + g % 2
    return jnp.where(i < N_HEADS, paired, i)


def _in_proj(x2d, gn, w_in_b, gq, gk, tm, to_bf16=()):
    rows = x2d.shape[0]
    steps = rows // tm
    mats = [w for w, _ in to_bf16]
    assert all(w.shape[0] % (steps * 2 * SUBLANES) == 0 for w in mats)
    slab_in = [pl.BlockSpec((w.shape[0] // steps, w.shape[1]), lambda i: (i, 0)) for w in mats]
    slab_out = [pl.BlockSpec((w.shape[0] // steps, w.shape[1]),
                             (lambda i: (i, 0)) if place is None else (lambda i, place=place: (place(i), 0)))
                for w, place in to_bf16]
    return pl.pallas_call(
        _in_proj_kernel,
        grid=(steps,),
        in_specs=[
            pl.BlockSpec((tm, D_MODEL), lambda i: (i, 0)),
            _const_spec((1, D_MODEL)),
            _const_spec((D_MODEL, PROJ_WIDTH)),
            _const_spec((1, MXU_DIM)),
            _const_spec((1, MXU_DIM)),
        ] + slab_in,
        out_specs=[
            pl.BlockSpec((tm, ATTN_WIDTH), lambda i: (i, 0)),
            pl.BlockSpec((tm, 2 * KV_WIDTH), lambda i: (i, 0)),
            pl.BlockSpec((tm, SSM_WIDTH), lambda i: (i, 0)),
        ] + slab_out,
        out_shape=(
            jax.ShapeDtypeStruct((rows, ATTN_WIDTH), BF16),
            jax.ShapeDtypeStruct((rows, 2 * KV_WIDTH), F32),
            jax.ShapeDtypeStruct((rows, SSM_WIDTH), F32),
        ) + tuple(jax.ShapeDtypeStruct(w.shape, BF16) for w in mats),
        compiler_params=_cparams(("arbitrary",)),
        name="in_proj",
    )(x2d, gn, w_in_b, gq, gk, *mats)


def _alibi_slope(head):
    return 2.0 ** (-8.0 * (head + 1) / N_HEADS)


ATTN_BLOCKS_PER_STEP = 8


def _prompt_attn_kernel(sink_ref, q_ref, kvp_ref, kvc_ref, bias_ref, o_ref):
    tq = WINDOW
    lane = lax.broadcasted_iota(jnp.int32, (tq, LANES), 1)
    low = lane < HEAD_DIM
    zero = jnp.zeros((tq, LANES), BF16)

    def keys_or_values(blk, lanes):
        own = kvc_ref[blk * tq:(blk + 1) * tq, lanes]
        prev = kvp_ref[:, lanes] if blk == 0 else kvc_ref[(blk - 1) * tq:blk * tq, lanes]
        return jnp.concatenate([prev, own], axis=0).astype(BF16)

    for blk in range(q_ref.shape[0] // tq):
        rows = slice(blk * tq, (blk + 1) * tq)
        has_prev = (pl.program_id(1) > 0).astype(jnp.int32) if blk == 0 else 1
        for pair in range(N_KV_HEADS // 2):
            kc = keys_or_values(blk, slice(pair * LANES, (pair + 1) * LANES))
            vc = keys_or_values(blk, slice(KV_WIDTH + pair * LANES, KV_WIDTH + (pair + 1) * LANES))
            qb = [q_ref[rows, (pair * Q_PER_KV + r) * LANES:(pair * Q_PER_KV + r + 1) * LANES]
                  for r in range(Q_PER_KV)]
            qs = jnp.concatenate([jnp.where(low, b, zero) for b in qb] + [jnp.where(low, zero, b) for b in qb],
                                 axis=0)
            scores = lax.dot_general(qs, kc, (((1,), (1,)), ((), ())), preferred_element_type=F32)
            probs = []
            inv_den = []
            for hh in range(2 * Q_PER_KV):
                head = (2 * pair + hh // Q_PER_KV) * Q_PER_KV + hh % Q_PER_KV
                s = scores[hh * tq:(hh + 1) * tq] + bias_ref[has_prev, head]
                sink = sink_ref[head]
                m = jnp.maximum(jnp.max(s, axis=-1, keepdims=True), sink)
                p = jnp.exp(s - m)
                den = jnp.sum(p, axis=-1, keepdims=True) + jnp.exp(sink - m)
                probs.append(p.astype(BF16))
                inv_den.append(1.0 / den)
            pv = jnp.dot(jnp.concatenate(probs, axis=0), vc, preferred_element_type=F32)
            for r in range(Q_PER_KV):
                o_low = pv[r * tq:(r + 1) * tq] * inv_den[r]
                o_high = pv[(Q_PER_KV + r) * tq:(Q_PER_KV + r + 1) * tq] * inv_den[Q_PER_KV + r]
                o_ref[rows, (pair * Q_PER_KV + r) * LANES:(pair * Q_PER_KV + r + 1) * LANES] = (
                    jnp.where(low, o_low, o_high).astype(BF16))


def _prompt_attn_bias():
    w = WINDOW
    slope = jnp.exp2(-8.0 * (jnp.arange(N_HEADS, dtype=F32) + 1.0) / N_HEADS)
    i = jnp.arange(w)[:, None]
    j = jnp.arange(2 * w)[None, :]
    d = i + w - j
    ok = (d >= 0) & (d <= w)
    ok = jnp.stack([ok & (j >= w), ok])
    return jnp.where(ok[:, None], -slope[None, :, None, None] * d.astype(F32)[None, None], NEG_INF)


def _prompt_attention(sinks, q, kv, batch, seq):
    per = ATTN_BLOCKS_PER_STEP
    tq = per * WINDOW
    nb = seq // tq
    bias = _prompt_attn_bias()
    return pl.pallas_call(
        _prompt_attn_kernel,
        grid=(batch, nb),
        in_specs=[
            pl.BlockSpec(memory_space=pltpu.SMEM),
            pl.BlockSpec((tq, ATTN_WIDTH), lambda b, n: (b * nb + n, 0)),
            pl.BlockSpec((WINDOW, 2 * KV_WIDTH), lambda b, n: (b * nb * per + jnp.maximum(n * per - 1, 0), 0)),
            pl.BlockSpec((tq, 2 * KV_WIDTH), lambda b, n: (b * nb + n, 0)),
            _const_spec(bias.shape),
        ],
        out_specs=pl.BlockSpec((tq, ATTN_WIDTH), lambda b, n: (b * nb + n, 0)),
        out_shape=jax.ShapeDtypeStruct((batch * seq, ATTN_WIDTH), BF16),
        compiler_params=_cparams(("parallel", "parallel")),
        name="prompt_attention",
    )(sinks, q, kv, kv, bias)


SAMPLE_ATTN_BATCH = 16
SAMPLE_ATTN_GROUP = 2


def _sample_attn_kernel(sink_ref, q_ref, ckt_ref, cvt_ref, kvn_ref, o_ref, nkt_ref, nvt_ref, *, t_new):
    rows = Q_PER_KV * t_new
    tk = 2 * WINDOW
    i = lax.broadcasted_iota(jnp.int32, (rows, tk), 0) % t_new
    c = lax.broadcasted_iota(jnp.int32, (rows, tk), 1)
    is_new = c >= tk - t_new
    j = jnp.where(c < WINDOW, c, c - (WINDOW - t_new))
    d = i + WINDOW - j
    valid = (d >= 0) & (d <= WINDOW) & ((c < WINDOW) | is_new)
    delta = d.astype(F32)
    lane_q = lax.broadcasted_iota(jnp.int32, (rows, LANES), 1)
    low = lane_q < HEAD_DIM
    zero_q = jnp.zeros((rows, LANES), BF16)
    lane_w = lax.broadcasted_iota(jnp.int32, (WINDOW, LANES), 1)
    tail = lane_w >= WINDOW - t_new
    rid = lax.broadcasted_iota(jnp.int32, (rows, 1), 0) // t_new
    alibi, sinks = [], []
    for pair in range(N_KV_HEADS // 2):
        sl, sk = [], []
        for half in range(2):
            slope = jnp.zeros((rows, 1), F32)
            sink = jnp.zeros((rows, 1), F32)
            for r in range(Q_PER_KV):
                head = (2 * pair + half) * Q_PER_KV + r
                slope = jnp.where(rid == r, _alibi_slope(head), slope)
                sink = jnp.where(rid == r, sink_ref[head], sink)
            sl.append(slope)
            sk.append(sink)
        alibi.append(jnp.concatenate(sl, axis=0) * jnp.concatenate([delta, delta], axis=0))
        sinks.append(jnp.concatenate(sk, axis=0))
    valid2 = jnp.concatenate([valid, valid], axis=0)

    lead_zero = jnp.zeros((WINDOW - SUBLANES, LANES), F32)

    def new_rows_transposed(e, lanes):
        return jnp.concatenate([lead_zero, kvn_ref[e, :, lanes]], axis=0).T

    def shifted(old_t, new_t):
        return jnp.where(tail, new_t, pltpu.roll(old_t, WINDOW - t_new, axis=1))

    def keys_and_scores(e, pair):
        heads = slice(2 * pair, 2 * pair + 2)
        kt = ckt_ref[e, heads].reshape(2 * HEAD_DIM, WINDOW)
        knt = new_rows_transposed(e, slice(pair * LANES, (pair + 1) * LANES))
        nkt_ref[e, heads] = shifted(kt, knt).reshape(2, HEAD_DIM, WINDOW)
        qp = q_ref[e, pair * rows:(pair + 1) * rows, :]
        qs = jnp.concatenate([jnp.where(low, qp, zero_q), jnp.where(low, zero_q, qp)], axis=0)
        kt_all = jnp.concatenate([kt, knt], axis=1).astype(BF16)
        return jnp.dot(qs, kt_all, preferred_element_type=F32)

    def values(e, pair):
        heads = slice(2 * pair, 2 * pair + 2)
        vt = cvt_ref[e, heads].reshape(2 * HEAD_DIM, WINDOW)
        vnt = new_rows_transposed(e, slice(KV_WIDTH + pair * LANES, KV_WIDTH + (pair + 1) * LANES))
        nvt_ref[e, heads] = shifted(vt, vnt).reshape(2, HEAD_DIM, WINDOW)
        return jnp.concatenate([vt, vnt], axis=1).astype(BF16)

    def softmax(pair, scores):
        s = jnp.where(valid2, scores - alibi[pair], NEG_INF)
        m = jnp.maximum(jnp.max(s, axis=-1, keepdims=True), sinks[pair])
        p = jnp.exp(s - m)
        den = jnp.sum(p, axis=-1, keepdims=True) + jnp.exp(sinks[pair] - m)
        return p.astype(BF16), 1.0 / den

    def body(i, carry):
        group = [(i * SAMPLE_ATTN_GROUP + k, pair) for k in range(SAMPLE_ATTN_GROUP) for pair in range(N_KV_HEADS // 2)]
        scores = [keys_and_scores(e, pair) for e, pair in group]
        vt_all = [values(e, pair) for e, pair in group]
        probs = [softmax(pair, sc) for (e, pair), sc in zip(group, scores)]
        for (e, pair), (p, inv_den), vt in zip(group, probs, vt_all):
            pv = lax.dot_general(p, vt, (((1,), (1,)), ((), ())), preferred_element_type=F32)
            pv = pv * inv_den
            o_ref[e, pair * rows:(pair + 1) * rows, :] = jnp.where(low, pv[:rows], pv[rows:]).astype(BF16)
        return carry

    lax.fori_loop(0, q_ref.shape[0] // SAMPLE_ATTN_GROUP, body, 0)


def _sample_attention(sinks, q3, ckt, cvt, kvn, t_new):
    n = q3.shape[0]
    bn = SAMPLE_ATTN_BATCH
    qrows = q3.shape[1]
    cache_spec = pl.BlockSpec((bn, N_KV_HEADS, HEAD_DIM, WINDOW), lambda b: (b, 0, 0, 0))
    cache_shape = jax.ShapeDtypeStruct((n, N_KV_HEADS, HEAD_DIM, WINDOW), F32)
    return pl.pallas_call(
        functools.partial(_sample_attn_kernel, t_new=t_new),
        grid=(n // bn,),
        in_specs=[
            pl.BlockSpec(memory_space=pltpu.SMEM),
            pl.BlockSpec((bn, qrows, LANES), lambda b: (b, 0, 0)),
            cache_spec,
            cache_spec,
            pl.BlockSpec((bn, SUBLANES, 2 * KV_WIDTH), lambda b: (b, 0, 0)),
        ],
        out_specs=[pl.BlockSpec((bn, qrows, LANES), lambda b: (b, 0, 0)), cache_spec, cache_spec],
        out_shape=(jax.ShapeDtypeStruct((n, qrows, LANES), BF16), cache_shape, cache_shape),
        compiler_params=_cparams(("parallel",)),
        name="sample_attention",
    )(sinks, q3, ckt, cvt, kvn)


def _gelu_tanh(x):
    c = math.sqrt(2.0 / math.pi)
    return 0.5 * x * (1.0 + jnp.tanh(c * (x + 0.044715 * (x * x * x))))


def _ssm_kernel(u_ref, h0re_ref, h0im_ref, ar_ref, ai_ref, wb_ref, wct_ref, d_ref, wglu_ref, bglu_ref, gs_ref,
                mix_ref, hre_ref, him_ref, s_ref, *, nb, steps, n_sub):
    @pl.when(pl.program_id(0) == 0)
    def _():
        hre_ref[...] = h0re_ref[...]
        him_ref[...] = h0im_ref[...]

    dims = dict(nb=nb, steps=steps)
    pending = []

    def trace_pending_tails():
        while pending:
            y, s = pending.pop(0)
            _ssm_gate_and_store(y, wglu_ref, bglu_ref, gs_ref, mix_ref, sub=s, **dims)

    for sub in range(n_sub):
        u, ub = _ssm_permute_in(u_ref, sub=sub, **dims)
        ys = []
        for j in range(N_SSM_BLOCKS):
            if j == SSM_TAIL_LAG_BLOCKS:
                trace_pending_tails()
            ys.append(_ssm_state_block(ub, j, ar_ref, ai_ref, wb_ref, wct_ref, hre_ref, him_ref, s_ref, sub=sub, **dims))
        trace_pending_tails()
        pending.append((jnp.concatenate(ys, axis=1) + d_ref[...] * u, sub))
    trace_pending_tails()


def _ssm_positions(ref, nb, steps, sub):
    per_sub = nb * steps // ref.shape[0]
    return slice(sub * per_sub, (sub + 1) * per_sub)


def _ssm_permute_in(u_ref, *, nb, steps, sub):
    rows = nb * steps
    tm_row = lax.broadcasted_iota(jnp.int32, (rows, rows), 0)
    bm_col = lax.broadcasted_iota(jnp.int32, (rows, rows), 1)
    to_time_major = (bm_col == (tm_row % nb) * steps + tm_row // nb).astype(BF16)
    u_bm = u_ref[:, _ssm_positions(u_ref, nb, steps, sub), :].reshape(rows, SSM_WIDTH)
    u_hi = u_bm.astype(BF16)
    u_lo = (u_bm - u_hi.astype(F32)).astype(BF16)
    u_hi_tm = jnp.dot(to_time_major, u_hi, preferred_element_type=F32)
    u = u_hi_tm + jnp.dot(to_time_major, u_lo, preferred_element_type=F32)
    return u, u_hi_tm.astype(BF16)


def _ssm_state_block(ub, j, ar_ref, ai_ref, wb_ref, wct_ref, hre_ref, him_ref, s_ref, *, nb, steps, sub):
    rows = nb * steps
    row0 = (sub % (s_ref.shape[0] // rows)) * rows
    blk_cols = 2 * BLOCK_STATES
    s_ref[row0:row0 + rows, j * blk_cols:(j + 1) * blk_cols] = jnp.dot(
        ub[:, j * MXU_DIM:(j + 1) * MXU_DIM], wb_ref[j], preferred_element_type=F32)
    for part in range(BLOCK_STATES // SCAN_LANES):
        rc = j * blk_cols + part * SCAN_LANES
        ic = rc + BLOCK_STATES
        sc = j * BLOCK_STATES + part * SCAN_LANES
        a_re = ar_ref[:, sc:sc + SCAN_LANES]
        a_im = ai_ref[:, sc:sc + SCAN_LANES]
        for bg in range(nb // SUBLANES):
            b0 = bg * SUBLANES
            h_re = hre_ref[b0:b0 + SUBLANES, sc:sc + SCAN_LANES]
            h_im = him_ref[b0:b0 + SUBLANES, sc:sc + SCAN_LANES]
            for t in range(steps):
                row = row0 + t * nb + b0
                n_re = a_re * h_re - a_im * h_im + s_ref[row:row + SUBLANES, rc:rc + SCAN_LANES]
                n_im = a_re * h_im + a_im * h_re + s_ref[row:row + SUBLANES, ic:ic + SCAN_LANES]
                s_ref[row:row + SUBLANES, rc:rc + SCAN_LANES] = n_re
                s_ref[row:row + SUBLANES, ic:ic + SCAN_LANES] = n_im
                h_re, h_im = n_re, n_im
            hre_ref[b0:b0 + SUBLANES, sc:sc + SCAN_LANES] = h_re
            him_ref[b0:b0 + SUBLANES, sc:sc + SCAN_LANES] = h_im
    hb = s_ref[row0:row0 + rows, j * blk_cols:(j + 1) * blk_cols].astype(BF16)
    return lax.dot_general(hb, wct_ref[j], (((1,), (1,)), ((), ())), preferred_element_type=F32)


def _ssm_gate_and_store(y, wglu_ref, bglu_ref, gs_ref, mix_ref, *, nb, steps, sub):
    rows = nb * steps
    bm_row = lax.broadcasted_iota(jnp.int32, (rows, rows), 0)
    tm_col = lax.broadcasted_iota(jnp.int32, (rows, rows), 1)
    to_batch_major = (bm_row == (tm_col % nb) * steps + tm_col // nb).astype(BF16)
    g = _gelu_tanh(y)
    gate = jnp.dot(g.astype(BF16), wglu_ref[...], preferred_element_type=F32) + bglu_ref[...]
    so = g * jax.nn.sigmoid(gate)
    r = lax.rsqrt(jnp.mean(so * so, axis=-1, keepdims=True) + EPS)
    mix_tm = (so * r * gs_ref[...]).astype(BF16)
    mix_bm = jnp.dot(to_batch_major, mix_tm, preferred_element_type=F32).astype(BF16)
    mix_ref[:, _ssm_positions(mix_ref, nb, steps, sub), :] = mix_bm.reshape(
        mix_ref.shape[0], rows // mix_ref.shape[0], SSM_WIDTH)


def _ssm_mixer(u3, h0_re, h0_im, ar8, ai8, wb, wct, d, wglu_b, bglu, gs, nb, steps, n_sub):
    nbv, seq, _ = u3.shape
    blk_rows = n_sub * nb * steps // nbv
    assert blk_rows % SUBLANES == 0 and seq % blk_rows == 0
    cols = 2 * N_STATE
    tr = min(n_sub, 2) * nb * steps
    return pl.pallas_call(
        functools.partial(_ssm_kernel, nb=nb, steps=steps, n_sub=n_sub),
        grid=(seq // blk_rows,),
        in_specs=[
            pl.BlockSpec((nbv, blk_rows, SSM_WIDTH), lambda i: (0, i, 0)),
            _const_spec((nb, N_STATE)),
            _const_spec((nb, N_STATE)),
            _const_spec((SUBLANES, N_STATE)),
            _const_spec((SUBLANES, N_STATE)),
            _const_spec((N_SSM_BLOCKS, MXU_DIM, 2 * BLOCK_STATES)),
            _const_spec((N_SSM_BLOCKS, MXU_DIM, 2 * BLOCK_STATES)),
            _const_spec((1, SSM_WIDTH)),
            _const_spec((SSM_WIDTH, SSM_WIDTH)),
            _const_spec((1, SSM_WIDTH)),
            _const_spec((1, SSM_WIDTH)),
        ],
        out_specs=[
            pl.BlockSpec((nbv, blk_rows, SSM_WIDTH), lambda i: (0, i, 0)),
            pl.BlockSpec((nb, N_STATE), lambda i: (0, 0)),
            pl.BlockSpec((nb, N_STATE), lambda i: (0, 0)),
        ],
        out_shape=(
            jax.ShapeDtypeStruct((nbv, seq, SSM_WIDTH), BF16),
            jax.ShapeDtypeStruct((nb, N_STATE), F32),
            jax.ShapeDtypeStruct((nb, N_STATE), F32),
        ),
        scratch_shapes=[pltpu.VMEM((tr, cols), F32)],
        compiler_params=_cparams(("arbitrary",)),
        name="ssm_mixer",
    )(u3, h0_re, h0_im, ar8, ai8, wb, wct, d, wglu_b, bglu, gs)


def _out_mlp_kernel(x_ref, a_ref, ga_ref, ms_ref, wo_ref, gm_ref, wup_ref, wdn_ref, y_ref, hn_ref):
    @pl.when(pl.program_id(1) == 0)
    def _():
        half_rows = x_ref.shape[0] // 2
        for half in range(2):
            rs = slice(half * half_rows, (half + 1) * half_rows)
            h = x_ref[rs, :] + jnp.dot(ms_ref[rs, :], wo_ref[ATTN_WIDTH:, :], preferred_element_type=F32)
            a = a_ref[rs, :].astype(F32)
            ra = lax.rsqrt(jnp.mean(a * a, axis=-1, keepdims=True) + EPS)
            ma = (a * ra * ga_ref[...]).astype(BF16)
            h = h + jnp.dot(ma, wo_ref[:ATTN_WIDTH, :], preferred_element_type=F32)
            y_ref[rs, :] = h
            rh = lax.rsqrt(jnp.mean(h * h, axis=-1, keepdims=True) + EPS)
            hn_ref[rs, :] = (h * rh * gm_ref[...]).astype(BF16)

    t = jnp.dot(hn_ref[...], wup_ref[...], preferred_element_type=F32)
    t = jnp.maximum(t, 0.0)
    t = (t * t).astype(BF16)
    y_ref[...] += jnp.dot(t, wdn_ref[...], preferred_element_type=F32)


def _out_mlp(x2d, attn, ga, mix_s, wo, gm, wup, wdn, tm, tf):
    rows = x2d.shape[0]
    n_ff = D_FF // tf

    def ff_tile(i, j):
        return jnp.where(i % 2 == 0, j, n_ff - 1 - j)

    return pl.pallas_call(
        _out_mlp_kernel,
        grid=(rows // tm, n_ff),
        in_specs=[
            pl.BlockSpec((tm, D_MODEL), lambda i, j: (i, 0)),
            pl.BlockSpec((tm, ATTN_WIDTH), lambda i, j: (i, 0)),
            _const_spec((1, ATTN_WIDTH)),
            pl.BlockSpec((tm, SSM_WIDTH), lambda i, j: (i, 0)),
            _const_spec((D_MODEL, D_MODEL)),
            _const_spec((1, D_MODEL)),
            pl.BlockSpec((D_MODEL, tf), lambda i, j: (0, ff_tile(i, j))),
            pl.BlockSpec((tf, D_MODEL), lambda i, j: (ff_tile(i, j), 0)),
        ],
        out_specs=pl.BlockSpec((tm, D_MODEL), lambda i, j: (i, 0)),
        out_shape=jax.ShapeDtypeStruct((rows, D_MODEL), F32),
        scratch_shapes=[pltpu.VMEM((tm, D_MODEL), BF16)],
        compiler_params=_cparams(("parallel", "arbitrary")),
        name="out_mlp",
    )(x2d, attn, ga, mix_s, wo, gm, wup, wdn)


def _pair_heads(a, axis):
    shape = a.shape
    split = shape[:axis] + (N_KV_HEADS // 2, 2, Q_PER_KV, HEAD_DIM) + shape[axis + 1:]
    return jnp.swapaxes(a.reshape(split), axis + 1, axis + 2).reshape(shape)


def _layer(x, cache_k, cache_v, h0_re, h0_im, p, side):
    n, t = x.shape[:2]
    rows = n * t
    tm = ROW_TILE
    x2d = x.reshape(rows, D_MODEL)
    if side is None:
        assert D_MODEL // (rows // tm) == HEAD_DIM
        q, kv, u, *side = _in_proj(x2d, p['gn'], p['w_in'], p['gq'], p['gk'], tm,
                                   to_bf16=((p['w_up_f32'], None), (p['w_down_f32'], None),
                                            (p['w_out_f32'], _paired_head_slab), (p['w_glu_f32'], None)))
    else:
        q, kv, u = _in_proj(x2d, p['gn'], p['w_in'], p['gq'], p['gk'], tm)
    w_up, w_down, w_out_b, w_glu_b = side
    kv3 = kv.reshape(n, t, 2 * KV_WIDTH)

    if cache_k is None:
        attn = _prompt_attention(p['sinks'], q, kv, n, t)
        tail = kv3[:, t - WINDOW:]
        new_k = tail[..., :KV_WIDTH].reshape(n, WINDOW, N_KV_HEADS, HEAD_DIM)
        new_v = tail[..., KV_WIDTH:].reshape(n, WINDOW, N_KV_HEADS, HEAD_DIM)
        u3 = u.reshape(n, t, SSM_WIDTH)
        ssm_steps, ssm_subs = SSM_SUB_TILE_ROWS // n, SSM_SUB_TILES
    else:
        nblk = ATTN_WIDTH // LANES
        q3 = q.reshape(n, t, nblk, LANES).transpose(0, 2, 1, 3).reshape(n, nblk * t, LANES)
        kvn = jnp.pad(kv3, ((0, 0), (SUBLANES - t, 0), (0, 0)))
        ckt = cache_k.transpose(0, 2, 3, 1)
        cvt = cache_v.transpose(0, 2, 3, 1)
        a3, nkt, nvt = _sample_attention(p['sinks'], q3, ckt, cvt, kvn, t)
        attn = a3.reshape(n, nblk, t, LANES).transpose(0, 2, 1, 3).reshape(rows, ATTN_WIDTH)
        new_k = nkt.transpose(0, 3, 1, 2)
        new_v = nvt.transpose(0, 3, 1, 2)
        u3 = u.reshape(1, rows, SSM_WIDTH)
        ssm_steps, ssm_subs = t, 1

    mix3, h_re, h_im = _ssm_mixer(u3, h0_re.reshape(n, N_STATE), h0_im.reshape(n, N_STATE), p['ar8'], p['ai8'],
                                  p['wb'], p['wct'], p['d'], w_glu_b, p['b_glu'], p['gs'], n, ssm_steps, ssm_subs)
    mix_s = mix3.reshape(rows, SSM_WIDTH)
    h_re = h_re.reshape(n, N_SSM_GROUPS, STATE_DIM)
    h_im = h_im.reshape(n, N_SSM_GROUPS, STATE_DIM)

    y = _out_mlp(x2d, attn, p['ga'], mix_s, w_out_b, p['gm'], w_up, w_down, tm, MLP_FF_TILE)
    return y.reshape(n, t, D_MODEL), new_k, new_v, h_re, h_im, side


def _prepare_params(l, attn_norm_g, w_in, q_norm_g, k_norm_g, attn_sinks,
                    ssm_A_re, ssm_A_im, ssm_log_dt, ssm_B_re, ssm_B_im, ssm_C_re, ssm_C_im, ssm_D,
                    w_glu, b_glu, attn_out_g, ssm_out_g, w_out, mlp_norm_g, w_mlp_up, w_mlp_down):
    w_in_l = w_in[l]
    heads_per_blk = MXU_DIM // HEAD_DIM

    a_re = ssm_A_re[l].reshape(1, N_STATE)
    a_im = ssm_A_im[l].reshape(1, N_STATE)
    ldt = jnp.broadcast_to(ssm_log_dt[l][:, None], (N_SSM_GROUPS, STATE_DIM)).reshape(1, N_STATE)
    b_re = ssm_B_re[l].transpose(2, 0, 1).reshape(SSM_GROUP, N_STATE)
    b_im = ssm_B_im[l].transpose(2, 0, 1).reshape(SSM_GROUP, N_STATE)
    c_re = ssm_C_re[l].transpose(1, 0, 2).reshape(SSM_GROUP, N_STATE)
    c_im = ssm_C_im[l].transpose(1, 0, 2).reshape(SSM_GROUP, N_STATE)
    ar8, ai8, wb, wct = _ssm_discretize(a_re, a_im, ldt, b_re, b_im, c_re, c_im)

    return dict(
        gn=attn_norm_g[l].reshape(1, D_MODEL),
        w_in=w_in_l.astype(BF16),
        gq=jnp.tile(q_norm_g[l], heads_per_blk).reshape(1, MXU_DIM),
        gk=jnp.tile(k_norm_g[l], heads_per_blk).reshape(1, MXU_DIM),
        sinks=attn_sinks[l].astype(F32),
        ar8=ar8, ai8=ai8, wb=wb, wct=wct,
        d=ssm_D[l].reshape(1, SSM_WIDTH),
        w_glu_f32=w_glu[l],
        b_glu=b_glu[l].reshape(1, SSM_WIDTH),
        gs=ssm_out_g[l].reshape(1, SSM_WIDTH),
        ga=_pair_heads(attn_out_g[l], 0).reshape(1, ATTN_WIDTH),
        w_out_f32=w_out[l],
        gm=mlp_norm_g[l].reshape(1, D_MODEL),
        w_up_f32=w_mlp_up[l],
        w_down_f32=w_mlp_down[l],
    )


def kernel(x_prompt, x_sample, cache_k, cache_v, state_ssm_re, state_ssm_im, attn_norm_g, w_in, q_norm_g, k_norm_g, attn_sinks, ssm_A_re, ssm_A_im, ssm_log_dt, ssm_B_re, ssm_B_im, ssm_C_re, ssm_C_im, ssm_D, w_glu, b_glu, attn_out_g, ssm_out_g, w_out, mlp_norm_g, w_mlp_up, w_mlp_down):
    depth = w_in.shape[0]
    xp, xs = x_prompt, x_sample
    zeros_state = jnp.zeros((x_prompt.shape[0], N_SSM_GROUPS, STATE_DIM), F32)
    outs = [[] for _ in range(8)]
    for l in range(depth):
        p = _prepare_params(l, attn_norm_g, w_in, q_norm_g, k_norm_g, attn_sinks,
                            ssm_A_re, ssm_A_im, ssm_log_dt, ssm_B_re, ssm_B_im, ssm_C_re, ssm_C_im, ssm_D,
                            w_glu, b_glu, attn_out_g, ssm_out_g, w_out, mlp_norm_g, w_mlp_up, w_mlp_down)
        xp, kp, vp, hrp, hip, side = _layer(xp, None, None, zeros_state, zeros_state, p, None)
        xs, ks, vs, hrs, his, _ = _layer(xs, cache_k[l], cache_v[l], state_ssm_re[l], state_ssm_im[l], p, side)
        for lst, val in zip(outs, (kp, vp, hrp, hip, ks, vs, hrs, his)):
            lst.append(val)
    return (xp, xs) + tuple(jnp.stack(o) for o in outs)
```

```python
import functools
import math

import jax
import jax.numpy as jnp
from jax import lax
from jax.experimental import pallas as pl
from jax.experimental.pallas import tpu as pltpu

D_MODEL = 2048
ATTN_WIDTH = 1024
SSM_WIDTH = 1024
HEAD_DIM = 64
N_HEADS = 16
N_KV_HEADS = 4
Q_PER_KV = 4
KV_WIDTH = 256
WINDOW = 128
SSM_GROUP = 16
N_SSM_GROUPS = 64
STATE_DIM = 64
N_STATE = N_SSM_GROUPS * STATE_DIM
D_FF = 8192
PROJ_WIDTH = ATTN_WIDTH + 2 * KV_WIDTH + SSM_WIDTH
EPS = 1e-6
NEG_INF = -1e30
LOG2E = 1.0 / math.log(2.0)

LANES = 128
SUBLANES = 8
MXU_DIM = 256
VMEM_LIMIT = 56 * 1024 * 1024

GROUPS_PER_BLOCK = MXU_DIM // SSM_GROUP
N_SSM_BLOCKS = N_SSM_GROUPS // GROUPS_PER_BLOCK
BLOCK_STATES = GROUPS_PER_BLOCK * STATE_DIM
SCAN_LANES = 512
SSM_SUB_TILE_ROWS = 256
SSM_SUB_TILES = 2
SSM_TAIL_LAG_BLOCKS = 4
ROW_TILE = 512
MLP_FF_TILE = 1024
IN_PROJ_ROW_CHUNKS = 2

F32 = jnp.float32
BF16 = jnp.bfloat16


def _cparams(sem):
    return pltpu.CompilerParams(dimension_semantics=sem, vmem_limit_bytes=VMEM_LIMIT)


def _const_spec(shape):
    nd = len(shape)
    return pl.BlockSpec(shape, lambda *_: (0,) * nd, pipeline_mode=pl.Buffered(1))


def _discretize_kernel(are_ref, aim_ref, ldt_ref, bre_ref, bim_ref, cre_ref, cim_ref,
                       abr_ref, abi_ref, wb_ref, wct_ref):
    a_re = are_ref[...]
    a_im = aim_ref[...]
    dt = jnp.exp(ldt_ref[...])
    mag = jnp.exp(a_re * dt)
    ab_re = mag * jnp.cos(a_im * dt)
    ab_im = mag * jnp.sin(a_im * dt)
    abr_ref[...] = jnp.broadcast_to(ab_re, abr_ref.shape)
    abi_ref[...] = jnp.broadcast_to(ab_im, abi_ref.shape)
    x = ab_re - 1.0
    y = ab_im
    den = a_re * a_re + a_im * a_im
    k_re = (x * a_re + y * a_im) / den
    k_im = (y * a_re - x * a_im) / den
    b_re = bre_ref[...]
    b_im = bim_ref[...]
    bb_re = k_re * b_re - k_im * b_im
    bb_im = k_re * b_im + k_im * b_re

    rows = GROUPS_PER_BLOCK * SSM_GROUP
    row_group = lax.broadcasted_iota(jnp.int32, (rows, BLOCK_STATES), 0) // SSM_GROUP
    col_group = lax.broadcasted_iota(jnp.int32, (rows, BLOCK_STATES), 1) // STATE_DIM
    same_group = row_group == col_group

    def block_diag(m, j):
        blk = m[:, j * BLOCK_STATES:(j + 1) * BLOCK_STATES]
        return jnp.where(same_group, jnp.concatenate([blk] * GROUPS_PER_BLOCK, axis=0), 0.0).astype(BF16)

    c_re = cre_ref[...]
    c_im_neg = -cim_ref[...]
    for j in range(N_SSM_BLOCKS):
        wb_ref[j, :, :BLOCK_STATES] = block_diag(bb_re, j)
        wb_ref[j, :, BLOCK_STATES:] = block_diag(bb_im, j)
        wct_ref[j, :, :BLOCK_STATES] = block_diag(c_re, j)
        wct_ref[j, :, BLOCK_STATES:] = block_diag(c_im_neg, j)


def _ssm_discretize(a_re, a_im, log_dt, b_re, b_im, c_re, c_im):
    rep = jax.ShapeDtypeStruct((SUBLANES, N_STATE), F32)
    blocks = jax.ShapeDtypeStruct((N_SSM_BLOCKS, GROUPS_PER_BLOCK * SSM_GROUP, 2 * BLOCK_STATES), BF16)
    return pl.pallas_call(
        _discretize_kernel,
        out_shape=(rep, rep, blocks, blocks),
        name="ssm_discretize",
    )(a_re, a_im, log_dt, b_re, b_im, c_re, c_im)


def _head_rmsnorm(zc, gain, ones_blk):
    sq = (zc * zc).astype(BF16)
    ss = jnp.dot(sq, ones_blk, preferred_element_type=F32)
    return zc * lax.rsqrt(ss * (1.0 / HEAD_DIM) + EPS) * gain


def _in_proj_kernel(x_ref, gn_ref, w_ref, gq_ref, gk_ref, *rest):
    n_cast = (len(rest) - 3) // 2
    cast_in = rest[:n_cast]
    q_ref, kv_ref, u_ref = rest[n_cast:n_cast + 3]
    cast_out = rest[n_cast + 3:]
    for src, dst in zip(cast_in, cast_out):
        dst[...] = src[...].astype(BF16)
    ri = lax.broadcasted_iota(jnp.int32, (MXU_DIM, MXU_DIM), 0) // HEAD_DIM
    ci = lax.broadcasted_iota(jnp.int32, (MXU_DIM, MXU_DIM), 1) // HEAD_DIM
    ones_blk = (ri == ci).astype(BF16)
    gq = gq_ref[...]
    chunk = x_ref.shape[0] // IN_PROJ_ROW_CHUNKS
    low = lax.broadcasted_iota(jnp.int32, (chunk, LANES), 1) < HEAD_DIM
    for ck in range(IN_PROJ_ROW_CHUNKS):
        rs = slice(ck * chunk, (ck + 1) * chunk)
        x = x_ref[rs, :]
        r = lax.rsqrt(jnp.mean(x * x, axis=-1, keepdims=True) + EPS)
        xn = (x * r * gn_ref[...]).astype(BF16)
        zq = jnp.dot(xn, w_ref[:, :ATTN_WIDTH], preferred_element_type=F32)
        zr = jnp.dot(xn, w_ref[:, ATTN_WIDTH:], preferred_element_type=F32)
        lane_blocks = []
        for c in range(ATTN_WIDTH // MXU_DIM):
            qn = _head_rmsnorm(zq[:, c * MXU_DIM:(c + 1) * MXU_DIM], gq, ones_blk) * (HEAD_DIM ** -0.5 * LOG2E)
            lane_blocks += [qn[:, :LANES], qn[:, LANES:]]
        for pair in range(N_KV_HEADS // 2):
            for r in range(Q_PER_KV):
                first = lane_blocks[((2 * pair) * Q_PER_KV + r) // 2]
                second = lane_blocks[((2 * pair + 1) * Q_PER_KV + r) // 2]
                if r % 2 == 0:
                    blk = jnp.where(low, first, pltpu.roll(second, HEAD_DIM, axis=1))
                else:
                    blk = jnp.where(low, pltpu.roll(first, HEAD_DIM, axis=1), second)
                dst = pair * Q_PER_KV + r
                q_ref[rs, dst * LANES:(dst + 1) * LANES] = blk.astype(BF16)
        kv_ref[rs, :KV_WIDTH] = _head_rmsnorm(zr[:, :KV_WIDTH], gk_ref[...], ones_blk)
        kv_ref[rs, KV_WIDTH:] = zr[:, KV_WIDTH:2 * KV_WIDTH]
        u_ref[rs, :] = zr[:, 2 * KV_WIDTH:]


def _paired_head_slab(i):
    g, r = i // Q_PER_KV, i % Q_PER_KV
    paired = (g // 2) * (2 * Q_PER_KV) + r * 2 + g % 2
    return jnp.where(i < N_HEADS, paired, i)


def _in_proj(x2d, gn, w_in_b, gq, gk, tm, to_bf16=()):
    rows = x2d.shape[0]
    steps = rows // tm
    mats = [w for w, _ in to_bf16]
    assert all(w.shape[0] % (steps * 2 * SUBLANES) == 0 for w in mats)
    slab_in = [pl.BlockSpec((w.shape[0] // steps, w.shape[1]), lambda i: (i, 0)) for w in mats]
    slab_out = [pl.BlockSpec((w.shape[0] // steps, w.shape[1]),
                             (lambda i: (i, 0)) if place is None else (lambda i, place=place: (place(i), 0)))
                for w, place in to_bf16]
    return pl.pallas_call(
        _in_proj_kernel,
        grid=(steps,),
        in_specs=[
            pl.BlockSpec((tm, D_MODEL), lambda i: (i, 0)),
            _const_spec((1, D_MODEL)),
            _const_spec((D_MODEL, PROJ_WIDTH)),
            _const_spec((1, MXU_DIM)),
            _const_spec((1, MXU_DIM)),
        ] + slab_in,
        out_specs=[
            pl.BlockSpec((tm, ATTN_WIDTH), lambda i: (i, 0)),
            pl.BlockSpec((tm, 2 * KV_WIDTH), lambda i: (i, 0)),
            pl.BlockSpec((tm, SSM_WIDTH), lambda i: (i, 0)),
        ] + slab_out,
        out_shape=(
            jax.ShapeDtypeStruct((rows, ATTN_WIDTH), BF16),
            jax.ShapeDtypeStruct((rows, 2 * KV_WIDTH), F32),
            jax.ShapeDtypeStruct((rows, SSM_WIDTH), F32),
        ) + tuple(jax.ShapeDtypeStruct(w.shape, BF16) for w in mats),
        compiler_params=_cparams(("arbitrary",)),
        name="in_proj",
    )(x2d, gn, w_in_b, gq, gk, *mats)


def _alibi_slope(head):
    return 2.0 ** (-8.0 * (head + 1) / N_HEADS)


ATTN_BLOCKS_PER_STEP = 8


def _prompt_attn_kernel(sink_ref, q_ref, kvp_ref, kvc_ref, bias_ref, o_ref):
    tq = WINDOW
    lane = lax.broadcasted_iota(jnp.int32, (tq, LANES), 1)
    low = lane < HEAD_DIM
    zero = jnp.zeros((tq, LANES), BF16)

    def keys_or_values(blk, lanes):
        own = kvc_ref[blk * tq:(blk + 1) * tq, lanes]
        prev = kvp_ref[:, lanes] if blk == 0 else kvc_ref[(blk - 1) * tq:blk * tq, lanes]
        return jnp.concatenate([prev, own], axis=0).astype(BF16)

    for blk in range(q_ref.shape[0] // tq):
        rows = slice(blk * tq, (blk + 1) * tq)
        has_prev = (pl.program_id(1) > 0).astype(jnp.int32) if blk == 0 else 1
        for pair in range(N_KV_HEADS // 2):
            kc = keys_or_values(blk, slice(pair * LANES, (pair + 1) * LANES))
            vc = keys_or_values(blk, slice(KV_WIDTH + pair * LANES, KV_WIDTH + (pair + 1) * LANES))
            qb = [q_ref[rows, (pair * Q_PER_KV + r) * LANES:(pair * Q_PER_KV + r + 1) * LANES]
                  for r in range(Q_PER_KV)]
            qs = jnp.concatenate([jnp.where(low, b, zero) for b in qb] + [jnp.where(low, zero, b) for b in qb],
                                 axis=0)
            scores = lax.dot_general(qs, kc, (((1,), (1,)), ((), ())), preferred_element_type=F32)
            probs = []
            inv_den = []
            for hh in range(2 * Q_PER_KV):
                head = (2 * pair + hh // Q_PER_KV) * Q_PER_KV + hh % Q_PER_KV
                s = scores[hh * tq:(hh + 1) * tq] + bias_ref[has_prev, head]
                sink = sink_ref[head] * LOG2E
                m = jnp.maximum(jnp.max(s, axis=-1, keepdims=True), sink)
                p = jnp.exp2(s - m)
                den = jnp.sum(p, axis=-1, keepdims=True) + jnp.exp2(sink - m)
                probs.append(p.astype(BF16))
                inv_den.append(1.0 / den)
            pv = jnp.dot(jnp.concatenate(probs, axis=0), vc, preferred_element_type=F32)
            for r in range(Q_PER_KV):
                o_low = pv[r * tq:(r + 1) * tq] * inv_den[r]
                o_high = pv[(Q_PER_KV + r) * tq:(Q_PER_KV + r + 1) * tq] * inv_den[Q_PER_KV + r]
                o_ref[rows, (pair * Q_PER_KV + r) * LANES:(pair * Q_PER_KV + r + 1) * LANES] = (
                    jnp.where(low, o_low, o_high).astype(BF16))


def _prompt_attn_bias():
    w = WINDOW
    slope = jnp.exp2(-8.0 * (jnp.arange(N_HEADS, dtype=F32) + 1.0) / N_HEADS)
    i = jnp.arange(w)[:, None]
    j = jnp.arange(2 * w)[None, :]
    d = i + w - j
    ok = (d >= 0) & (d <= w)
    ok = jnp.stack([ok & (j >= w), ok])
    return jnp.where(ok[:, None], -LOG2E * slope[None, :, None, None] * d.astype(F32)[None, None], NEG_INF)


def _prompt_attention(sinks, q, kv, batch, seq):
    per = ATTN_BLOCKS_PER_STEP
    tq = per * WINDOW
    nb = seq // tq
    bias = _prompt_attn_bias()
    return pl.pallas_call(
        _prompt_attn_kernel,
        grid=(batch, nb),
        in_specs=[
            pl.BlockSpec(memory_space=pltpu.SMEM),
            pl.BlockSpec((tq, ATTN_WIDTH), lambda b, n: (b * nb + n, 0)),
            pl.BlockSpec((WINDOW, 2 * KV_WIDTH), lambda b, n: (b * nb * per + jnp.maximum(n * per - 1, 0), 0)),
            pl.BlockSpec((tq, 2 * KV_WIDTH), lambda b, n: (b * nb + n, 0)),
            _const_spec(bias.shape),
        ],
        out_specs=pl.BlockSpec((tq, ATTN_WIDTH), lambda b, n: (b * nb + n, 0)),
        out_shape=jax.ShapeDtypeStruct((batch * seq, ATTN_WIDTH), BF16),
        compiler_params=_cparams(("parallel", "parallel")),
        name="prompt_attention",
    )(sinks, q, kv, kv, bias)


SAMPLE_ATTN_BATCH = 16
SAMPLE_ATTN_GROUP = 2


def _sample_attn_kernel(sink_ref, q_ref, ckt_ref, cvt_ref, kvn_ref, o_ref, nkt_ref, nvt_ref, *, t_new):
    rows = Q_PER_KV * t_new
    tk = 2 * WINDOW
    i = lax.broadcasted_iota(jnp.int32, (rows, tk), 0) % t_new
    c = lax.broadcasted_iota(jnp.int32, (rows, tk), 1)
    is_new = c >= tk - t_new
    j = jnp.where(c < WINDOW, c, c - (WINDOW - t_new))
    d = i + WINDOW - j
    valid = (d >= 0) & (d <= WINDOW) & ((c < WINDOW) | is_new)
    delta = d.astype(F32)
    lane_q = lax.broadcasted_iota(jnp.int32, (rows, LANES), 1)
    low = lane_q < HEAD_DIM
    zero_q = jnp.zeros((rows, LANES), BF16)
    lane_w = lax.broadcasted_iota(jnp.int32, (WINDOW, LANES), 1)
    tail = lane_w >= WINDOW - t_new
    rid = lax.broadcasted_iota(jnp.int32, (rows, 1), 0) // t_new
    alibi, sinks = [], []
    for pair in range(N_KV_HEADS // 2):
        sl, sk = [], []
        for half in range(2):
            slope = jnp.zeros((rows, 1), F32)
            sink = jnp.zeros((rows, 1), F32)
            for r in range(Q_PER_KV):
                head = (2 * pair + half) * Q_PER_KV + r
                slope = jnp.where(rid == r, _alibi_slope(head) * LOG2E, slope)
                sink = jnp.where(rid == r, sink_ref[head] * LOG2E, sink)
            sl.append(slope)
            sk.append(sink)
        alibi.append(jnp.concatenate(sl, axis=0) * jnp.concatenate([delta, delta], axis=0))
        sinks.append(jnp.concatenate(sk, axis=0))
    valid2 = jnp.concatenate([valid, valid], axis=0)

    lead_zero = jnp.zeros((WINDOW - SUBLANES, LANES), F32)

    def new_rows_transposed(e, lanes):
        return jnp.concatenate([lead_zero, kvn_ref[e, :, lanes]], axis=0).T

    def shifted(old_t, new_t):
        return jnp.where(tail, new_t, pltpu.roll(old_t, WINDOW - t_new, axis=1))

    def keys_and_scores(e, pair):
        heads = slice(2 * pair, 2 * pair + 2)
        kt = ckt_ref[e, heads].reshape(2 * HEAD_DIM, WINDOW)
        knt = new_rows_transposed(e, slice(pair * LANES, (pair + 1) * LANES))
        nkt_ref[e, heads] = shifted(kt, knt).reshape(2, HEAD_DIM, WINDOW)
        qp = q_ref[e, pair * rows:(pair + 1) * rows, :]
        qs = jnp.concatenate([jnp.where(low, qp, zero_q), jnp.where(low, zero_q, qp)], axis=0)
        kt_all = jnp.concatenate([kt, knt], axis=1).astype(BF16)
        return jnp.dot(qs, kt_all, preferred_element_type=F32)

    def values(e, pair):
        heads = slice(2 * pair, 2 * pair + 2)
        vt = cvt_ref[e, heads].reshape(2 * HEAD_DIM, WINDOW)
        vnt = new_rows_transposed(e, slice(KV_WIDTH + pair * LANES, KV_WIDTH + (pair + 1) * LANES))
        nvt_ref[e, heads] = shifted(vt, vnt).reshape(2, HEAD_DIM, WINDOW)
        return jnp.concatenate([vt, vnt], axis=1).astype(BF16)

    def softmax(pair, scores):
        s = jnp.where(valid2, scores - alibi[pair], NEG_INF)
        m = jnp.maximum(jnp.max(s, axis=-1, keepdims=True), sinks[pair])
        p = jnp.exp2(s - m)
        den = jnp.sum(p, axis=-1, keepdims=True) + jnp.exp2(sinks[pair] - m)
        return p.astype(BF16), 1.0 / den

    def body(i, carry):
        group = [(i * SAMPLE_ATTN_GROUP + k, pair) for k in range(SAMPLE_ATTN_GROUP) for pair in range(N_KV_HEADS // 2)]
        scores = [keys_and_scores(e, pair) for e, pair in group]
        vt_all = [values(e, pair) for e, pair in group]
        probs = [softmax(pair, sc) for (e, pair), sc in zip(group, scores)]
        for (e, pair), (p, inv_den), vt in zip(group, probs, vt_all):
            pv = lax.dot_general(p, vt, (((1,), (1,)), ((), ())), preferred_element_type=F32)
            pv = pv * inv_den
            o_ref[e, pair * rows:(pair + 1) * rows, :] = jnp.where(low, pv[:rows], pv[rows:]).astype(BF16)
        return carry

    lax.fori_loop(0, q_ref.shape[0] // SAMPLE_ATTN_GROUP, body, 0)


def _sample_attention(sinks, q3, ckt, cvt, kvn, t_new):
    n = q3.shape[0]
    bn = SAMPLE_ATTN_BATCH
    qrows = q3.shape[1]
    cache_spec = pl.BlockSpec((bn, N_KV_HEADS, HEAD_DIM, WINDOW), lambda b: (b, 0, 0, 0))
    cache_shape = jax.ShapeDtypeStruct((n, N_KV_HEADS, HEAD_DIM, WINDOW), F32)
    return pl.pallas_call(
        functools.partial(_sample_attn_kernel, t_new=t_new),
        grid=(n // bn,),
        in_specs=[
            pl.BlockSpec(memory_space=pltpu.SMEM),
            pl.BlockSpec((bn, qrows, LANES), lambda b: (b, 0, 0)),
            cache_spec,
            cache_spec,
            pl.BlockSpec((bn, SUBLANES, 2 * KV_WIDTH), lambda b: (b, 0, 0)),
        ],
        out_specs=[pl.BlockSpec((bn, qrows, LANES), lambda b: (b, 0, 0)), cache_spec, cache_spec],
        out_shape=(jax.ShapeDtypeStruct((n, qrows, LANES), BF16), cache_shape, cache_shape),
        compiler_params=_cparams(("parallel",)),
        name="sample_attention",
    )(sinks, q3, ckt, cvt, kvn)


def _gelu_tanh(x):
    c = math.sqrt(2.0 / math.pi)
    return 0.5 * x * (1.0 + jnp.tanh(c * (x + 0.044715 * (x * x * x))))


def _ssm_kernel(u_ref, h0re_ref, h0im_ref, ar_ref, ai_ref, wb_ref, wct_ref, d_ref, wglu_ref, bglu_ref, gs_ref,
                mix_ref, hre_ref, him_ref, s_ref, *, nb, steps, n_sub):
    @pl.when(pl.program_id(0) == 0)
    def _():
        hre_ref[...] = h0re_ref[...]
        him_ref[...] = h0im_ref[...]

    dims = dict(nb=nb, steps=steps)
    pending = []

    def trace_pending_tails():
        while pending:
            y, s = pending.pop(0)
            _ssm_gate_and_store(y, wglu_ref, bglu_ref, gs_ref, mix_ref, sub=s, **dims)

    for sub in range(n_sub):
        u, ub = _ssm_permute_in(u_ref, sub=sub, **dims)
        ys = []
        for j in range(N_SSM_BLOCKS):
            if j == SSM_TAIL_LAG_BLOCKS:
                trace_pending_tails()
            ys.append(_ssm_state_block(ub, j, ar_ref, ai_ref, wb_ref, wct_ref, hre_ref, him_ref, s_ref, sub=sub, **dims))
        trace_pending_tails()
        pending.append((jnp.concatenate(ys, axis=1) + d_ref[...] * u, sub))
    trace_pending_tails()


def _ssm_positions(ref, nb, steps, sub):
    per_sub = nb * steps // ref.shape[0]
    return slice(sub * per_sub, (sub + 1) * per_sub)


def _ssm_permute_in(u_ref, *, nb, steps, sub):
    rows = nb * steps
    tm_row = lax.broadcasted_iota(jnp.int32, (rows, rows), 0)
    bm_col = lax.broadcasted_iota(jnp.int32, (rows, rows), 1)
    to_time_major = (bm_col == (tm_row % nb) * steps + tm_row // nb).astype(BF16)
    u_bm = u_ref[:, _ssm_positions(u_ref, nb, steps, sub), :].reshape(rows, SSM_WIDTH)
    u_hi = u_bm.astype(BF16)
    u_lo = (u_bm - u_hi.astype(F32)).astype(BF16)
    u_hi_tm = jnp.dot(to_time_major, u_hi, preferred_element_type=F32)
    u = u_hi_tm + jnp.dot(to_time_major, u_lo, preferred_element_type=F32)
    return u, u_hi_tm.astype(BF16)


def _ssm_state_block(ub, j, ar_ref, ai_ref, wb_ref, wct_ref, hre_ref, him_ref, s_ref, *, nb, steps, sub):
    rows = nb * steps
    row0 = (sub % (s_ref.shape[0] // rows)) * rows
    blk_cols = 2 * BLOCK_STATES
    s_ref[row0:row0 + rows, j * blk_cols:(j + 1) * blk_cols] = jnp.dot(
        ub[:, j * MXU_DIM:(j + 1) * MXU_DIM], wb_ref[j], preferred_element_type=F32)
    for part in range(BLOCK_STATES // SCAN_LANES):
        rc = j * blk_cols + part * SCAN_LANES
        ic = rc + BLOCK_STATES
        sc = j * BLOCK_STATES + part * SCAN_LANES
        a_re = ar_ref[:, sc:sc + SCAN_LANES]
        a_im = ai_ref[:, sc:sc + SCAN_LANES]
        for bg in range(nb // SUBLANES):
            b0 = bg * SUBLANES
            h_re = hre_ref[b0:b0 + SUBLANES, sc:sc + SCAN_LANES]
            h_im = him_ref[b0:b0 + SUBLANES, sc:sc + SCAN_LANES]
            for t in range(steps):
                row = row0 + t * nb + b0
                n_re = a_re * h_re - a_im * h_im + s_ref[row:row + SUBLANES, rc:rc + SCAN_LANES]
                n_im = a_re * h_im + a_im * h_re + s_ref[row:row + SUBLANES, ic:ic + SCAN_LANES]
                s_ref[row:row + SUBLANES, rc:rc + SCAN_LANES] = n_re
                s_ref[row:row + SUBLANES, ic:ic + SCAN_LANES] = n_im
                h_re, h_im = n_re, n_im
            hre_ref[b0:b0 + SUBLANES, sc:sc + SCAN_LANES] = h_re
            him_ref[b0:b0 + SUBLANES, sc:sc + SCAN_LANES] = h_im
    hb = s_ref[row0:row0 + rows, j * blk_cols:(j + 1) * blk_cols].astype(BF16)
    return lax.dot_general(hb, wct_ref[j], (((1,), (1,)), ((), ())), preferred_element_type=F32)


def _ssm_gate_and_store(y, wglu_ref, bglu_ref, gs_ref, mix_ref, *, nb, steps, sub):
    rows = nb * steps
    bm_row = lax.broadcasted_iota(jnp.int32, (rows, rows), 0)
    tm_col = lax.broadcasted_iota(jnp.int32, (rows, rows), 1)
    to_batch_major = (bm_row == (tm_col % nb) * steps + tm_col // nb).astype(BF16)
    g = _gelu_tanh(y)
    gate = jnp.dot(g.astype(BF16), wglu_ref[...], preferred_element_type=F32) + bglu_ref[...]
    so = g * jax.nn.sigmoid(gate)
    r = lax.rsqrt(jnp.mean(so * so, axis=-1, keepdims=True) + EPS)
    mix_tm = (so * r * gs_ref[...]).astype(BF16)
    mix_bm = jnp.dot(to_batch_major, mix_tm, preferred_element_type=F32).astype(BF16)
    mix_ref[:, _ssm_positions(mix_ref, nb, steps, sub), :] = mix_bm.reshape(
        mix_ref.shape[0], rows // mix_ref.shape[0], SSM_WIDTH)


def _ssm_mixer(u3, h0_re, h0_im, ar8, ai8, wb, wct, d, wglu_b, bglu, gs, nb, steps, n_sub):
    nbv, seq, _ = u3.shape
    blk_rows = n_sub * nb * steps // nbv
    assert blk_rows % SUBLANES == 0 and seq % blk_rows == 0
    cols = 2 * N_STATE
    tr = min(n_sub, 2) * nb * steps
    return pl.pallas_call(
        functools.partial(_ssm_kernel, nb=nb, steps=steps, n_sub=n_sub),
        grid=(seq // blk_rows,),
        in_specs=[
            pl.BlockSpec((nbv, blk_rows, SSM_WIDTH), lambda i: (0, i, 0)),
            _const_spec((nb, N_STATE)),
            _const_spec((nb, N_STATE)),
            _const_spec((SUBLANES, N_STATE)),
            _const_spec((SUBLANES, N_STATE)),
            _const_spec((N_SSM_BLOCKS, MXU_DIM, 2 * BLOCK_STATES)),
            _const_spec((N_SSM_BLOCKS, MXU_DIM, 2 * BLOCK_STATES)),
            _const_spec((1, SSM_WIDTH)),
            _const_spec((SSM_WIDTH, SSM_WIDTH)),
            _const_spec((1, SSM_WIDTH)),
            _const_spec((1, SSM_WIDTH)),
        ],
        out_specs=[
            pl.BlockSpec((nbv, blk_rows, SSM_WIDTH), lambda i: (0, i, 0)),
            pl.BlockSpec((nb, N_STATE), lambda i: (0, 0)),
            pl.BlockSpec((nb, N_STATE), lambda i: (0, 0)),
        ],
        out_shape=(
            jax.ShapeDtypeStruct((nbv, seq, SSM_WIDTH), BF16),
            jax.ShapeDtypeStruct((nb, N_STATE), F32),
            jax.ShapeDtypeStruct((nb, N_STATE), F32),
        ),
        scratch_shapes=[pltpu.VMEM((tr, cols), F32)],
        compiler_params=_cparams(("arbitrary",)),
        name="ssm_mixer",
    )(u3, h0_re, h0_im, ar8, ai8, wb, wct, d, wglu_b, bglu, gs)


def _out_mlp_kernel(x_ref, a_ref, ga_ref, ms_ref, wo_ref, gm_ref, wup_ref, wdn_ref, y_ref, hn_ref):
    @pl.when(pl.program_id(1) == 0)
    def _():
        half_rows = x_ref.shape[0] // 2
        for half in range(2):
            rs = slice(half * half_rows, (half + 1) * half_rows)
            h = x_ref[rs, :] + jnp.dot(ms_ref[rs, :], wo_ref[ATTN_WIDTH:, :], preferred_element_type=F32)
            a = a_ref[rs, :].astype(F32)
            ra = lax.rsqrt(jnp.mean(a * a, axis=-1, keepdims=True) + EPS)
            ma = (a * ra * ga_ref[...]).astype(BF16)
            h = h + jnp.dot(ma, wo_ref[:ATTN_WIDTH, :], preferred_element_type=F32)
            y_ref[rs, :] = h
            rh = lax.rsqrt(jnp.mean(h * h, axis=-1, keepdims=True) + EPS)
            hn_ref[rs, :] = (h * rh * gm_ref[...]).astype(BF16)

    t = jnp.dot(hn_ref[...], wup_ref[...], preferred_element_type=F32)
    t = jnp.maximum(t, 0.0)
    t = (t * t).astype(BF16)
    y_ref[...] += jnp.dot(t, wdn_ref[...], preferred_element_type=F32)


def _out_mlp(x2d, attn, ga, mix_s, wo, gm, wup, wdn, tm, tf):
    rows = x2d.shape[0]
    n_ff = D_FF // tf

    def ff_tile(i, j):
        return jnp.where(i % 2 == 0, j, n_ff - 1 - j)

    return pl.pallas_call(
        _out_mlp_kernel,
        grid=(rows // tm, n_ff),
        in_specs=[
            pl.BlockSpec((tm, D_MODEL), lambda i, j: (i, 0)),
            pl.BlockSpec((tm, ATTN_WIDTH), lambda i, j: (i, 0)),
            _const_spec((1, ATTN_WIDTH)),
            pl.BlockSpec((tm, SSM_WIDTH), lambda i, j: (i, 0)),
            _const_spec((D_MODEL, D_MODEL)),
            _const_spec((1, D_MODEL)),
            pl.BlockSpec((D_MODEL, tf), lambda i, j: (0, ff_tile(i, j))),
            pl.BlockSpec((tf, D_MODEL), lambda i, j: (ff_tile(i, j), 0)),
        ],
        out_specs=pl.BlockSpec((tm, D_MODEL), lambda i, j: (i, 0)),
        out_shape=jax.ShapeDtypeStruct((rows, D_MODEL), F32),
        scratch_shapes=[pltpu.VMEM((tm, D_MODEL), BF16)],
        compiler_params=_cparams(("parallel", "arbitrary")),
        name="out_mlp",
    )(x2d, attn, ga, mix_s, wo, gm, wup, wdn)


def _pair_heads(a, axis):
    shape = a.shape
    split = shape[:axis] + (N_KV_HEADS // 2, 2, Q_PER_KV, HEAD_DIM) + shape[axis + 1:]
    return jnp.swapaxes(a.reshape(split), axis + 1, axis + 2).reshape(shape)


def _layer(x, cache_k, cache_v, h0_re, h0_im, p, side):
    n, t = x.shape[:2]
    rows = n * t
    tm = ROW_TILE
    x2d = x.reshape(rows, D_MODEL)
    if side is None:
        assert D_MODEL // (rows // tm) == HEAD_DIM
        q, kv, u, *side = _in_proj(x2d, p['gn'], p['w_in'], p['gq'], p['gk'], tm,
                                   to_bf16=((p['w_up_f32'], None), (p['w_down_f32'], None),
                                            (p['w_out_f32'], _paired_head_slab), (p['w_glu_f32'], None)))
    else:
        q, kv, u = _in_proj(x2d, p['gn'], p['w_in'], p['gq'], p['gk'], tm)
    w_up, w_down, w_out_b, w_glu_b = side
    kv3 = kv.reshape(n, t, 2 * KV_WIDTH)

    if cache_k is None:
        attn = _prompt_attention(p['sinks'], q, kv, n, t)
        tail = kv3[:, t - WINDOW:]
        new_k = tail[..., :KV_WIDTH].reshape(n, WINDOW, N_KV_HEADS, HEAD_DIM)
        new_v = tail[..., KV_WIDTH:].reshape(n, WINDOW, N_KV_HEADS, HEAD_DIM)
        u3 = u.reshape(n, t, SSM_WIDTH)
        ssm_steps, ssm_subs = SSM_SUB_TILE_ROWS // n, SSM_SUB_TILES
    else:
        nblk = ATTN_WIDTH // LANES
        q3 = q.reshape(n, t, nblk, LANES).transpose(0, 2, 1, 3).reshape(n, nblk * t, LANES)
        kvn = jnp.pad(kv3, ((0, 0), (SUBLANES - t, 0), (0, 0)))
        ckt = cache_k.transpose(0, 2, 3, 1)
        cvt = cache_v.transpose(0, 2, 3, 1)
        a3, nkt, nvt = _sample_attention(p['sinks'], q3, ckt, cvt, kvn, t)
        attn = a3.reshape(n, nblk, t, LANES).transpose(0, 2, 1, 3).reshape(rows, ATTN_WIDTH)
        new_k = nkt.transpose(0, 3, 1, 2)
        new_v = nvt.transpose(0, 3, 1, 2)
        u3 = u.reshape(1, rows, SSM_WIDTH)
        ssm_steps, ssm_subs = t, 1

    mix3, h_re, h_im = _ssm_mixer(u3, h0_re.reshape(n, N_STATE), h0_im.reshape(n, N_STATE), p['ar8'], p['ai8'],
                                  p['wb'], p['wct'], p['d'], w_glu_b, p['b_glu'], p['gs'], n, ssm_steps, ssm_subs)
    mix_s = mix3.reshape(rows, SSM_WIDTH)
    h_re = h_re.reshape(n, N_SSM_GROUPS, STATE_DIM)
    h_im = h_im.reshape(n, N_SSM_GROUPS, STATE_DIM)

    y = _out_mlp(x2d, attn, p['ga'], mix_s, w_out_b, p['gm'], w_up, w_down, tm, MLP_FF_TILE)
    return y.reshape(n, t, D_MODEL), new_k, new_v, h_re, h_im, side


def _prepare_params(l, attn_norm_g, w_in, q_norm_g, k_norm_g, attn_sinks,
                    ssm_A_re, ssm_A_im, ssm_log_dt, ssm_B_re, ssm_B_im, ssm_C_re, ssm_C_im, ssm_D,
                    w_glu, b_glu, attn_out_g, ssm_out_g, w_out, mlp_norm_g, w_mlp_up, w_mlp_down):
    w_in_l = w_in[l]
    heads_per_blk = MXU_DIM // HEAD_DIM

    a_re = ssm_A_re[l].reshape(1, N_STATE)
    a_im = ssm_A_im[l].reshape(1, N_STATE)
    ldt = jnp.broadcast_to(ssm_log_dt[l][:, None], (N_SSM_GROUPS, STATE_DIM)).reshape(1, N_STATE)
    b_re = ssm_B_re[l].transpose(2, 0, 1).reshape(SSM_GROUP, N_STATE)
    b_im = ssm_B_im[l].transpose(2, 0, 1).reshape(SSM_GROUP, N_STATE)
    c_re = ssm_C_re[l].transpose(1, 0, 2).reshape(SSM_GROUP, N_STATE)
    c_im = ssm_C_im[l].transpose(1, 0, 2).reshape(SSM_GROUP, N_STATE)
    ar8, ai8, wb, wct = _ssm_discretize(a_re, a_im, ldt, b_re, b_im, c_re, c_im)

    return dict(
        gn=attn_norm_g[l].reshape(1, D_MODEL),
        w_in=w_in_l.astype(BF16),
        gq=jnp.tile(q_norm_g[l], heads_per_blk).reshape(1, MXU_DIM),
        gk=jnp.tile(k_norm_g[l], heads_per_blk).reshape(1, MXU_DIM),
        sinks=attn_sinks[l].astype(F32),
        ar8=ar8, ai8=ai8, wb=wb, wct=wct,
        d=ssm_D[l].reshape(1, SSM_WIDTH),
        w_glu_f32=w_glu[l],
        b_glu=b_glu[l].reshape(1, SSM_WIDTH),
        gs=ssm_out_g[l].reshape(1, SSM_WIDTH),
        ga=_pair_heads(attn_out_g[l], 0).reshape(1, ATTN_WIDTH),
        w_out_f32=w_out[l],
        gm=mlp_norm_g[l].reshape(1, D_MODEL),
        w_up_f32=w_mlp_up[l],
        w_down_f32=w_mlp_down[l],
    )


def kernel(x_prompt, x_sample, cache_k, cache_v, state_ssm_re, state_ssm_im, attn_norm_g, w_in, q_norm_g, k_norm_g, attn_sinks, ssm_A_re, ssm_A_im, ssm_log_dt, ssm_B_re, ssm_B_im, ssm_C_re, ssm_C_im, ssm_D, w_glu, b_glu, attn_out_g, ssm_out_g, w_out, mlp_norm_g, w_mlp_up, w_mlp_down):
    depth = w_in.shape[0]
    xp, xs = x_prompt, x_sample
    zeros_state = jnp.zeros((x_prompt.shape[0], N_SSM_GROUPS, STATE_DIM), F32)
    outs = [[] for _ in range(8)]
    for l in range(depth):
        p = _prepare_params(l, attn_norm_g, w_in, q_norm_g, k_norm_g, attn_sinks,
                            ssm_A_re, ssm_A_im, ssm_log_dt, ssm_B_re, ssm_B_im, ssm_C_re, ssm_C_im, ssm_D,
                            w_glu, b_glu, attn_out_g, ssm_out_g, w_out, mlp_norm_g, w_mlp_up, w_mlp_down)
        xp, kp, vp, hrp, hip, side = _layer(xp, None, None, zeros_state, zeros_state, p, None)
        xs, ks, vs, hrs, his, _ = _layer(xs, cache_k[l], cache_v[l], state_ssm_re[l], state_ssm_im[l], p, side)
        for lst, val in zip(outs, (kp, vp, hrp, hip, ks, vs, hrs, his)):
            lst.append(val)
    return (xp, xs) + tuple(jnp.stack(o) for o in outs)
```

```python
import functools
import math

import jax
import jax.numpy as jnp
from jax import lax
from jax.experimental import pallas as pl
from jax.experimental.pallas import tpu as pltpu

D_MODEL = 2048
ATTN_WIDTH = 1024
SSM_WIDTH = 1024
HEAD_DIM = 64
N_HEADS = 16
N_KV_HEADS = 4
Q_PER_KV = 4
KV_WIDTH = 256
WINDOW = 128
SSM_GROUP = 16
N_SSM_GROUPS = 64
STATE_DIM = 64
N_STATE = N_SSM_GROUPS * STATE_DIM
D_FF = 8192
PROJ_WIDTH = ATTN_WIDTH + 2 * KV_WIDTH + SSM_WIDTH
EPS = 1e-6
NEG_INF = -1e30
LOG2E = 1.0 / math.log(2.0)

LANES = 128
SUBLANES = 8
MXU_DIM = 256
VMEM_LIMIT = 56 * 1024 * 1024

GROUPS_PER_BLOCK = MXU_DIM // SSM_GROUP
N_SSM_BLOCKS = N_SSM_GROUPS // GROUPS_PER_BLOCK
BLOCK_STATES = GROUPS_PER_BLOCK * STATE_DIM
SCAN_LANES = 512
SSM_SUB_TILE_ROWS = 256
SSM_SUB_TILES = 2
SSM_TAIL_LAG_BLOCKS = 4
ROW_TILE = 512
MLP_FF_TILE = 1024
IN_PROJ_ROW_CHUNKS = 2

F32 = jnp.float32
BF16 = jnp.bfloat16


def _cparams(sem):
    return pltpu.CompilerParams(dimension_semantics=sem, vmem_limit_bytes=VMEM_LIMIT)


def _const_spec(shape):
    nd = len(shape)
    return pl.BlockSpec(shape, lambda *_: (0,) * nd, pipeline_mode=pl.Buffered(1))


def _discretize_kernel(are_ref, aim_ref, ldt_ref, bre_ref, bim_ref, cre_ref, cim_ref,
                       abr_ref, abi_ref, wb_ref, wct_ref):
    a_re = are_ref[...]
    a_im = aim_ref[...]
    dt = jnp.exp(ldt_ref[...])
    mag = jnp.exp(a_re * dt)
    ab_re = mag * jnp.cos(a_im * dt)
    ab_im = mag * jnp.sin(a_im * dt)
    abr_ref[...] = jnp.broadcast_to(ab_re, abr_ref.shape)
    abi_ref[...] = jnp.broadcast_to(ab_im, abi_ref.shape)
    x = ab_re - 1.0
    y = ab_im
    den = a_re * a_re + a_im * a_im
    k_re = (x * a_re + y * a_im) / den
    k_im = (y * a_re - x * a_im) / den
    b_re = bre_ref[...]
    b_im = bim_ref[...]
    bb_re = k_re * b_re - k_im * b_im
    bb_im = k_re * b_im + k_im * b_re

    rows = GROUPS_PER_BLOCK * SSM_GROUP
    row_group = lax.broadcasted_iota(jnp.int32, (rows, BLOCK_STATES), 0) // SSM_GROUP
    col_group = lax.broadcasted_iota(jnp.int32, (rows, BLOCK_STATES), 1) // STATE_DIM
    same_group = row_group == col_group

    def block_diag(m, j):
        blk = m[:, j * BLOCK_STATES:(j + 1) * BLOCK_STATES]
        return jnp.where(same_group, jnp.concatenate([blk] * GROUPS_PER_BLOCK, axis=0), 0.0).astype(BF16)

    c_re = cre_ref[...]
    c_im_neg = -cim_ref[...]
    for j in range(N_SSM_BLOCKS):
        wb_ref[j, :, :BLOCK_STATES] = block_diag(bb_re, j)
        wb_ref[j, :, BLOCK_STATES:] = block_diag(bb_im, j)
        wct_ref[j, :, :BLOCK_STATES] = block_diag(c_re, j)
        wct_ref[j, :, BLOCK_STATES:] = block_diag(c_im_neg, j)


def _ssm_discretize(a_re, a_im, log_dt, b_re, b_im, c_re, c_im):
    rep = jax.ShapeDtypeStruct((SUBLANES, N_STATE), F32)
    blocks = jax.ShapeDtypeStruct((N_SSM_BLOCKS, GROUPS_PER_BLOCK * SSM_GROUP, 2 * BLOCK_STATES), BF16)
    return pl.pallas_call(
        _discretize_kernel,
        out_shape=(rep, rep, blocks, blocks),
        name="ssm_discretize",
    )(a_re, a_im, log_dt, b_re, b_im, c_re, c_im)


def _head_rmsnorm(zc, gain, ones_blk):
    sq = (zc * zc).astype(BF16)
    ss = jnp.dot(sq, ones_blk, preferred_element_type=F32)
    return zc * lax.rsqrt(ss * (1.0 / HEAD_DIM) + EPS) * gain


def _in_proj_kernel(x_ref, gn_ref, w_ref, gq_ref, gk_ref, *rest):
    n_cast = (len(rest) - 3) // 2
    cast_in = rest[:n_cast]
    q_ref, kv_ref, u_ref = rest[n_cast:n_cast + 3]
    cast_out = rest[n_cast + 3:]
    for src, dst in zip(cast_in, cast_out):
        dst[...] = src[...].astype(BF16)
    ri = lax.broadcasted_iota(jnp.int32, (MXU_DIM, MXU_DIM), 0) // HEAD_DIM
    ci = lax.broadcasted_iota(jnp.int32, (MXU_DIM, MXU_DIM), 1) // HEAD_DIM
    ones_blk = (ri == ci).astype(BF16)
    gq = gq_ref[...]
    chunk = x_ref.shape[0] // IN_PROJ_ROW_CHUNKS
    low = lax.broadcasted_iota(jnp.int32, (chunk, LANES), 1) < HEAD_DIM
    for ck in range(IN_PROJ_ROW_CHUNKS):
        rs = slice(ck * chunk, (ck + 1) * chunk)
        x = x_ref[rs, :]
        r = lax.rsqrt(jnp.mean(x * x, axis=-1, keepdims=True) + EPS)
        xn = (x * r * gn_ref[...]).astype(BF16)
        zq = jnp.dot(xn, w_ref[:, :ATTN_WIDTH], preferred_element_type=F32)
        zr = jnp.dot(xn, w_ref[:, ATTN_WIDTH:], preferred_element_type=F32)
        lane_blocks = []
        for c in range(ATTN_WIDTH // MXU_DIM):
            qn = _head_rmsnorm(zq[:, c * MXU_DIM:(c + 1) * MXU_DIM], gq, ones_blk) * (HEAD_DIM ** -0.5 * LOG2E)
            lane_blocks += [qn[:, :LANES], qn[:, LANES:]]
        for pair in range(N_KV_HEADS // 2):
            for r in range(Q_PER_KV):
                first = lane_blocks[((2 * pair) * Q_PER_KV + r) // 2]
                second = lane_blocks[((2 * pair + 1) * Q_PER_KV + r) // 2]
                if r % 2 == 0:
                    blk = jnp.where(low, first, pltpu.roll(second, HEAD_DIM, axis=1))
                else:
                    blk = jnp.where(low, pltpu.roll(first, HEAD_DIM, axis=1), second)
                dst = pair * Q_PER_KV + r
                q_ref[rs, dst * LANES:(dst + 1) * LANES] = blk.astype(BF16)
        kv_ref[rs, :KV_WIDTH] = _head_rmsnorm(zr[:, :KV_WIDTH], gk_ref[...], ones_blk)
        kv_ref[rs, KV_WIDTH:] = zr[:, KV_WIDTH:2 * KV_WIDTH]
        u_ref[rs, :] = zr[:, 2 * KV_WIDTH:]


def _paired_head_slab(i):
    g, r = i // Q_PER_KV, i % Q_PER_KV
    paired = (g // 2) * (2 * Q_PER_KV) + r * 2 + g % 2
    return jnp.where(i < N_HEADS, paired, i)


def _in_proj(x2d, gn, w_in_b, gq, gk, tm, to_bf16=()):
    rows = x2d.shape[0]
    steps = rows // tm
    mats = [w for w, _ in to_bf16]
    assert all(w.shape[0] % (steps * 2 * SUBLANES) == 0 for w in mats)
    slab_in = [pl.BlockSpec((w.shape[0] // steps, w.shape[1]), lambda i: (i, 0)) for w in mats]
    slab_out = [pl.BlockSpec((w.shape[0] // steps, w.shape[1]),
                             (lambda i: (i, 0)) if place is None else (lambda i, place=place: (place(i), 0)))
                for w, place in to_bf16]
    return pl.pallas_call(
        _in_proj_kernel,
        grid=(steps,),
        in_specs=[
            pl.BlockSpec((tm, D_MODEL), lambda i: (i, 0)),
            _const_spec((1, D_MODEL)),
            _const_spec((D_MODEL, PROJ_WIDTH)),
            _const_spec((1, MXU_DIM)),
            _const_spec((1, MXU_DIM)),
        ] + slab_in,
        out_specs=[
            pl.BlockSpec((tm, ATTN_WIDTH), lambda i: (i, 0)),
            pl.BlockSpec((tm, 2 * KV_WIDTH), lambda i: (i, 0)),
            pl.BlockSpec((tm, SSM_WIDTH), lambda i: (i, 0)),
        ] + slab_out,
        out_shape=(
            jax.ShapeDtypeStruct((rows, ATTN_WIDTH), BF16),
            jax.ShapeDtypeStruct((rows, 2 * KV_WIDTH), F32),
            jax.ShapeDtypeStruct((rows, SSM_WIDTH), F32),
        ) + tuple(jax.ShapeDtypeStruct(w.shape, BF16) for w in mats),
        compiler_params=_cparams(("arbitrary",)),
        name="in_proj",
    )(x2d, gn, w_in_b, gq, gk, *mats)


def _alibi_slope(head):
    return 2.0 ** (-8.0 * (head + 1) / N_HEADS)


ATTN_BLOCKS_PER_STEP = 16


def _prompt_attn_kernel(sink_ref, q_ref, kvp_ref, kvc_ref, bias_ref, o_ref):
    tq = WINDOW
    lane = lax.broadcasted_iota(jnp.int32, (tq, LANES), 1)
    low = lane < HEAD_DIM
    zero = jnp.zeros((tq, LANES), BF16)

    def keys_or_values(blk, lanes):
        own = kvc_ref[blk * tq:(blk + 1) * tq, lanes]
        prev = kvp_ref[:, lanes] if blk == 0 else kvc_ref[(blk - 1) * tq:blk * tq, lanes]
        return jnp.concatenate([prev, own], axis=0).astype(BF16)

    for blk in range(q_ref.shape[0] // tq):
        rows = slice(blk * tq, (blk + 1) * tq)
        has_prev = (pl.program_id(1) > 0).astype(jnp.int32) if blk == 0 else 1
        for pair in range(N_KV_HEADS // 2):
            kc = keys_or_values(blk, slice(pair * LANES, (pair + 1) * LANES))
            vc = keys_or_values(blk, slice(KV_WIDTH + pair * LANES, KV_WIDTH + (pair + 1) * LANES))
            qb = [q_ref[rows, (pair * Q_PER_KV + r) * LANES:(pair * Q_PER_KV + r + 1) * LANES]
                  for r in range(Q_PER_KV)]
            qs = jnp.concatenate([jnp.where(low, b, zero) for b in qb] + [jnp.where(low, zero, b) for b in qb],
                                 axis=0)
            scores = lax.dot_general(qs, kc, (((1,), (1,)), ((), ())), preferred_element_type=F32)
            probs = []
            inv_den = []
            for hh in range(2 * Q_PER_KV):
                head = (2 * pair + hh // Q_PER_KV) * Q_PER_KV + hh % Q_PER_KV
                s = scores[hh * tq:(hh + 1) * tq] + bias_ref[has_prev, head]
                sink = sink_ref[head] * LOG2E
                m = jnp.maximum(jnp.max(s, axis=-1, keepdims=True), sink)
                p = jnp.exp2(s - m)
                den = jnp.sum(p, axis=-1, keepdims=True) + jnp.exp2(sink - m)
                probs.append(p.astype(BF16))
                inv_den.append(1.0 / den)
            pv = jnp.dot(jnp.concatenate(probs, axis=0), vc, preferred_element_type=F32)
            for r in range(Q_PER_KV):
                o_low = pv[r * tq:(r + 1) * tq] * inv_den[r]
                o_high = pv[(Q_PER_KV + r) * tq:(Q_PER_KV + r + 1) * tq] * inv_den[Q_PER_KV + r]
                o_ref[rows, (pair * Q_PER_KV + r) * LANES:(pair * Q_PER_KV + r + 1) * LANES] = (
                    jnp.where(low, o_low, o_high).astype(BF16))


def _prompt_attn_bias():
    w = WINDOW
    slope = jnp.exp2(-8.0 * (jnp.arange(N_HEADS, dtype=F32) + 1.0) / N_HEADS)
    i = jnp.arange(w)[:, None]
    j = jnp.arange(2 * w)[None, :]
    d = i + w - j
    ok = (d >= 0) & (d <= w)
    ok = jnp.stack([ok & (j >= w), ok])
    return jnp.where(ok[:, None], -LOG2E * slope[None, :, None, None] * d.astype(F32)[None, None], NEG_INF)


def _prompt_attention(sinks, q, kv, batch, seq):
    per = ATTN_BLOCKS_PER_STEP
    tq = per * WINDOW
    nb = seq // tq
    bias = _prompt_attn_bias()
    return pl.pallas_call(
        _prompt_attn_kernel,
        grid=(batch, nb),
        in_specs=[
            pl.BlockSpec(memory_space=pltpu.SMEM),
            pl.BlockSpec((tq, ATTN_WIDTH), lambda b, n: (b * nb + n, 0)),
            pl.BlockSpec((WINDOW, 2 * KV_WIDTH), lambda b, n: (b * nb * per + jnp.maximum(n * per - 1, 0), 0)),
            pl.BlockSpec((tq, 2 * KV_WIDTH), lambda b, n: (b * nb + n, 0)),
            _const_spec(bias.shape),
        ],
        out_specs=pl.BlockSpec((tq, ATTN_WIDTH), lambda b, n: (b * nb + n, 0)),
        out_shape=jax.ShapeDtypeStruct((batch * seq, ATTN_WIDTH), BF16),
        compiler_params=_cparams(("parallel", "parallel")),
        name="prompt_attention",
    )(sinks, q, kv, kv, bias)


SAMPLE_ATTN_BATCH = 16
SAMPLE_ATTN_GROUP = 2


def _sample_attn_kernel(sink_ref, q_ref, ckt_ref, cvt_ref, kvn_ref, o_ref, nkt_ref, nvt_ref, *, t_new):
    rows = Q_PER_KV * t_new
    tk = 2 * WINDOW
    i = lax.broadcasted_iota(jnp.int32, (rows, tk), 0) % t_new
    c = lax.broadcasted_iota(jnp.int32, (rows, tk), 1)
    is_new = c >= tk - t_new
    j = jnp.where(c < WINDOW, c, c - (WINDOW - t_new))
    d = i + WINDOW - j
    valid = (d >= 0) & (d <= WINDOW) & ((c < WINDOW) | is_new)
    delta = d.astype(F32)
    lane_q = lax.broadcasted_iota(jnp.int32, (rows, LANES), 1)
    low = lane_q < HEAD_DIM
    zero_q = jnp.zeros((rows, LANES), BF16)
    lane_w = lax.broadcasted_iota(jnp.int32, (WINDOW, LANES), 1)
    tail = lane_w >= WINDOW - t_new
    rid = lax.broadcasted_iota(jnp.int32, (rows, 1), 0) // t_new
    alibi, sinks = [], []
    for pair in range(N_KV_HEADS // 2):
        sl, sk = [], []
        for half in range(2):
            slope = jnp.zeros((rows, 1), F32)
            sink = jnp.zeros((rows, 1), F32)
            for r in range(Q_PER_KV):
                head = (2 * pair + half) * Q_PER_KV + r
                slope = jnp.where(rid == r, _alibi_slope(head) * LOG2E, slope)
                sink = jnp.where(rid == r, sink_ref[head] * LOG2E, sink)
            sl.append(slope)
            sk.append(sink)
        alibi.append(jnp.concatenate(sl, axis=0) * jnp.concatenate([delta, delta], axis=0))
        sinks.append(jnp.concatenate(sk, axis=0))
    valid2 = jnp.concatenate([valid, valid], axis=0)

    lead_zero = jnp.zeros((WINDOW - SUBLANES, LANES), F32)

    def new_rows_transposed(e, lanes):
        return jnp.concatenate([lead_zero, kvn_ref[e, :, lanes]], axis=0).T

    def shifted(old_t, new_t):
        return jnp.where(tail, new_t, pltpu.roll(old_t, WINDOW - t_new, axis=1))

    def keys_and_scores(e, pair):
        heads = slice(2 * pair, 2 * pair + 2)
        kt = ckt_ref[e, heads].reshape(2 * HEAD_DIM, WINDOW)
        knt = new_rows_transposed(e, slice(pair * LANES, (pair + 1) * LANES))
        nkt_ref[e, heads] = shifted(kt, knt).reshape(2, HEAD_DIM, WINDOW)
        qp = q_ref[e, pair * rows:(pair + 1) * rows, :]
        qs = jnp.concatenate([jnp.where(low, qp, zero_q), jnp.where(low, zero_q, qp)], axis=0)
        kt_all = jnp.concatenate([kt, knt], axis=1).astype(BF16)
        return jnp.dot(qs, kt_all, preferred_element_type=F32)

    def values(e, pair):
        heads = slice(2 * pair, 2 * pair + 2)
        vt = cvt_ref[e, heads].reshape(2 * HEAD_DIM, WINDOW)
        vnt = new_rows_transposed(e, slice(KV_WIDTH + pair * LANES, KV_WIDTH + (pair + 1) * LANES))
        nvt_ref[e, heads] = shifted(vt, vnt).reshape(2, HEAD_DIM, WINDOW)
        return jnp.concatenate([vt, vnt], axis=1).astype(BF16)

    def softmax(pair, scores):
        s = jnp.where(valid2, scores - alibi[pair], NEG_INF)
        m = jnp.maximum(jnp.max(s, axis=-1, keepdims=True), sinks[pair])
        p = jnp.exp2(s - m)
        den = jnp.sum(p, axis=-1, keepdims=True) + jnp.exp2(sinks[pair] - m)
        return p.astype(BF16), 1.0 / den

    def body(i, carry):
        group = [(i * SAMPLE_ATTN_GROUP + k, pair) for k in range(SAMPLE_ATTN_GROUP) for pair in range(N_KV_HEADS // 2)]
        scores = [keys_and_scores(e, pair) for e, pair in group]
        vt_all = [values(e, pair) for e, pair in group]
        probs = [softmax(pair, sc) for (e, pair), sc in zip(group, scores)]
        for (e, pair), (p, inv_den), vt in zip(group, probs, vt_all):
            pv = lax.dot_general(p, vt, (((1,), (1,)), ((), ())), preferred_element_type=F32)
            pv = pv * inv_den
            o_ref[e, pair * rows:(pair + 1) * rows, :] = jnp.where(low, pv[:rows], pv[rows:]).astype(BF16)
        return carry

    lax.fori_loop(0, q_ref.shape[0] // SAMPLE_ATTN_GROUP, body, 0)


def _sample_attention(sinks, q3, ckt, cvt, kvn, t_new):
    n = q3.shape[0]
    bn = SAMPLE_ATTN_BATCH
    qrows = q3.shape[1]
    cache_spec = pl.BlockSpec((bn, N_KV_HEADS, HEAD_DIM, WINDOW), lambda b: (b, 0, 0, 0))
    cache_shape = jax.ShapeDtypeStruct((n, N_KV_HEADS, HEAD_DIM, WINDOW), F32)
    return pl.pallas_call(
        functools.partial(_sample_attn_kernel, t_new=t_new),
        grid=(n // bn,),
        in_specs=[
            pl.BlockSpec(memory_space=pltpu.SMEM),
            pl.BlockSpec((bn, qrows, LANES), lambda b: (b, 0, 0)),
            cache_spec,
            cache_spec,
            pl.BlockSpec((bn, SUBLANES, 2 * KV_WIDTH), lambda b: (b, 0, 0)),
        ],
        out_specs=[pl.BlockSpec((bn, qrows, LANES), lambda b: (b, 0, 0)), cache_spec, cache_spec],
        out_shape=(jax.ShapeDtypeStruct((n, qrows, LANES), BF16), cache_shape, cache_shape),
        compiler_params=_cparams(("parallel",)),
        name="sample_attention",
    )(sinks, q3, ckt, cvt, kvn)


def _gelu_tanh(x):
    c = math.sqrt(2.0 / math.pi)
    return 0.5 * x * (1.0 + jnp.tanh(c * (x + 0.044715 * (x * x * x))))


def _ssm_kernel(u_ref, h0re_ref, h0im_ref, ar_ref, ai_ref, wb_ref, wct_ref, d_ref, wglu_ref, bglu_ref, gs_ref,
                mix_ref, hre_ref, him_ref, s_ref, *, nb, steps, n_sub):
    @pl.when(pl.program_id(0) == 0)
    def _():
        hre_ref[...] = h0re_ref[...]
        him_ref[...] = h0im_ref[...]

    dims = dict(nb=nb, steps=steps)
    pending = []

    def trace_pending_tails():
        while pending:
            y, s = pending.pop(0)
            _ssm_gate_and_store(y, wglu_ref, bglu_ref, gs_ref, mix_ref, sub=s, **dims)

    for sub in range(n_sub):
        u, ub = _ssm_permute_in(u_ref, sub=sub, **dims)
        ys = []
        for j in range(N_SSM_BLOCKS):
            if j == SSM_TAIL_LAG_BLOCKS:
                trace_pending_tails()
            ys.append(_ssm_state_block(ub, j, ar_ref, ai_ref, wb_ref, wct_ref, hre_ref, him_ref, s_ref, sub=sub, **dims))
        trace_pending_tails()
        pending.append((jnp.concatenate(ys, axis=1) + d_ref[...] * u, sub))
    trace_pending_tails()


def _ssm_positions(ref, nb, steps, sub):
    per_sub = nb * steps // ref.shape[0]
    return slice(sub * per_sub, (sub + 1) * per_sub)


def _ssm_permute_in(u_ref, *, nb, steps, sub):
    rows = nb * steps
    tm_row = lax.broadcasted_iota(jnp.int32, (rows, rows), 0)
    bm_col = lax.broadcasted_iota(jnp.int32, (rows, rows), 1)
    to_time_major = (bm_col == (tm_row % nb) * steps + tm_row // nb).astype(BF16)
    u_bm = u_ref[:, _ssm_positions(u_ref, nb, steps, sub), :].reshape(rows, SSM_WIDTH)
    u_hi = u_bm.astype(BF16)
    u_lo = (u_bm - u_hi.astype(F32)).astype(BF16)
    u_hi_tm = jnp.dot(to_time_major, u_hi, preferred_element_type=F32)
    u = u_hi_tm + jnp.dot(to_time_major, u_lo, preferred_element_type=F32)
    return u, u_hi_tm.astype(BF16)


def _ssm_state_block(ub, j, ar_ref, ai_ref, wb_ref, wct_ref, hre_ref, him_ref, s_ref, *, nb, steps, sub):
    rows = nb * steps
    row0 = (sub % (s_ref.shape[0] // rows)) * rows
    blk_cols = 2 * BLOCK_STATES
    s_ref[row0:row0 + rows, j * blk_cols:(j + 1) * blk_cols] = jnp.dot(
        ub[:, j * MXU_DIM:(j + 1) * MXU_DIM], wb_ref[j], preferred_element_type=F32)
    for part in range(BLOCK_STATES // SCAN_LANES):
        rc = j * blk_cols + part * SCAN_LANES
        ic = rc + BLOCK_STATES
        sc = j * BLOCK_STATES + part * SCAN_LANES
        a_re = ar_ref[:, sc:sc + SCAN_LANES]
        a_im = ai_ref[:, sc:sc + SCAN_LANES]
        for bg in range(nb // SUBLANES):
            b0 = bg * SUBLANES
            h_re = hre_ref[b0:b0 + SUBLANES, sc:sc + SCAN_LANES]
            h_im = him_ref[b0:b0 + SUBLANES, sc:sc + SCAN_LANES]
            for t in range(steps):
                row = row0 + t * nb + b0
                n_re = a_re * h_re - a_im * h_im + s_ref[row:row + SUBLANES, rc:rc + SCAN_LANES]
                n_im = a_re * h_im + a_im * h_re + s_ref[row:row + SUBLANES, ic:ic + SCAN_LANES]
                s_ref[row:row + SUBLANES, rc:rc + SCAN_LANES] = n_re
                s_ref[row:row + SUBLANES, ic:ic + SCAN_LANES] = n_im
                h_re, h_im = n_re, n_im
            hre_ref[b0:b0 + SUBLANES, sc:sc + SCAN_LANES] = h_re
            him_ref[b0:b0 + SUBLANES, sc:sc + SCAN_LANES] = h_im
    hb = s_ref[row0:row0 + rows, j * blk_cols:(j + 1) * blk_cols].astype(BF16)
    return lax.dot_general(hb, wct_ref[j], (((1,), (1,)), ((), ())), preferred_element_type=F32)


def _ssm_gate_and_store(y, wglu_ref, bglu_ref, gs_ref, mix_ref, *, nb, steps, sub):
    rows = nb * steps
    bm_row = lax.broadcasted_iota(jnp.int32, (rows, rows), 0)
    tm_col = lax.broadcasted_iota(jnp.int32, (rows, rows), 1)
    to_batch_major = (bm_row == (tm_col % nb) * steps + tm_col // nb).astype(BF16)
    g = _gelu_tanh(y)
    gate = jnp.dot(g.astype(BF16), wglu_ref[...], preferred_element_type=F32) + bglu_ref[...]
    so = g * jax.nn.sigmoid(gate)
    r = lax.rsqrt(jnp.mean(so * so, axis=-1, keepdims=True) + EPS)
    mix_tm = (so * r * gs_ref[...]).astype(BF16)
    mix_bm = jnp.dot(to_batch_major, mix_tm, preferred_element_type=F32).astype(BF16)
    mix_ref[:, _ssm_positions(mix_ref, nb, steps, sub), :] = mix_bm.reshape(
        mix_ref.shape[0], rows // mix_ref.shape[0], SSM_WIDTH)


def _ssm_mixer(u3, h0_re, h0_im, ar8, ai8, wb, wct, d, wglu_b, bglu, gs, nb, steps, n_sub):
    nbv, seq, _ = u3.shape
    blk_rows = n_sub * nb * steps // nbv
    assert blk_rows % SUBLANES == 0 and seq % blk_rows == 0
    cols = 2 * N_STATE
    tr = min(n_sub, 2) * nb * steps
    return pl.pallas_call(
        functools.partial(_ssm_kernel, nb=nb, steps=steps, n_sub=n_sub),
        grid=(seq // blk_rows,),
        in_specs=[
            pl.BlockSpec((nbv, blk_rows, SSM_WIDTH), lambda i: (0, i, 0)),
            _const_spec((nb, N_STATE)),
            _const_spec((nb, N_STATE)),
            _const_spec((SUBLANES, N_STATE)),
            _const_spec((SUBLANES, N_STATE)),
            _const_spec((N_SSM_BLOCKS, MXU_DIM, 2 * BLOCK_STATES)),
            _const_spec((N_SSM_BLOCKS, MXU_DIM, 2 * BLOCK_STATES)),
            _const_spec((1, SSM_WIDTH)),
            _const_spec((SSM_WIDTH, SSM_WIDTH)),
            _const_spec((1, SSM_WIDTH)),
            _const_spec((1, SSM_WIDTH)),
        ],
        out_specs=[
            pl.BlockSpec((nbv, blk_rows, SSM_WIDTH), lambda i: (0, i, 0)),
            pl.BlockSpec((nb, N_STATE), lambda i: (0, 0)),
            pl.BlockSpec((nb, N_STATE), lambda i: (0, 0)),
        ],
        out_shape=(
            jax.ShapeDtypeStruct((nbv, seq, SSM_WIDTH), BF16),
            jax.ShapeDtypeStruct((nb, N_STATE), F32),
            jax.ShapeDtypeStruct((nb, N_STATE), F32),
        ),
        scratch_shapes=[pltpu.VMEM((tr, cols), F32)],
        compiler_params=_cparams(("arbitrary",)),
        name="ssm_mixer",
    )(u3, h0_re, h0_im, ar8, ai8, wb, wct, d, wglu_b, bglu, gs)


def _out_mlp_kernel(x_ref, a_ref, ga_ref, ms_ref, wo_ref, gm_ref, wup_ref, wdn_ref, y_ref, hn_ref):
    @pl.when(pl.program_id(1) == 0)
    def _():
        half_rows = x_ref.shape[0] // 2
        for half in range(2):
            rs = slice(half * half_rows, (half + 1) * half_rows)
            h = x_ref[rs, :] + jnp.dot(ms_ref[rs, :], wo_ref[ATTN_WIDTH:, :], preferred_element_type=F32)
            a = a_ref[rs, :].astype(F32)
            ra = lax.rsqrt(jnp.mean(a * a, axis=-1, keepdims=True) + EPS)
            ma = (a * ra * ga_ref[...]).astype(BF16)
            h = h + jnp.dot(ma, wo_ref[:ATTN_WIDTH, :], preferred_element_type=F32)
            y_ref[rs, :] = h
            rh = lax.rsqrt(jnp.mean(h * h, axis=-1, keepdims=True) + EPS)
            hn_ref[rs, :] = (h * rh * gm_ref[...]).astype(BF16)

    t = jnp.dot(hn_ref[...], wup_ref[...], preferred_element_type=F32)
    t = jnp.maximum(t, 0.0)
    t = (t * t).astype(BF16)
    y_ref[...] += jnp.dot(t, wdn_ref[...], preferred_element_type=F32)


def _out_mlp(x2d, attn, ga, mix_s, wo, gm, wup, wdn, tm, tf):
    rows = x2d.shape[0]
    n_ff = D_FF // tf

    def ff_tile(i, j):
        return jnp.where(i % 2 == 0, j, n_ff - 1 - j)

    return pl.pallas_call(
        _out_mlp_kernel,
        grid=(rows // tm, n_ff),
        in_specs=[
            pl.BlockSpec((tm, D_MODEL), lambda i, j: (i, 0)),
            pl.BlockSpec((tm, ATTN_WIDTH), lambda i, j: (i, 0)),
            _const_spec((1, ATTN_WIDTH)),
            pl.BlockSpec((tm, SSM_WIDTH), lambda i, j: (i, 0)),
            _const_spec((D_MODEL, D_MODEL)),
            _const_spec((1, D_MODEL)),
            pl.BlockSpec((D_MODEL, tf), lambda i, j: (0, ff_tile(i, j))),
            pl.BlockSpec((tf, D_MODEL), lambda i, j: (ff_tile(i, j), 0)),
        ],
        out_specs=pl.BlockSpec((tm, D_MODEL), lambda i, j: (i, 0)),
        out_shape=jax.ShapeDtypeStruct((rows, D_MODEL), F32),
        scratch_shapes=[pltpu.VMEM((tm, D_MODEL), BF16)],
        compiler_params=_cparams(("parallel", "arbitrary")),
        name="out_mlp",
    )(x2d, attn, ga, mix_s, wo, gm, wup, wdn)


def _pair_heads(a, axis):
    shape = a.shape
    split = shape[:axis] + (N_KV_HEADS // 2, 2, Q_PER_KV, HEAD_DIM) + shape[axis + 1:]
    return jnp.swapaxes(a.reshape(split), axis + 1, axis + 2).reshape(shape)


def _layer(x, cache_k, cache_v, h0_re, h0_im, p, side):
    n, t = x.shape[:2]
    rows = n * t
    tm = ROW_TILE
    x2d = x.reshape(rows, D_MODEL)
    if side is None:
        assert D_MODEL // (rows // tm) == HEAD_DIM
        q, kv, u, *side = _in_proj(x2d, p['gn'], p['w_in'], p['gq'], p['gk'], tm,
                                   to_bf16=((p['w_up_f32'], None), (p['w_down_f32'], None),
                                            (p['w_out_f32'], _paired_head_slab), (p['w_glu_f32'], None)))
    else:
        q, kv, u = _in_proj(x2d, p['gn'], p['w_in'], p['gq'], p['gk'], tm)
    w_up, w_down, w_out_b, w_glu_b = side
    kv3 = kv.reshape(n, t, 2 * KV_WIDTH)

    if cache_k is None:
        attn = _prompt_attention(p['sinks'], q, kv, n, t)
        tail = kv3[:, t - WINDOW:]
        new_k = tail[..., :KV_WIDTH].reshape(n, WINDOW, N_KV_HEADS, HEAD_DIM)
        new_v = tail[..., KV_WIDTH:].reshape(n, WINDOW, N_KV_HEADS, HEAD_DIM)
        u3 = u.reshape(n, t, SSM_WIDTH)
        ssm_steps, ssm_subs = SSM_SUB_TILE_ROWS // n, SSM_SUB_TILES
    else:
        nblk = ATTN_WIDTH // LANES
        q3 = q.reshape(n, t, nblk, LANES).transpose(0, 2, 1, 3).reshape(n, nblk * t, LANES)
        kvn = jnp.pad(kv3, ((0, 0), (SUBLANES - t, 0), (0, 0)))
        ckt = cache_k.transpose(0, 2, 3, 1)
        cvt = cache_v.transpose(0, 2, 3, 1)
        a3, nkt, nvt = _sample_attention(p['sinks'], q3, ckt, cvt, kvn, t)
        attn = a3.reshape(n, nblk, t, LANES).transpose(0, 2, 1, 3).reshape(rows, ATTN_WIDTH)
        new_k = nkt.transpose(0, 3, 1, 2)
        new_v = nvt.transpose(0, 3, 1, 2)
        u3 = u.reshape(1, rows, SSM_WIDTH)
        ssm_steps, ssm_subs = t, 1

    mix3, h_re, h_im = _ssm_mixer(u3, h0_re.reshape(n, N_STATE), h0_im.reshape(n, N_STATE), p['ar8'], p['ai8'],
                                  p['wb'], p['wct'], p['d'], w_glu_b, p['b_glu'], p['gs'], n, ssm_steps, ssm_subs)
    mix_s = mix3.reshape(rows, SSM_WIDTH)
    h_re = h_re.reshape(n, N_SSM_GROUPS, STATE_DIM)
    h_im = h_im.reshape(n, N_SSM_GROUPS, STATE_DIM)

    y = _out_mlp(x2d, attn, p['ga'], mix_s, w_out_b, p['gm'], w_up, w_down, tm, MLP_FF_TILE)
    return y.reshape(n, t, D_MODEL), new_k, new_v, h_re, h_im, side


def _prepare_params(l, attn_norm_g, w_in, q_norm_g, k_norm_g, attn_sinks,
                    ssm_A_re, ssm_A_im, ssm_log_dt, ssm_B_re, ssm_B_im, ssm_C_re, ssm_C_im, ssm_D,
                    w_glu, b_glu, attn_out_g, ssm_out_g, w_out, mlp_norm_g, w_mlp_up, w_mlp_down):
    w_in_l = w_in[l]
    heads_per_blk = MXU_DIM // HEAD_DIM

    a_re = ssm_A_re[l].reshape(1, N_STATE)
    a_im = ssm_A_im[l].reshape(1, N_STATE)
    ldt = jnp.broadcast_to(ssm_log_dt[l][:, None], (N_SSM_GROUPS, STATE_DIM)).reshape(1, N_STATE)
    b_re = ssm_B_re[l].transpose(2, 0, 1).reshape(SSM_GROUP, N_STATE)
    b_im = ssm_B_im[l].transpose(2, 0, 1).reshape(SSM_GROUP, N_STATE)
    c_re = ssm_C_re[l].transpose(1, 0, 2).reshape(SSM_GROUP, N_STATE)
    c_im = ssm_C_im[l].transpose(1, 0, 2).reshape(SSM_GROUP, N_STATE)
    ar8, ai8, wb, wct = _ssm_discretize(a_re, a_im, ldt, b_re, b_im, c_re, c_im)

    return dict(
        gn=attn_norm_g[l].reshape(1, D_MODEL),
        w_in=w_in_l.astype(BF16),
        gq=jnp.tile(q_norm_g[l], heads_per_blk).reshape(1, MXU_DIM),
        gk=jnp.tile(k_norm_g[l], heads_per_blk).reshape(1, MXU_DIM),
        sinks=attn_sinks[l].astype(F32),
        ar8=ar8, ai8=ai8, wb=wb, wct=wct,
        d=ssm_D[l].reshape(1, SSM_WIDTH),
        w_glu_f32=w_glu[l],
        b_glu=b_glu[l].reshape(1, SSM_WIDTH),
        gs=ssm_out_g[l].reshape(1, SSM_WIDTH),
        ga=_pair_heads(attn_out_g[l], 0).reshape(1, ATTN_WIDTH),
        w_out_f32=w_out[l],
        gm=mlp_norm_g[l].reshape(1, D_MODEL),
        w_up_f32=w_mlp_up[l],
        w_down_f32=w_mlp_down[l],
    )


def kernel(x_prompt, x_sample, cache_k, cache_v, state_ssm_re, state_ssm_im, attn_norm_g, w_in, q_norm_g, k_norm_g, attn_sinks, ssm_A_re, ssm_A_im, ssm_log_dt, ssm_B_re, ssm_B_im, ssm_C_re, ssm_C_im, ssm_D, w_glu, b_glu, attn_out_g, ssm_out_g, w_out, mlp_norm_g, w_mlp_up, w_mlp_down):
    depth = w_in.shape[0]
    xp, xs = x_prompt, x_sample
    zeros_state = jnp.zeros((x_prompt.shape[0], N_SSM_GROUPS, STATE_DIM), F32)
    outs = [[] for _ in range(8)]
    for l in range(depth):
        p = _prepare_params(l, attn_norm_g, w_in, q_norm_g, k_norm_g, attn_sinks,
                            ssm_A_re, ssm_A_im, ssm_log_dt, ssm_B_re, ssm_B_im, ssm_C_re, ssm_C_im, ssm_D,
                            w_glu, b_glu, attn_out_g, ssm_out_g, w_out, mlp_norm_g, w_mlp_up, w_mlp_down)
        xp, kp, vp, hrp, hip, side = _layer(xp, None, None, zeros_state, zeros_state, p, None)
        xs, ks, vs, hrs, his, _ = _layer(xs, cache_k[l], cache_v[l], state_ssm_re[l], state_ssm_im[l], p, side)
        for lst, val in zip(outs, (kp, vp, hrp, hip, ks, vs, hrs, his)):
            lst.append(val)
    return (xp, xs) + tuple(jnp.stack(o) for o in outs)
```
